```python
import jax
import jax.numpy as jnp
from jax import lax
import numpy as np

D_MODEL = 1024
BATCH = 4
SEQ = 4096
DEPTH = 1

HGRN_HEADS = 8
HGRN_DK = D_MODEL // HGRN_HEADS
HGRN_DV = D_MODEL // HGRN_HEADS
HGRN_CHUNK = 64
MOBA_HEADS = 8
MOBA_HD = D_MODEL // MOBA_HEADS
MOBA_BLOCK = 256
MOBA_TOPK = 3
MOBA_QCHUNK = 16
ROPE_THETA = 10000.0
D_FF = ((8 * D_MODEL + 3 * 256 - 1) // (3 * 256)) * 256
N_PROJ = 2 * HGRN_HEADS * HGRN_DK + 2 * HGRN_HEADS * HGRN_DV + 3 * MOBA_HEADS * MOBA_HD + 2 * D_MODEL
DN_ALPHA = (2.0 * DEPTH) ** 0.25
DN_BETA = (8.0 * DEPTH) ** -0.25
LN_EPS = 1e-5
RMS_EPS = 1e-6

kernel_name = 'hybrid_hgrn2_moba_deepnorm'


def layer_norm(x, w, b):
    xf = x.astype(jnp.float32)
    mu = jnp.mean(xf, axis=-1, keepdims=True)
    var = jnp.mean(jnp.square(xf - mu), axis=-1, keepdims=True)
    y = (xf - mu) * lax.rsqrt(var + LN_EPS) * w.astype(jnp.float32) + b.astype(jnp.float32)
    return y.astype(x.dtype)


def rms_norm(x, w):
    xf = x.astype(jnp.float32)
    y = xf * lax.rsqrt(jnp.mean(jnp.square(xf), axis=-1, keepdims=True) + RMS_EPS)
    return y * w.astype(jnp.float32)


def apply_rope(t):
    S, hd = t.shape[1], t.shape[-1]
    half = hd // 2
    inv_freq = ROPE_THETA ** (-jnp.arange(half, dtype=jnp.float32) / half)
    ang = jnp.arange(S, dtype=jnp.float32)[:, None] * inv_freq[None, :]
    cos = jnp.cos(ang)[None, :, None, :]
    sin = jnp.sin(ang)[None, :, None, :]
    tf = t.astype(jnp.float32)
    t1, t2 = tf[..., :half], tf[..., half:]
    return jnp.concatenate([t1 * cos - t2 * sin, t2 * cos + t1 * sin], axis=-1).astype(t.dtype)


def hgrn2_mixer(q, f_logit, inp, lb):
    B, S, H, dk = q.shape
    dv = inp.shape[-1]
    C = HGRN_CHUNK
    nc = S // C
    lb = lb.reshape(H, dk)
    z = f_logit.astype(jnp.float32)
    log_f = jnp.logaddexp(jnp.log(lb), jnp.log1p(-lb) + jax.nn.log_sigmoid(z))
    k = (1.0 - lb) * jax.nn.sigmoid(-z)
    qf = jax.nn.silu(q.astype(jnp.float32))
    v = inp.astype(jnp.float32)

    def to_chunks(t):
        return t.reshape(B, nc, C, H, t.shape[-1]).transpose(1, 0, 3, 2, 4)

    causal = jnp.tril(jnp.ones((C, C), dtype=bool))[:, :, None]

    def step(state, xs):
        qc, kc, vc, lfc = xs
        G = jnp.cumsum(lfc, axis=2)
        diff = G[:, :, :, None, :] - G[:, :, None, :, :]
        decay = jnp.exp(jnp.where(causal, diff, -jnp.inf))
        scores = jnp.einsum('bhtsd,bhtd,bhsd->bhts', decay, qc, kc)
        o = jnp.einsum('bhts,bhsv->bhtv', scores, vc) + jnp.einsum('bhtd,bhdv->bhtv', qc * jnp.exp(G), state)
        g_last = G[:, :, -1, :]
        new_state = jnp.exp(g_last)[..., None] * state + jnp.einsum(
            'bhsd,bhsv->bhdv', kc * jnp.exp(g_last[:, :, None, :] - G), vc)
        return new_state, o

    s0 = jnp.zeros((B, H, dk, dv), jnp.float32)
    _, o = lax.scan(step, s0, (to_chunks(qf), to_chunks(k), to_chunks(v), to_chunks(log_f)))
    return o.transpose(1, 0, 3, 2, 4).reshape(B, S, H, dv)


def moba_attention(q, k, v):
    B, S, H, hd = q.shape
    BLK = MOBA_BLOCK
    QC = MOBA_QCHUNK
    nb = -(-S // BLK)
    Sp = nb * BLK
    nq = S // QC
    topk = min(MOBA_TOPK, nb)
    scale = hd ** -0.5
    qh = q.transpose(0, 2, 1, 3)
    pad = ((0, 0), (0, 0), (0, Sp - S), (0, 0))
    kb = jnp.pad(k.transpose(0, 2, 1, 3), pad).reshape(B, H, nb, BLK, hd)
    vb = jnp.pad(v.transpose(0, 2, 1, 3), pad).reshape(B, H, nb, BLK, hd)
    k_mean = jnp.mean(kb.astype(jnp.float32), axis=3)
    gate = jnp.einsum('bhsd,bhnd->bhsn', qh.astype(jnp.float32), k_mean)
    own = jnp.arange(S) // BLK
    past = jnp.arange(nb)[None, :] < own[:, None]
    gate = jnp.where(past, gate, -jnp.inf)
    _, sel = lax.top_k(gate, topk)

    q_ch = qh.reshape(B, H, nq, QC, hd).transpose(2, 0, 1, 3, 4)
    sel_ch = sel.reshape(B, H, nq, QC, topk).transpose(2, 0, 1, 3, 4)
    gather = jax.vmap(jax.vmap(lambda blocks, ix: blocks[ix]))

    def attend(args):
        ci, qc, sc = args
        start = ci * QC
        qpos = start + jnp.arange(QC)
        ob = start // BLK
        kpos = ob * BLK + jnp.arange(BLK)
        k_own = lax.dynamic_index_in_dim(kb, ob, axis=2, keepdims=False)
        v_own = lax.dynamic_index_in_dim(vb, ob, axis=2, keepdims=False)
        k_sel = gather(kb, sc)
        v_sel = gather(vb, sc)
        s_sel = jnp.einsum('bhqd,bhqnkd->bhqnk', qc, k_sel).astype(jnp.float32) * scale
        valid = sc < (qpos // BLK)[:, None]
        s_sel = jnp.where(valid[..., None], s_sel, -jnp.inf).reshape(B, H, QC, topk * BLK)
        s_own = jnp.einsum('bhqd,bhkd->bhqk', qc, k_own).astype(jnp.float32) * scale
        s_own = jnp.where(kpos[None, :] <= qpos[:, None], s_own, -jnp.inf)
        p = jax.nn.softmax(jnp.concatenate([s_sel, s_own], axis=-1), axis=-1).astype(qc.dtype)
        p_sel = p[..., :topk * BLK].reshape(B, H, QC, topk, BLK)
        p_own = p[..., topk * BLK:]
        return (jnp.einsum('bhqnk,bhqnkd->bhqd', p_sel, v_sel)
                + jnp.einsum('bhqk,bhkd->bhqd', p_own, v_own))

    o = lax.map(attend, (jnp.arange(nq), q_ch, sel_ch))
    return o.transpose(1, 0, 3, 2, 4).reshape(B, S, H, hd)


def hybrid_layer(x, w_in, lb, hgrn_norm_w, w_branch_a, w_branch_b, b_gate, w_out,
                 ln1_w, ln1_b, w_ffn_in, w_ffn_down, ln2_w, ln2_b):
    B, S, D = x.shape
    proj = jnp.einsum('bsd,dn->bsn', x, w_in)
    sizes = (HGRN_HEADS * HGRN_DK, HGRN_HEADS * HGRN_DK, HGRN_HEADS * HGRN_DV, HGRN_HEADS * HGRN_DV,
             MOBA_HEADS * MOBA_HD, MOBA_HEADS * MOBA_HD, MOBA_HEADS * MOBA_HD, 2 * D_MODEL)
    offsets = np.cumsum(sizes)[:-1].tolist()
    hq, hf, hi, hg, mq, mk, mv, gate_logits = jnp.split(proj, offsets, axis=-1)

    o_a = hgrn2_mixer(hq.reshape(B, S, HGRN_HEADS, HGRN_DK), hf.reshape(B, S, HGRN_HEADS, HGRN_DK),
                      hi.reshape(B, S, HGRN_HEADS, HGRN_DV), lb)
    y_a = rms_norm(o_a, hgrn_norm_w.reshape(HGRN_HEADS, HGRN_DV)) * jax.nn.silu(
        hg.reshape(B, S, HGRN_HEADS, HGRN_DV).astype(jnp.float32))
    y_a = y_a.reshape(B, S, HGRN_HEADS * HGRN_DV).astype(x.dtype)

    q_b = apply_rope(mq.reshape(B, S, MOBA_HEADS, MOBA_HD))
    k_b = apply_rope(mk.reshape(B, S, MOBA_HEADS, MOBA_HD))
    y_b = moba_attention(q_b, k_b, mv.reshape(B, S, MOBA_HEADS, MOBA_HD)).reshape(B, S, MOBA_HEADS * MOBA_HD)

    z_a = jnp.einsum('bsc,cd->bsd', y_a, w_branch_a)
    z_b = jnp.einsum('bsc,cd->bsd', y_b, w_branch_b)
    g_a, g_b = jnp.split(jax.nn.sigmoid(gate_logits + b_gate), 2, axis=-1)
    mixed = jnp.einsum('bsc,cd->bsd', g_a * z_a + g_b * z_b, w_out)
    x = layer_norm(DN_ALPHA * x + mixed, ln1_w, ln1_b)

    h = jnp.einsum('bsd,df->bsf', x, w_ffn_in)
    h_gate, h_up = jnp.split(h, 2, axis=-1)
    y = jnp.einsum('bsf,fd->bsd', jax.nn.silu(h_gate) * h_up, w_ffn_down)
    return layer_norm(DN_ALPHA * x + y, ln2_w, ln2_b)


def setup_inputs(seed: int = 0) -> dict:
    key = jax.random.key(seed)
    ks = jax.random.split(key, 16)

    def nrm(k, shape, scale):
        return jax.random.normal(k, shape, jnp.float32) * scale

    d_a = HGRN_HEADS * HGRN_DV
    d_b = MOBA_HEADS * MOBA_HD
    return {
        'x': nrm(ks[0], (BATCH, SEQ, D_MODEL), 1.0),
        'w_in': nrm(ks[1], (DEPTH, D_MODEL, N_PROJ), D_MODEL ** -0.5),
        'lb_logits': nrm(ks[2], (DEPTH + 1, HGRN_HEADS * HGRN_DK), 0.1),
        'hgrn_norm_w': 1.0 + nrm(ks[3], (DEPTH, d_a), 0.02),
        'w_branch_a': nrm(ks[4], (DEPTH, d_a, D_MODEL), d_a ** -0.5),
        'w_branch_b': nrm(ks[5], (DEPTH, d_b, D_MODEL), d_b ** -0.5),
        'b_gate': nrm(ks[6], (DEPTH, 2 * D_MODEL), 0.1),
        'w_out': nrm(ks[7], (DEPTH, D_MODEL, D_MODEL), D_MODEL ** -0.5 * DN_BETA),
        'ln1_w': 1.0 + nrm(ks[8], (DEPTH, D_MODEL), 0.02),
        'ln1_b': nrm(ks[9], (DEPTH, D_MODEL), 0.02),
        'w_ffn_in': nrm(ks[10], (DEPTH, D_MODEL, 2 * D_FF), D_MODEL ** -0.5),
        'w_ffn_down': nrm(ks[11], (DEPTH, D_FF, D_MODEL), D_FF ** -0.5 * DN_BETA),
        'ln2_w': 1.0 + nrm(ks[12], (DEPTH, D_MODEL), 0.02),
        'ln2_b': nrm(ks[13], (DEPTH, D_MODEL), 0.02),
    }


def reference(x, w_in, lb_logits, hgrn_norm_w, w_branch_a, w_branch_b, b_gate, w_out,
              ln1_w, ln1_b, w_ffn_in, w_ffn_down, ln2_w, ln2_b):
    lower_bounds = jnp.cumsum(jax.nn.softmax(lb_logits.astype(jnp.float32), axis=0), axis=0)
    h = x
    for l in range(DEPTH):
        h = hybrid_layer(h, w_in[l], lower_bounds[l], hgrn_norm_w[l], w_branch_a[l], w_branch_b[l],
                         b_gate[l], w_out[l], ln1_w[l], ln1_b[l], w_ffn_in[l], w_ffn_down[l],
                         ln2_w[l], ln2_b[l])
    return h
```

```python
import functools

import numpy as np
import jax
import jax.numpy as jnp
from jax import lax
from jax.experimental import pallas as pl
from jax.experimental.pallas import tpu as pltpu

D_MODEL = 1024
BATCH = 4
SEQ = 4096
DEPTH = 1
HEADS = 8
HEAD_DIM = 128
HGRN_CHUNK = 64
HGRN_LEVELS = 6
MOBA_BLOCK = 256
MOBA_NBLK = SEQ // MOBA_BLOCK
MOBA_TOPK = 3
ROPE_THETA = 10000.0
D_FF = 2816
DN_ALPHA = (2.0 * DEPTH) ** 0.25
LN_EPS = 1e-5
RMS_EPS = 1e-6
M_TOKENS = BATCH * SEQ

MASK_VALUE = -1e30
VMEM_LIMIT = 56 * 1024 * 1024

F32 = jnp.float32
BF16 = jnp.bfloat16
NT_DIMS = (((1,), (1,)), ((), ()))
TN_DIMS = (((0,), (0,)), ((), ()))


def _params(semantics):
    return pltpu.CompilerParams(dimension_semantics=semantics, vmem_limit_bytes=VMEM_LIMIT)


def _sigmoid(z):
    return 1.0 / (1.0 + jnp.exp(-z))


def _layer_norm(r, w, b):
    mu = jnp.mean(r, axis=-1, keepdims=True)
    d = r - mu
    var = jnp.mean(d * d, axis=-1, keepdims=True)
    return d * lax.rsqrt(var + LN_EPS) * w + b


PROJ_TM = 512


def _proj_silu_kernel(x_ref, w_ref, o_ref):
    acc = jnp.dot(x_ref[...], w_ref[...], preferred_element_type=F32)
    o_ref[...] = acc * _sigmoid(acc)


def _proj_plain_kernel(x_ref, w_ref, o_ref):
    o_ref[...] = jnp.dot(x_ref[...], w_ref[...], preferred_element_type=F32).astype(o_ref.dtype)


def _proj_forget_kernel(x_ref, w_ref, lb_ref, logf_ref, key_ref):
    z = jnp.dot(x_ref[...], w_ref[...], preferred_element_type=F32)
    lb = lb_ref[...]
    logf_ref[...] = jnp.log(lb + (1.0 - lb) * _sigmoid(z))
    key_ref[...] = (1.0 - lb) * _sigmoid(-z)


def _proj_gate_kernel(x_ref, w_ref, b_ref, o_ref):
    acc = jnp.dot(x_ref[...], w_ref[...], preferred_element_type=F32)
    o_ref[...] = _sigmoid(acc + b_ref[...])


def _rope(t, cos, sin_signed):
    outs = []
    for h in range(HEADS):
        th = t[:, h * HEAD_DIM:(h + 1) * HEAD_DIM]
        outs.append(th * cos + pltpu.roll(th, HEAD_DIM // 2, 1) * sin_signed)
    return jnp.concatenate(outs, axis=1)


def _proj_rope_kernel(x_ref, w_ref, cos_ref, sin_ref, o_ref):
    acc = jnp.dot(x_ref[...], w_ref[...], preferred_element_type=F32)
    o_ref[...] = _rope(acc, cos_ref[...], sin_ref[...])


def _proj_rope_kmean_kernel(x_ref, w_ref, cos_ref, sin_ref, o_ref, km_ref):
    acc = jnp.dot(x_ref[...], w_ref[...], preferred_element_type=F32)
    rot = _rope(acc, cos_ref[...], sin_ref[...])
    o_ref[...] = rot
    for r in range(PROJ_TM // MOBA_BLOCK):
        blk = rot[r * MOBA_BLOCK:(r + 1) * MOBA_BLOCK, :]
        km_ref[0, r:r + 1, :] = jnp.mean(blk, axis=0, keepdims=True)


def _proj_vt_kernel(x_ref, w_ref, o_ref):
    acc = jnp.dot(x_ref[...], w_ref[...], preferred_element_type=F32)
    for r in range(PROJ_TM // MOBA_BLOCK):
        o_ref[0, r] = acc[r * MOBA_BLOCK:(r + 1) * MOBA_BLOCK, :].T


def _proj_call(kernel, x, w, extra_inputs, extra_specs, out_shapes, out_specs):
    n = w.shape[1]
    return pl.pallas_call(
        kernel,
        grid=(M_TOKENS // PROJ_TM,),
        in_specs=[pl.BlockSpec((PROJ_TM, D_MODEL), lambda i: (i, 0)),
                  pl.BlockSpec((D_MODEL, n), lambda i: (0, 0))] + extra_specs,
        out_specs=out_specs,
        out_shape=out_shapes,
        compiler_params=_params(("parallel",)),
    )(x, w, *extra_inputs)


def _row_spec(n):
    return pl.BlockSpec((PROJ_TM, n), lambda i: (i, 0))


def _vec_spec(n):
    return pl.BlockSpec((1, n), lambda i: (0, 0))


def _moba_sel_kernel(q_ref, km_ref, bias_ref):
    own = pl.program_id(1)
    blk = lax.broadcasted_iota(jnp.int32, (MOBA_NBLK, MOBA_BLOCK), 0)
    eligible = blk < own
    for h in range(HEADS):
        qh = q_ref[:, h * HEAD_DIM:(h + 1) * HEAD_DIM]
        kmh = km_ref[0, :, h * HEAD_DIM:(h + 1) * HEAD_DIM]
        gate = lax.dot_general(kmh, qh, NT_DIMS, precision=lax.Precision.HIGHEST,
                               preferred_element_type=F32)
        gate = jnp.where(eligible, gate, -jnp.inf)
        rank = jnp.zeros((MOBA_NBLK, MOBA_BLOCK), F32)
        for m in range(MOBA_NBLK):
            gm = gate[m:m + 1, :]
            beats = (gm > gate) | ((gm == gate) & (m < blk))
            rank = rank + beats.astype(F32)
        selected = eligible & (rank < float(MOBA_TOPK))
        bias_ref[0, h] = jnp.where(selected, 0.0, MASK_VALUE)


def _moba_select(q_rot, k_mean):
    return pl.pallas_call(
        _moba_sel_kernel,
        grid=(BATCH, MOBA_NBLK),
        in_specs=[pl.BlockSpec((MOBA_BLOCK, D_MODEL), lambda b, i: (b * MOBA_NBLK + i, 0)),
                  pl.BlockSpec((1, MOBA_NBLK, D_MODEL), lambda b, i: (b, 0, 0))],
        out_specs=pl.BlockSpec((1, HEADS, MOBA_NBLK, MOBA_BLOCK), lambda b, i: (b, 0, 0, i)),
        out_shape=jax.ShapeDtypeStruct((BATCH, HEADS, MOBA_NBLK, SEQ), F32),
        compiler_params=_params(("parallel", "parallel")),
    )(q_rot, k_mean)


def _moba_att_kernel(q_ref, k_ref, vt_ref, bias_ref, o_ref):
    own = pl.program_id(2)
    q = q_ref[...].astype(BF16)
    scale = HEAD_DIM ** -0.5

    def scores(j):
        kj = k_ref[pl.ds(pl.multiple_of(j * MOBA_BLOCK, MOBA_BLOCK), MOBA_BLOCK), :].astype(BF16)
        return lax.dot_general(kj, q, NT_DIMS, preferred_element_type=F32) * scale

    def pv(j, p):
        return jnp.dot(vt_ref[0, j].astype(BF16), p.astype(BF16), preferred_element_type=F32)

    key_pos = lax.broadcasted_iota(jnp.int32, (MOBA_BLOCK, MOBA_BLOCK), 0)
    qry_pos = lax.broadcasted_iota(jnp.int32, (MOBA_BLOCK, MOBA_BLOCK), 1)
    s = jnp.where(key_pos <= qry_pos, scores(own), MASK_VALUE)
    m = jnp.max(s, axis=0, keepdims=True)
    p = jnp.exp(s - m)
    l = jnp.sum(p, axis=0, keepdims=True)
    acc = pv(own, p)

    def body(j, carry):
        m, l, acc = carry
        s = scores(j) + bias_ref[0, 0, pl.ds(j, 1), :]
        m_new = jnp.maximum(m, jnp.max(s, axis=0, keepdims=True))
        a = jnp.exp(m - m_new)
        p = jnp.exp(s - m_new)
        return m_new, a * l + jnp.sum(p, axis=0, keepdims=True), a * acc + pv(j, p)

    m, l, acc = lax.fori_loop(0, own, body, (m, l, acc))
    o_ref[...] = (acc / l).T.astype(o_ref.dtype)


def _moba_attention(q_rot, k_rot, v_t, bias):
    return pl.pallas_call(
        _moba_att_kernel,
        grid=(BATCH, HEADS, MOBA_NBLK),
        in_specs=[pl.BlockSpec((MOBA_BLOCK, HEAD_DIM), lambda b, h, i: (b * MOBA_NBLK + i, h)),
                  pl.BlockSpec((SEQ, HEAD_DIM), lambda b, h, i: (b, h)),
                  pl.BlockSpec((1, MOBA_NBLK, HEAD_DIM, MOBA_BLOCK), lambda b, h, i: (b, 0, h, 0)),
                  pl.BlockSpec((1, 1, MOBA_NBLK, MOBA_BLOCK), lambda b, h, i: (b, h, 0, i))],
        out_specs=pl.BlockSpec((MOBA_BLOCK, HEAD_DIM), lambda b, h, i: (b * MOBA_NBLK + i, h)),
        out_shape=jax.ShapeDtypeStruct((M_TOKENS, D_MODEL), BF16),
        compiler_params=_params(("parallel", "parallel", "arbitrary")),
    )(q_rot, k_rot, v_t, bias)


HGRN_TILE = 256


def _hgrn_constants():
    c = HGRN_CHUNK
    idx = np.arange(c)
    tri = (idx[None, :] <= idx[:, None]).astype(np.float32)
    rows = [tri]
    masks = []
    for lvl in range(HGRN_LEVELS):
        half = 1 << lvl
        ref_row = (idx // (2 * half)) * (2 * half) + half - 1
        rows.append(tri[ref_row])
        masks.append((idx[:, None] // (2 * half) == idx[None, :] // (2 * half)).astype(np.float32))
    masks.append(np.eye(c, dtype=np.float32))
    return np.concatenate(rows, axis=0), np.stack(masks)


def _hgrn_kernel(q_ref, lf_ref, k_ref, v_ref, og_ref, nw_ref, pm_ref, mask_ref, o_ref, state_ref):
    c = HGRN_CHUNK

    @pl.when(pl.program_id(2) == 0)
    def _():
        state_ref[...] = jnp.zeros_like(state_ref)

    pm = pm_ref[...]
    row = lax.broadcasted_iota(jnp.int32, (c, HEAD_DIM), 0)
    for ch in range(HGRN_TILE // c):
        rs = slice(ch * c, (ch + 1) * c)
        lf = lf_ref[rs, :]
        hi = lf.astype(BF16)
        rem = lf - hi.astype(F32)
        mid = rem.astype(BF16)
        lo = (rem - mid.astype(F32)).astype(BF16)
        sums = (jnp.dot(pm, hi, preferred_element_type=F32)
                + jnp.dot(pm, mid, preferred_element_type=F32)
                + jnp.dot(pm, lo, preferred_element_type=F32))
        g = sums[0:c]
        qf = q_ref[rs, :]
        kk = k_ref[rs, :]
        v = v_ref[rs, :].astype(BF16)

        a = lax.dot_general(qf.astype(BF16), kk.astype(BF16), NT_DIMS,
                            preferred_element_type=F32) * mask_ref[HGRN_LEVELS]
        for lvl in range(HGRN_LEVELS):
            g_ref_rows = sums[(lvl + 1) * c:(lvl + 2) * c]
            upper = (row & (1 << lvl)) != 0
            e = g - g_ref_rows
            x = jnp.exp(jnp.where(upper, e, -e))
            qt = jnp.where(upper, qf * x, 0.0).astype(BF16)
            kt = jnp.where(upper, 0.0, kk * x).astype(BF16)
            a = a + lax.dot_general(qt, kt, NT_DIMS, preferred_element_type=F32) * mask_ref[lvl]

        state_t = state_ref[...]
        o = jnp.dot(a.astype(BF16), v, preferred_element_type=F32)
        o = o + lax.dot_general((qf * jnp.exp(g)).astype(BF16), state_t.astype(BF16), NT_DIMS,
                                preferred_element_type=F32)
        g_last = g[c - 1:c, :]
        k_dec = (kk * jnp.exp(g_last - g)).astype(BF16)
        state_ref[...] = state_t * jnp.exp(g_last) + lax.dot_general(
            v, k_dec, TN_DIMS, preferred_element_type=F32)

        ms = jnp.mean(o * o, axis=-1, keepdims=True)
        y = o * lax.rsqrt(ms + RMS_EPS) * nw_ref[...] * og_ref[rs, :]
        o_ref[rs, :] = y.astype(o_ref.dtype)


def _hgrn(qg, logf, key, val, norm_w):
    pm, masks = _hgrn_constants()
    n_t = SEQ // HGRN_TILE
    tile = lambda off: pl.BlockSpec((HGRN_TILE, HEAD_DIM), lambda b, h, t: (b * n_t + t, h + off))
    return pl.pallas_call(
        _hgrn_kernel,
        grid=(BATCH, HEADS, n_t),
        in_specs=[tile(0), tile(0), tile(0), tile(0), tile(HEADS),
                  pl.BlockSpec((1, HEAD_DIM), lambda b, h, t: (0, h)),
                  pl.BlockSpec(pm.shape, lambda b, h, t: (0, 0)),
                  pl.BlockSpec(masks.shape, lambda b, h, t: (0, 0, 0))],
        out_specs=tile(0),
        out_shape=jax.ShapeDtypeStruct((M_TOKENS, D_MODEL), BF16),
        scratch_shapes=[pltpu.VMEM((HEAD_DIM, HEAD_DIM), F32)],
        compiler_params=_params(("parallel", "parallel", "arbitrary")),
    )(qg, logf, key, val, qg, norm_w, jnp.asarray(pm, BF16), jnp.asarray(masks, F32))


MIX_TM = 512


def _mix_kernel(ya_ref, yb_ref, gate_ref, x_ref, wa_ref, wb_ref, wo_ref, lnw_ref, lnb_ref,
                o_ref, ob_ref):
    za = jnp.dot(ya_ref[...], wa_ref[...], preferred_element_type=F32)
    zb = jnp.dot(yb_ref[...], wb_ref[...], preferred_element_type=F32)
    merged = gate_ref[:, :D_MODEL] * za + gate_ref[:, D_MODEL:] * zb
    mixed = jnp.dot(merged.astype(BF16), wo_ref[...], preferred_element_type=F32)
    y = _layer_norm(DN_ALPHA * x_ref[...] + mixed, lnw_ref[...], lnb_ref[...])
    o_ref[...] = y
    ob_ref[...] = y.astype(BF16)


def _mix(ya, yb, gates, x, wa, wb, wo, lnw, lnb):
    row = lambda n: pl.BlockSpec((MIX_TM, n), lambda i: (i, 0))
    full = lambda r, n: pl.BlockSpec((r, n), lambda i: (0, 0))
    return pl.pallas_call(
        _mix_kernel,
        grid=(M_TOKENS // MIX_TM,),
        in_specs=[row(D_MODEL), row(D_MODEL), row(2 * D_MODEL), row(D_MODEL),
                  full(D_MODEL, D_MODEL), full(D_MODEL, D_MODEL), full(D_MODEL, D_MODEL),
                  full(1, D_MODEL), full(1, D_MODEL)],
        out_specs=[row(D_MODEL), row(D_MODEL)],
        out_shape=[jax.ShapeDtypeStruct((M_TOKENS, D_MODEL), F32),
                   jax.ShapeDtypeStruct((M_TOKENS, D_MODEL), BF16)],
        compiler_params=_params(("parallel",)),
    )(ya, yb, gates, x, wa, wb, wo, lnw, lnb)


FFN_TM = 512
FFN_CHUNK = 1408


def _ffn_kernel(xb_ref, x_ref, wg_ref, wu_ref, wd_ref, lnw_ref, lnb_ref, o_ref):
    xb = xb_ref[...]
    y = jnp.zeros((FFN_TM, D_MODEL), F32)
    for c in range(D_FF // FFN_CHUNK):
        cs = slice(c * FFN_CHUNK, (c + 1) * FFN_CHUNK)
        hg = jnp.dot(xb, wg_ref[:, cs], preferred_element_type=F32)
        hu = jnp.dot(xb, wu_ref[:, cs], preferred_element_type=F32)
        act = (hg * _sigmoid(hg) * hu).astype(BF16)
        y = y + jnp.dot(act, wd_ref[cs, :], preferred_element_type=F32)
    o_ref[...] = _layer_norm(DN_ALPHA * x_ref[...] + y, lnw_ref[...], lnb_ref[...])


def _ffn(xb, x, wg, wu, wd, lnw, lnb):
    row = lambda n: pl.BlockSpec((FFN_TM, n), lambda i: (i, 0))
    full = lambda r, n: pl.BlockSpec((r, n), lambda i: (0, 0), pipeline_mode=pl.Buffered(1))
    return pl.pallas_call(
        _ffn_kernel,
        grid=(M_TOKENS // FFN_TM,),
        in_specs=[row(D_MODEL), row(D_MODEL), full(D_MODEL, D_FF), full(D_MODEL, D_FF),
                  full(D_FF, D_MODEL), full(1, D_MODEL), full(1, D_MODEL)],
        out_specs=row(D_MODEL),
        out_shape=jax.ShapeDtypeStruct((M_TOKENS, D_MODEL), F32),
        compiler_params=_params(("parallel",)),
    )(xb, x, wg, wu, wd, lnw, lnb)


def _rope_tables():
    half = HEAD_DIM // 2
    inv_freq = ROPE_THETA ** (-jnp.arange(half, dtype=F32) / half)
    ang = jnp.arange(SEQ, dtype=F32)[:, None] * inv_freq[None, :]
    cos, sin = jnp.cos(ang), jnp.sin(ang)
    return jnp.concatenate([cos, cos], axis=1), jnp.concatenate([-sin, sin], axis=1)


def _layer(x2, w_in, lb, hgrn_norm_w, w_branch_a, w_branch_b, b_gate, w_out,
           ln1_w, ln1_b, w_ffn_in, w_ffn_down, ln2_w, ln2_b):
    d = D_MODEL
    xb = x2.astype(BF16)
    wseg = lambda a, b: w_in[:, a:b].astype(BF16)
    row2 = lambda v: v.reshape(1, -1).astype(F32)
    shape = lambda n, dt=F32: jax.ShapeDtypeStruct((M_TOKENS, n), dt)
    tiles_per_seq = SEQ // PROJ_TM
    rope_spec = pl.BlockSpec((PROJ_TM, HEAD_DIM), lambda i: (i % tiles_per_seq, 0))
    cos, sin = _rope_tables()

    w_qg = jnp.concatenate([w_in[:, 0:d], w_in[:, 3 * d:4 * d]], axis=1).astype(BF16)
    qg = _proj_call(_proj_silu_kernel, xb, w_qg, [], [], shape(2 * d), _row_spec(2 * d))
    logf, key = _proj_call(_proj_forget_kernel, xb, wseg(d, 2 * d), [row2(lb)], [_vec_spec(d)],
                           [shape(d), shape(d)], [_row_spec(d), _row_spec(d)])
    val = _proj_call(_proj_plain_kernel, xb, wseg(2 * d, 3 * d), [], [], shape(d), _row_spec(d))
    y_a = _hgrn(qg, logf, key, val, row2(hgrn_norm_w))

    q_rot = _proj_call(_proj_rope_kernel, xb, wseg(4 * d, 5 * d), [cos, sin], [rope_spec, rope_spec],
                       shape(d), _row_spec(d))
    blocks_per_tile = PROJ_TM // MOBA_BLOCK
    k_rot, k_mean = _proj_call(
        _proj_rope_kmean_kernel, xb, wseg(5 * d, 6 * d), [cos, sin], [rope_spec, rope_spec],
        [shape(d), jax.ShapeDtypeStruct((M_TOKENS // PROJ_TM, blocks_per_tile, d), F32)],
        [_row_spec(d), pl.BlockSpec((1, blocks_per_tile, d), lambda i: (i, 0, 0))])
    k_mean = k_mean.reshape(BATCH, MOBA_NBLK, d)
    v_t = _proj_call(
        _proj_vt_kernel, xb, wseg(6 * d, 7 * d), [], [],
        jax.ShapeDtypeStruct((BATCH, MOBA_NBLK, d, MOBA_BLOCK), F32),
        pl.BlockSpec((1, blocks_per_tile, d, MOBA_BLOCK),
                     lambda i: (i // tiles_per_seq, i % tiles_per_seq, 0, 0)))
    bias = _moba_select(q_rot, k_mean)
    y_b = _moba_attention(q_rot, k_rot, v_t, bias)

    gates = _proj_call(_proj_gate_kernel, xb, wseg(7 * d, 9 * d), [row2(b_gate)], [_vec_spec(2 * d)],
                       shape(2 * d), _row_spec(2 * d))
    x1, x1b = _mix(y_a, y_b, gates, x2, w_branch_a.astype(BF16), w_branch_b.astype(BF16),
                   w_out.astype(BF16), row2(ln1_w), row2(ln1_b))
    return _ffn(x1b, x1, w_ffn_in[:, :D_FF].astype(BF16), w_ffn_in[:, D_FF:].astype(BF16),
                w_ffn_down.astype(BF16), row2(ln2_w), row2(ln2_b))


def kernel(x, w_in, lb_logits, hgrn_norm_w, w_branch_a, w_branch_b, b_gate, w_out, ln1_w, ln1_b,
           w_ffn_in, w_ffn_down, ln2_w, ln2_b):
    lower_bounds = jnp.cumsum(jax.nn.softmax(lb_logits.astype(F32), axis=0), axis=0)
    h = x.reshape(M_TOKENS, D_MODEL)
    for l in range(DEPTH):
        h = _layer(h, w_in[l], lower_bounds[l], hgrn_norm_w[l], w_branch_a[l], w_branch_b[l],
                   b_gate[l], w_out[l], ln1_w[l], ln1_b[l], w_ffn_in[l], w_ffn_down[l],
                   ln2_w[l], ln2_b[l])
    return h.reshape(BATCH, SEQ, D_MODEL)
```

```python
import functools

import numpy as np
import jax
import jax.numpy as jnp
from jax import lax
from jax.experimental import pallas as pl
from jax.experimental.pallas import tpu as pltpu

D_MODEL = 1024
BATCH = 4
SEQ = 4096
DEPTH = 1
HEADS = 8
HEAD_DIM = 128
HGRN_CHUNK = 64
HGRN_LEVELS = 6
MOBA_BLOCK = 256
MOBA_NBLK = SEQ // MOBA_BLOCK
MOBA_TOPK = 3
ROPE_THETA = 10000.0
D_FF = 2816
DN_ALPHA = (2.0 * DEPTH) ** 0.25
LN_EPS = 1e-5
RMS_EPS = 1e-6
M_TOKENS = BATCH * SEQ

MASK_VALUE = -1e30
LOG2_E = 1.4426950408889634
VMEM_LIMIT = 56 * 1024 * 1024

F32 = jnp.float32
BF16 = jnp.bfloat16
NT_DIMS = (((1,), (1,)), ((), ()))
TN_DIMS = (((0,), (0,)), ((), ()))


def _params(semantics, flags=None):
    return pltpu.CompilerParams(dimension_semantics=semantics, vmem_limit_bytes=VMEM_LIMIT,
                                flags=flags)


def _sigmoid(z):
    return 1.0 / (1.0 + jnp.exp(-z))


def _layer_norm(r, w, b):
    mu = jnp.mean(r, axis=-1, keepdims=True)
    d = r - mu
    var = jnp.mean(d * d, axis=-1, keepdims=True)
    return d * lax.rsqrt(var + LN_EPS) * w + b


PROJ_TM = 512


def _proj_silu_kernel(x_ref, w_ref, o_ref):
    acc = jnp.dot(x_ref[...], w_ref[...], preferred_element_type=F32)
    o_ref[...] = acc * _sigmoid(acc)


def _proj_plain_kernel(x_ref, w_ref, o_ref):
    o_ref[...] = jnp.dot(x_ref[...], w_ref[...], preferred_element_type=F32).astype(o_ref.dtype)


def _proj_forget_kernel(x_ref, w_ref, lb_ref, logf_ref, key_ref):
    z = jnp.dot(x_ref[...], w_ref[...], preferred_element_type=F32)
    lb = lb_ref[...]
    logf_ref[...] = jnp.log(lb + (1.0 - lb) * _sigmoid(z))
    key_ref[...] = (1.0 - lb) * _sigmoid(-z)


def _proj_gate_kernel(x_ref, w_ref, b_ref, o_ref):
    acc = jnp.dot(x_ref[...], w_ref[...], preferred_element_type=F32)
    o_ref[...] = _sigmoid(acc + b_ref[...])


def _rope(t, cos, sin_signed):
    outs = []
    for h in range(HEADS):
        th = t[:, h * HEAD_DIM:(h + 1) * HEAD_DIM]
        outs.append(th * cos + pltpu.roll(th, HEAD_DIM // 2, 1) * sin_signed)
    return jnp.concatenate(outs, axis=1)


def _proj_rope_kernel(x_ref, w_ref, cos_ref, sin_ref, o_ref):
    acc = jnp.dot(x_ref[...], w_ref[...], preferred_element_type=F32)
    o_ref[...] = _rope(acc, cos_ref[...], sin_ref[...])


def _proj_rope_kmean_kernel(x_ref, w_ref, cos_ref, sin_ref, o_ref, km_ref):
    acc = jnp.dot(x_ref[...], w_ref[...], preferred_element_type=F32)
    rot = _rope(acc, cos_ref[...], sin_ref[...])
    o_ref[...] = rot.astype(o_ref.dtype)
    for r in range(PROJ_TM // MOBA_BLOCK):
        blk = rot[r * MOBA_BLOCK:(r + 1) * MOBA_BLOCK, :]
        km_ref[0, r:r + 1, :] = jnp.mean(blk, axis=0, keepdims=True)


def _proj_vt_kernel(x_ref, w_ref, o_ref):
    acc = jnp.dot(x_ref[...], w_ref[...], preferred_element_type=F32)
    for r in range(PROJ_TM // MOBA_BLOCK):
        o_ref[0, r] = acc[r * MOBA_BLOCK:(r + 1) * MOBA_BLOCK, :].T.astype(o_ref.dtype)


def _proj_call(kernel, x, w, extra_inputs, extra_specs, out_shapes, out_specs):
    n = w.shape[1]
    return pl.pallas_call(
        kernel,
        name=kernel.__name__.strip("_").replace("_kernel", ""),
        grid=(M_TOKENS // PROJ_TM,),
        in_specs=[pl.BlockSpec((PROJ_TM, D_MODEL), lambda i: (i, 0)),
                  pl.BlockSpec((D_MODEL, n), lambda i: (0, 0))] + extra_specs,
        out_specs=out_specs,
        out_shape=out_shapes,
        compiler_params=_params(("parallel",)),
    )(x, w, *extra_inputs)


def _row_spec(n):
    return pl.BlockSpec((PROJ_TM, n), lambda i: (i, 0))


def _vec_spec(n):
    return pl.BlockSpec((1, n), lambda i: (0, 0))


def _moba_sel_kernel(q_ref, km_ref, bias_ref):
    own = pl.program_id(1)
    blk = lax.broadcasted_iota(jnp.int32, (MOBA_NBLK, MOBA_BLOCK), 0)
    eligible = blk < own
    for h in range(HEADS):
        qh = q_ref[:, h * HEAD_DIM:(h + 1) * HEAD_DIM]
        kmh = km_ref[0, :, h * HEAD_DIM:(h + 1) * HEAD_DIM]
        gate = lax.dot_general(kmh, qh, NT_DIMS, precision=lax.Precision.HIGHEST,
                               preferred_element_type=F32)
        gate = jnp.where(eligible, gate, -jnp.inf)
        rank = jnp.zeros((MOBA_NBLK, MOBA_BLOCK), F32)
        for m in range(MOBA_NBLK):
            gm = gate[m:m + 1, :]
            beats = (gm > gate) | ((gm == gate) & (m < blk))
            rank = rank + beats.astype(F32)
        selected = eligible & (rank < float(MOBA_TOPK))
        bias_ref[0, h] = jnp.where(selected, 0.0, MASK_VALUE)


def _moba_select(q_rot, k_mean):
    return pl.pallas_call(
        _moba_sel_kernel,
        grid=(BATCH, MOBA_NBLK),
        in_specs=[pl.BlockSpec((MOBA_BLOCK, D_MODEL), lambda b, i: (b * MOBA_NBLK + i, 0)),
                  pl.BlockSpec((1, MOBA_NBLK, D_MODEL), lambda b, i: (b, 0, 0))],
        out_specs=pl.BlockSpec((1, HEADS, MOBA_NBLK, MOBA_BLOCK), lambda b, i: (b, 0, 0, i)),
        out_shape=jax.ShapeDtypeStruct((BATCH, HEADS, MOBA_NBLK, SEQ), F32),
        compiler_params=_params(("parallel", "parallel")),
        name="moba_sel",
    )(q_rot, k_mean)


def _moba_att_kernel(q_ref, k_ref, vt_ref, bias_ref, o_ref, qs_ref, acc_ref, s_ref):
    own = pl.program_id(1)
    qs_ref[...] = (q_ref[...] * (HEAD_DIM ** -0.5 * LOG2_E)).astype(BF16)

    def head(h):
        return slice(h * HEAD_DIM, (h + 1) * HEAD_DIM)

    def scores(h, j):
        kj = k_ref[pl.ds(pl.multiple_of(j * MOBA_BLOCK, MOBA_BLOCK), MOBA_BLOCK), head(h)]
        return lax.dot_general(kj, qs_ref[:, head(h)], NT_DIMS, preferred_element_type=F32)

    def pv(h, j, p):
        return jnp.dot(vt_ref[0, j, head(h), :], p.astype(BF16), preferred_element_type=F32)

    key_pos = lax.broadcasted_iota(jnp.int32, (MOBA_BLOCK, MOBA_BLOCK), 0)
    qry_pos = lax.broadcasted_iota(jnp.int32, (MOBA_BLOCK, MOBA_BLOCK), 1)
    causal = key_pos <= qry_pos
    last = jnp.maximum(own - 1, 0)
    ms, ls = [], []
    for h in range(HEADS):
        s = jnp.where(causal, scores(h, own), MASK_VALUE)
        m = jnp.max(s, axis=0, keepdims=True)
        p = jnp.exp2(s - m)
        s_ref[h] = scores(h, 0)
        ms.append(m)
        ls.append(jnp.sum(p, axis=0, keepdims=True))
        acc_ref[h] = pv(h, own, p)

    def body(j, carry):
        ms, ls = carry
        new_ms, new_ls = [], []
        nxt = jnp.minimum(j + 1, last)
        for h in range(HEADS):
            s = s_ref[h] + bias_ref[0, h, pl.ds(j, 1), :]
            m_new = jnp.maximum(ms[h], jnp.max(s, axis=0, keepdims=True))
            a = jnp.exp2(ms[h] - m_new)
            p = jnp.exp2(s - m_new)
            s_ref[h] = scores(h, nxt)
            new_ms.append(m_new)
            new_ls.append(a * ls[h] + jnp.sum(p, axis=0, keepdims=True))
            acc_ref[h] = a * acc_ref[h] + pv(h, j, p)
        return tuple(new_ms), tuple(new_ls)

    ms, ls = lax.fori_loop(0, own, body, (tuple(ms), tuple(ls)))
    for h in range(HEADS):
        o_ref[:, head(h)] = (acc_ref[h] * (1.0 / ls[h])).T.astype(o_ref.dtype)


def _moba_attention(q_rot, k_rot, v_t, bias):
    return pl.pallas_call(
        _moba_att_kernel,
        grid=(BATCH, MOBA_NBLK),
        in_specs=[pl.BlockSpec((MOBA_BLOCK, D_MODEL), lambda b, i: (b * MOBA_NBLK + i, 0)),
                  pl.BlockSpec((SEQ, D_MODEL), lambda b, i: (b, 0)),
                  pl.BlockSpec((1, MOBA_NBLK, D_MODEL, MOBA_BLOCK), lambda b, i: (b, 0, 0, 0)),
                  pl.BlockSpec((1, HEADS, MOBA_NBLK, MOBA_BLOCK), lambda b, i: (b, 0, 0, i))],
        out_specs=pl.BlockSpec((MOBA_BLOCK, D_MODEL), lambda b, i: (b * MOBA_NBLK + i, 0)),
        out_shape=jax.ShapeDtypeStruct((M_TOKENS, D_MODEL), BF16),
        scratch_shapes=[pltpu.VMEM((MOBA_BLOCK, D_MODEL), BF16),
                        pltpu.VMEM((HEADS, HEAD_DIM, MOBA_BLOCK), F32),
                        pltpu.VMEM((HEADS, MOBA_BLOCK, MOBA_BLOCK), F32)],
        compiler_params=_params(("parallel", "arbitrary")),
        name="moba_att",
    )(q_rot, k_rot, v_t, bias)


HGRN_TILE = 256
HGRN_GROUP = 2


def _hgrn_constants():
    c = HGRN_CHUNK
    idx = np.arange(c)
    tri = (idx[None, :] <= idx[:, None]).astype(np.float32)
    rows = [tri]
    masks = []
    for lvl in range(HGRN_LEVELS):
        half = 1 << lvl
        ref_row = (idx // (2 * half)) * (2 * half) + half - 1
        rows.append(tri[ref_row])
        masks.append((idx[:, None] // (2 * half) == idx[None, :] // (2 * half)).astype(np.float32))
    masks.append(np.eye(c, dtype=np.float32))
    return np.concatenate(rows, axis=0), np.stack(masks)


def _hgrn_kernel(q_ref, lf_ref, k_ref, v_ref, og_ref, nw_ref, pm_ref, mask_ref, o_ref, state_ref):
    c = HGRN_CHUNK

    @pl.when(pl.program_id(2) == 0)
    def _():
        state_ref[...] = jnp.zeros_like(state_ref)

    n_ch = HGRN_TILE // c
    units = [(ch, h) for ch in range(n_ch) for h in range(HGRN_GROUP)]
    rows = lambda ch: slice(ch * c, (ch + 1) * c)
    lanes = lambda h: slice(h * HEAD_DIM, (h + 1) * HEAD_DIM)
    chunks_on_lanes = lambda t: jnp.concatenate([t[rows(ch), :] for ch in range(n_ch)], axis=1)

    lf = lf_ref[...]
    hi = lf.astype(BF16)
    rem = lf - hi.astype(F32)
    mid = rem.astype(BF16)
    lo = (rem - mid.astype(F32)).astype(BF16)
    split = jnp.concatenate([chunks_on_lanes(hi), chunks_on_lanes(mid), chunks_on_lanes(lo)], axis=0)
    sums = jnp.dot(pm_ref[...], split, preferred_element_type=F32)

    row = lax.broadcasted_iota(jnp.int32, (c, HEAD_DIM), 0)
    ops = []
    for u, (ch, h) in enumerate(units):
        col = slice(u * HEAD_DIM, (u + 1) * HEAD_DIM)
        g = sums[0:c, col]
        qf = q_ref[rows(ch), lanes(h)]
        kk = k_ref[rows(ch), lanes(h)]
        pairs = [(qf.astype(BF16), kk.astype(BF16))]
        for lvl in range(HGRN_LEVELS):
            upper = (row & (1 << lvl)) != 0
            e = g - sums[(lvl + 1) * c:(lvl + 2) * c, col]
            x = jnp.exp(jnp.where(upper, e, -e))
            pairs.append((jnp.where(upper, qf * x, 0.0).astype(BF16),
                          jnp.where(upper, 0.0, kk * x).astype(BF16)))
        g_last = g[c - 1:c, :]
        ops.append(dict(pairs=pairs,
                        q_in=(qf * jnp.exp(g)).astype(BF16),
                        k_out=(kk * jnp.exp(g_last - g)).astype(BF16),
                        decay=jnp.exp(g_last),
                        v=v_ref[rows(ch), lanes(h)].astype(BF16)))

    level_masks = [mask_ref[HGRN_LEVELS]] + [mask_ref[lvl] for lvl in range(HGRN_LEVELS)]
    for op in ops:
        a = None
        for (qt, kt), mask in zip(op["pairs"], level_masks):
            term = lax.dot_general(qt, kt, NT_DIMS, preferred_element_type=F32) * mask
            a = term if a is None else a + term
        op["a"] = a.astype(BF16)
    for op in ops:
        op["o"] = jnp.dot(op["a"], op["v"], preferred_element_type=F32)
        op["update"] = lax.dot_general(op["v"], op["k_out"], TN_DIMS, preferred_element_type=F32)

    for h in range(HGRN_GROUP):
        state_t = state_ref[h]
        for ch in range(n_ch):
            op = ops[ch * HGRN_GROUP + h]
            op["state"] = state_t.astype(BF16)
            state_t = state_t * op["decay"] + op["update"]
        state_ref[h] = state_t
    for (ch, h), op in zip(units, ops):
        o = op["o"] + lax.dot_general(op["q_in"], op["state"], NT_DIMS, preferred_element_type=F32)
        ms = jnp.mean(o * o, axis=-1, keepdims=True)
        y = o * lax.rsqrt(ms + RMS_EPS) * nw_ref[:, lanes(h)] * og_ref[rows(ch), lanes(h)]
        o_ref[rows(ch), lanes(h)] = y.astype(o_ref.dtype)


def _hgrn(qg, logf, key, val, norm_w):
    pm, masks = _hgrn_constants()
    pm = np.tile(pm, (1, 3))
    n_t = SEQ // HGRN_TILE
    width = HGRN_GROUP * HEAD_DIM
    groups = HEADS // HGRN_GROUP
    tile = lambda off: pl.BlockSpec((HGRN_TILE, width), lambda b, g, t: (b * n_t + t, g + off))
    return pl.pallas_call(
        _hgrn_kernel,
        grid=(BATCH, groups, n_t),
        in_specs=[tile(0), tile(0), tile(0), tile(0), tile(groups),
                  pl.BlockSpec((1, width), lambda b, g, t: (0, g)),
                  pl.BlockSpec(pm.shape, lambda b, g, t: (0, 0)),
                  pl.BlockSpec(masks.shape, lambda b, g, t: (0, 0, 0))],
        out_specs=tile(0),
        out_shape=jax.ShapeDtypeStruct((M_TOKENS, D_MODEL), BF16),
        scratch_shapes=[pltpu.VMEM((HGRN_GROUP, HEAD_DIM, HEAD_DIM), F32)],
        compiler_params=_params(("parallel", "parallel", "arbitrary")),
        name="hgrn",
    )(qg, logf, key, val, qg, norm_w, jnp.asarray(pm, BF16), jnp.asarray(masks, F32))


MIX_TM = 512


def _mix_kernel(ya_ref, yb_ref, gate_ref, x_ref, wa_ref, wb_ref, wo_ref, lnw_ref, lnb_ref,
                o_ref, ob_ref):
    za = jnp.dot(ya_ref[...], wa_ref[...], preferred_element_type=F32)
    zb = jnp.dot(yb_ref[...], wb_ref[...], preferred_element_type=F32)
    merged = gate_ref[:, :D_MODEL] * za + gate_ref[:, D_MODEL:] * zb
    mixed = jnp.dot(merged.astype(BF16), wo_ref[...], preferred_element_type=F32)
    y = _layer_norm(DN_ALPHA * x_ref[...] + mixed, lnw_ref[...], lnb_ref[...])
    o_ref[...] = y
    ob_ref[...] = y.astype(BF16)


def _mix(ya, yb, gates, x, wa, wb, wo, lnw, lnb):
    row = lambda n: pl.BlockSpec((MIX_TM, n), lambda i: (i, 0))
    full = lambda r, n: pl.BlockSpec((r, n), lambda i: (0, 0))
    return pl.pallas_call(
        _mix_kernel,
        grid=(M_TOKENS // MIX_TM,),
        in_specs=[row(D_MODEL), row(D_MODEL), row(2 * D_MODEL), row(D_MODEL),
                  full(D_MODEL, D_MODEL), full(D_MODEL, D_MODEL), full(D_MODEL, D_MODEL),
                  full(1, D_MODEL), full(1, D_MODEL)],
        out_specs=[row(D_MODEL), row(D_MODEL)],
        out_shape=[jax.ShapeDtypeStruct((M_TOKENS, D_MODEL), F32),
                   jax.ShapeDtypeStruct((M_TOKENS, D_MODEL), BF16)],
        compiler_params=_params(("parallel",)),
        name="mix",
    )(ya, yb, gates, x, wa, wb, wo, lnw, lnb)


FFN_TM = 512
FFN_CHUNK = 1408


def _ffn_kernel(xb_ref, x_ref, wg_ref, wu_ref, wd_ref, lnw_ref, lnb_ref, o_ref):
    xb = xb_ref[...]
    y = jnp.zeros((FFN_TM, D_MODEL), F32)
    for c in range(D_FF // FFN_CHUNK):
        cs = slice(c * FFN_CHUNK, (c + 1) * FFN_CHUNK)
        hg = jnp.dot(xb, wg_ref[:, cs], preferred_element_type=F32)
        hu = jnp.dot(xb, wu_ref[:, cs], preferred_element_type=F32)
        act = (hg * _sigmoid(hg) * hu).astype(BF16)
        y = y + jnp.dot(act, wd_ref[cs, :], preferred_element_type=F32)
    o_ref[...] = _layer_norm(DN_ALPHA * x_ref[...] + y, lnw_ref[...], lnb_ref[...])


def _ffn(xb, x, wg, wu, wd, lnw, lnb):
    row = lambda n: pl.BlockSpec((FFN_TM, n), lambda i: (i, 0))
    full = lambda r, n: pl.BlockSpec((r, n), lambda i: (0, 0), pipeline_mode=pl.Buffered(1))
    return pl.pallas_call(
        _ffn_kernel,
        grid=(M_TOKENS // FFN_TM,),
        in_specs=[row(D_MODEL), row(D_MODEL), full(D_MODEL, D_FF), full(D_MODEL, D_FF),
                  full(D_FF, D_MODEL), full(1, D_MODEL), full(1, D_MODEL)],
        out_specs=row(D_MODEL),
        out_shape=jax.ShapeDtypeStruct((M_TOKENS, D_MODEL), F32),
        compiler_params=_params(("parallel",)),
        name="ffn",
    )(xb, x, wg, wu, wd, lnw, lnb)


def _rope_tables():
    half = HEAD_DIM // 2
    inv_freq = ROPE_THETA ** (-jnp.arange(half, dtype=F32) / half)
    ang = jnp.arange(SEQ, dtype=F32)[:, None] * inv_freq[None, :]
    cos, sin = jnp.cos(ang), jnp.sin(ang)
    return jnp.concatenate([cos, cos], axis=1), jnp.concatenate([-sin, sin], axis=1)


def _layer(x2, w_in, lb, hgrn_norm_w, w_branch_a, w_branch_b, b_gate, w_out,
           ln1_w, ln1_b, w_ffn_in, w_ffn_down, ln2_w, ln2_b):
    d = D_MODEL
    xb = x2.astype(BF16)
    wseg = lambda a, b: w_in[:, a:b].astype(BF16)
    row2 = lambda v: v.reshape(1, -1).astype(F32)
    shape = lambda n, dt=F32: jax.ShapeDtypeStruct((M_TOKENS, n), dt)
    tiles_per_seq = SEQ // PROJ_TM
    rope_spec = pl.BlockSpec((PROJ_TM, HEAD_DIM), lambda i: (i % tiles_per_seq, 0))
    cos, sin = _rope_tables()

    w_qg = jnp.concatenate([w_in[:, 0:d], w_in[:, 3 * d:4 * d]], axis=1).astype(BF16)
    qg = _proj_call(_proj_silu_kernel, xb, w_qg, [], [], shape(2 * d), _row_spec(2 * d))
    logf, key = _proj_call(_proj_forget_kernel, xb, wseg(d, 2 * d), [row2(lb)], [_vec_spec(d)],
                           [shape(d), shape(d)], [_row_spec(d), _row_spec(d)])
    val = _proj_call(_proj_plain_kernel, xb, wseg(2 * d, 3 * d), [], [], shape(d), _row_spec(d))
    y_a = _hgrn(qg, logf, key, val, row2(hgrn_norm_w))

    q_rot = _proj_call(_proj_rope_kernel, xb, wseg(4 * d, 5 * d), [cos, sin], [rope_spec, rope_spec],
                       shape(d), _row_spec(d))
    blocks_per_tile = PROJ_TM // MOBA_BLOCK
    k_rot, k_mean = _proj_call(
        _proj_rope_kmean_kernel, xb, wseg(5 * d, 6 * d), [cos, sin], [rope_spec, rope_spec],
        [shape(d, BF16), jax.ShapeDtypeStruct((M_TOKENS // PROJ_TM, blocks_per_tile, d), F32)],
        [_row_spec(d), pl.BlockSpec((1, blocks_per_tile, d), lambda i: (i, 0, 0))])
    k_mean = k_mean.reshape(BATCH, MOBA_NBLK, d)
    v_t = _proj_call(
        _proj_vt_kernel, xb, wseg(6 * d, 7 * d), [], [],
        jax.ShapeDtypeStruct((BATCH, MOBA_NBLK, d, MOBA_BLOCK), BF16),
        pl.BlockSpec((1, blocks_per_tile, d, MOBA_BLOCK),
                     lambda i: (i // tiles_per_seq, i % tiles_per_seq, 0, 0)))
    bias = _moba_select(q_rot, k_mean)
    y_b = _moba_attention(q_rot, k_rot, v_t, bias)

    gates = _proj_call(_proj_gate_kernel, xb, wseg(7 * d, 9 * d), [row2(b_gate)], [_vec_spec(2 * d)],
                       shape(2 * d), _row_spec(2 * d))
    x1, x1b = _mix(y_a, y_b, gates, x2, w_branch_a.astype(BF16), w_branch_b.astype(BF16),
                   w_out.astype(BF16), row2(ln1_w), row2(ln1_b))
    return _ffn(x1b, x1, w_ffn_in[:, :D_FF].astype(BF16), w_ffn_in[:, D_FF:].astype(BF16),
                w_ffn_down.astype(BF16), row2(ln2_w), row2(ln2_b))


def kernel(x, w_in, lb_logits, hgrn_norm_w, w_branch_a, w_branch_b, b_gate, w_out, ln1_w, ln1_b,
           w_ffn_in, w_ffn_down, ln2_w, ln2_b):
    lower_bounds = jnp.cumsum(jax.nn.softmax(lb_logits.astype(F32), axis=0), axis=0)
    h = x.reshape(M_TOKENS, D_MODEL)
    for l in range(DEPTH):
        h = _layer(h, w_in[l], lower_bounds[l], hgrn_norm_w[l], w_branch_a[l], w_branch_b[l],
                   b_gate[l], w_out[l], ln1_w[l], ln1_b[l], w_ffn_in[l], w_ffn_down[l],
                   ln2_w[l], ln2_b[l])
    return h.reshape(BATCH, SEQ, D_MODEL)
```

```python
import numpy as np
import jax
import jax.numpy as jnp
from jax import lax
from jax.experimental import pallas as pl
from jax.experimental.pallas import tpu as pltpu

D_MODEL = 1024
BATCH = 4
SEQ = 4096
DEPTH = 1
HEADS = 8
HEAD_DIM = 128
HGRN_CHUNK = 64
HGRN_LEVELS = 6
MOBA_BLOCK = 256
MOBA_NBLK = SEQ // MOBA_BLOCK
MOBA_TOPK = 3
ROPE_THETA = 10000.0
D_FF = 2816
DN_ALPHA = (2.0 * DEPTH) ** 0.25
LN_EPS = 1e-5
RMS_EPS = 1e-6
M_TOKENS = BATCH * SEQ

MASK_VALUE = -1e30
LOG2_E = 1.4426950408889634
VMEM_LIMIT = 56 * 1024 * 1024

F32 = jnp.float32
BF16 = jnp.bfloat16
NT_DIMS = (((1,), (1,)), ((), ()))
TN_DIMS = (((0,), (0,)), ((), ()))


def _params(semantics):
    return pltpu.CompilerParams(dimension_semantics=semantics, vmem_limit_bytes=VMEM_LIMIT)


def _resident(shape):
    return pl.BlockSpec(shape, lambda *_: (0,) * len(shape), pipeline_mode=pl.Buffered(1))


def _sigmoid(z):
    return 1.0 / (1.0 + jnp.exp(-z))


def _layer_norm(r, w, b):
    mu = jnp.mean(r, axis=-1, keepdims=True)
    d = r - mu
    var = jnp.mean(d * d, axis=-1, keepdims=True)
    return d * lax.rsqrt(var + LN_EPS) * w + b


def _rope(t, cos, sin_signed):
    outs = []
    for h in range(HEADS):
        th = t[:, h * HEAD_DIM:(h + 1) * HEAD_DIM]
        outs.append(th * cos + pltpu.roll(th, HEAD_DIM // 2, 1) * sin_signed)
    return jnp.concatenate(outs, axis=1)


PROJ_TM = 512


def _proj_hgrn_kernel(x_ref, wqg_ref, wf_ref, wi_ref, lbl_ref, qg_ref, logf_ref, key_ref, val_ref):
    xb = x_ref[...].astype(BF16)
    acc = jnp.dot(xb, wqg_ref[...], preferred_element_type=F32)
    qg_ref[...] = (acc * _sigmoid(acc)).astype(qg_ref.dtype)
    l0, l1 = lbl_ref[0:1, :], lbl_ref[1:2, :]
    top = jnp.maximum(l0, l1)
    e0, e1 = jnp.exp(l0 - top), jnp.exp(l1 - top)
    lb = e0 / (e0 + e1)
    z = jnp.dot(xb, wf_ref[...], preferred_element_type=F32)
    logf_ref[...] = jnp.log(lb + (1.0 - lb) * _sigmoid(z))
    key_ref[...] = ((1.0 - lb) * _sigmoid(-z)).astype(key_ref.dtype)
    val_ref[...] = jnp.dot(xb, wi_ref[...], preferred_element_type=F32).astype(val_ref.dtype)


def _proj_hgrn(x, w_qg, w_f, w_i, lb_logits):
    d = D_MODEL
    row = lambda n: pl.BlockSpec((PROJ_TM, n), lambda i: (i, 0))
    shape = lambda n, dt: jax.ShapeDtypeStruct((M_TOKENS, n), dt)
    return pl.pallas_call(
        _proj_hgrn_kernel,
        grid=(M_TOKENS // PROJ_TM,),
        in_specs=[row(d), _resident((d, 2 * d)), _resident((d, d)), _resident((d, d)),
                  _resident((DEPTH + 1, d))],
        out_specs=[row(2 * d), row(d), row(d), row(d)],
        out_shape=[shape(2 * d, BF16), shape(d, F32), shape(d, BF16), shape(d, BF16)],
        compiler_params=_params(("parallel",)),
        name="proj_hgrn",
    )(x, w_qg, w_f, w_i, lb_logits)


def _proj_kv_kernel(x_ref, wk_ref, wv_ref, wg_ref, cos_ref, sin_ref, bg_ref,
                    k_ref, km_ref, vt_ref, gate_ref):
    xb = x_ref[...].astype(BF16)
    rot = _rope(jnp.dot(xb, wk_ref[...], preferred_element_type=F32), cos_ref[...], sin_ref[...])
    k_ref[...] = rot.astype(k_ref.dtype)
    v = jnp.dot(xb, wv_ref[...], preferred_element_type=F32)
    for r in range(PROJ_TM // MOBA_BLOCK):
        rows = slice(r * MOBA_BLOCK, (r + 1) * MOBA_BLOCK)
        km_ref[0, r:r + 1, :] = jnp.mean(rot[rows, :], axis=0, keepdims=True)
        vt_ref[0, r] = v[rows, :].T.astype(vt_ref.dtype)
    acc = jnp.dot(xb, wg_ref[...], preferred_element_type=F32)
    gate_ref[...] = _sigmoid(acc + bg_ref[...]).astype(gate_ref.dtype)


def _proj_kv(x, w_k, w_v, w_g, cos, sin, b_gate):
    d = D_MODEL
    tiles_per_seq = SEQ // PROJ_TM
    blocks_per_tile = PROJ_TM // MOBA_BLOCK
    row = lambda n: pl.BlockSpec((PROJ_TM, n), lambda i: (i, 0))
    rope = pl.BlockSpec((PROJ_TM, HEAD_DIM), lambda i: (i % tiles_per_seq, 0))
    k_rot, k_mean, v_t, gates = pl.pallas_call(
        _proj_kv_kernel,
        grid=(M_TOKENS // PROJ_TM,),
        in_specs=[row(d), _resident((d, d)), _resident((d, d)), _resident((d, 2 * d)), rope, rope,
                  _resident((1, 2 * d))],
        out_specs=[row(d),
                   pl.BlockSpec((1, blocks_per_tile, d), lambda i: (i, 0, 0)),
                   pl.BlockSpec((1, blocks_per_tile, d, MOBA_BLOCK),
                                lambda i: (i // tiles_per_seq, i % tiles_per_seq, 0, 0)),
                   row(2 * d)],
        out_shape=[jax.ShapeDtypeStruct((M_TOKENS, d), BF16),
                   jax.ShapeDtypeStruct((M_TOKENS // PROJ_TM, blocks_per_tile, d), F32),
                   jax.ShapeDtypeStruct((BATCH, MOBA_NBLK, d, MOBA_BLOCK), BF16),
                   jax.ShapeDtypeStruct((M_TOKENS, 2 * d), BF16)],
        compiler_params=_params(("parallel",)),
        name="proj_kv",
    )(x, w_k, w_v, w_g, cos, sin, b_gate)
    return k_rot, k_mean.reshape(BATCH, MOBA_NBLK, d), v_t, gates


def _proj_q_kernel(x_ref, wq_ref, cos_ref, sin_ref, km_ref, q_ref, bias_ref):
    own = pl.program_id(1)
    q = _rope(jnp.dot(x_ref[...].astype(BF16), wq_ref[...], preferred_element_type=F32),
              cos_ref[...], sin_ref[...])
    q_ref[...] = (q * (HEAD_DIM ** -0.5 * LOG2_E)).astype(q_ref.dtype)

    q_hi = q.astype(BF16)
    q_lo = (q - q_hi.astype(F32)).astype(BF16)
    gate = jnp.dot(jnp.concatenate([q_hi, q_lo, q_hi], axis=1), km_ref[0],
                   preferred_element_type=F32)
    lanes = HEADS * MOBA_NBLK
    lane = lax.broadcasted_iota(jnp.int32, (MOBA_BLOCK, lanes), 1)
    eligible = lane < own * HEADS
    gate = jnp.where(eligible, gate, -jnp.inf)
    rank = jnp.zeros((MOBA_BLOCK, lanes), F32)
    for r in range(1, MOBA_NBLK):
        partner = pltpu.roll(gate, lanes - r * HEADS, 1)
        wrapped = lane >= lanes - r * HEADS
        beats = (partner > gate) | (wrapped & (partner == gate))
        rank = rank + beats.astype(F32)
    selected = eligible & (rank < float(MOBA_TOPK))
    bias_ref[0] = jnp.where(selected, 0.0, MASK_VALUE).T


def _proj_q(x, w_q, cos, sin, gate_weights):
    d = D_MODEL
    lanes = HEADS * MOBA_NBLK
    return pl.pallas_call(
        _proj_q_kernel,
        grid=(BATCH, MOBA_NBLK),
        in_specs=[pl.BlockSpec((MOBA_BLOCK, d), lambda b, i: (b * MOBA_NBLK + i, 0)),
                  _resident((d, d)),
                  pl.BlockSpec((MOBA_BLOCK, HEAD_DIM), lambda b, i: (i, 0)),
                  pl.BlockSpec((MOBA_BLOCK, HEAD_DIM), lambda b, i: (i, 0)),
                  pl.BlockSpec((1, 3 * d, lanes), lambda b, i: (b, 0, 0))],
        out_specs=[pl.BlockSpec((MOBA_BLOCK, d), lambda b, i: (b * MOBA_NBLK + i, 0)),
                   pl.BlockSpec((1, lanes, MOBA_BLOCK), lambda b, i: (b, 0, i))],
        out_shape=[jax.ShapeDtypeStruct((M_TOKENS, d), BF16),
                   jax.ShapeDtypeStruct((BATCH, lanes, SEQ), F32)],
        compiler_params=_params(("parallel", "parallel")),
        name="proj_q",
    )(x, w_q, cos, sin, gate_weights)


def _moba_att_kernel(q_ref, k_ref, vt_ref, bias_ref, o_ref, acc_ref, s_ref):
    own = pl.program_id(1)

    def head(h):
        return slice(h * HEAD_DIM, (h + 1) * HEAD_DIM)

    def scores(h, j):
        kj = k_ref[pl.ds(pl.multiple_of(j * MOBA_BLOCK, MOBA_BLOCK), MOBA_BLOCK), head(h)]
        return lax.dot_general(kj, q_ref[:, head(h)], NT_DIMS, preferred_element_type=F32)

    def item(h, j, s, m, l, next_scores):
        m_new = jnp.maximum(m, jnp.max(s, axis=0, keepdims=True))
        a = jnp.exp2(m - m_new)
        p = jnp.exp2(s - m_new)
        if next_scores is not None:
            s_ref[h] = next_scores()
        pv = jnp.dot(vt_ref[0, j, head(h), :], p.astype(BF16), preferred_element_type=F32)
        acc_ref[h] = a * acc_ref[h] + pv
        return m_new, a * l + jnp.sum(p, axis=0, keepdims=True)

    acc_ref[...] = jnp.zeros_like(acc_ref)
    for h in range(HEADS):
        s_ref[h] = scores(h, 0)
    m0 = jnp.full((1, MOBA_BLOCK), MASK_VALUE, F32)
    l0 = jnp.zeros((1, MOBA_BLOCK), F32)

    def body(j, carry):
        ms, ls = list(carry[0]), list(carry[1])
        for h in range(HEADS):
            s = s_ref[h] + bias_ref[0, pl.ds(j * HEADS + h, 1), :]
            ms[h], ls[h] = item(h, j, s, ms[h], ls[h], lambda h=h: scores(h, j + 1))
        return tuple(ms), tuple(ls)

    ms, ls = lax.fori_loop(0, own, body, ((m0,) * HEADS, (l0,) * HEADS))

    key_pos = lax.broadcasted_iota(jnp.int32, (MOBA_BLOCK, MOBA_BLOCK), 0)
    qry_pos = lax.broadcasted_iota(jnp.int32, (MOBA_BLOCK, MOBA_BLOCK), 1)
    causal = key_pos <= qry_pos
    for h in range(HEADS):
        s = jnp.where(causal, s_ref[h], MASK_VALUE)
        _, l = item(h, own, s, ms[h], ls[h], None)
        o_ref[:, head(h)] = (acc_ref[h] * (1.0 / l)).T.astype(o_ref.dtype)


def _moba_attention(q_scaled, k_rot, v_t, bias):
    lanes = HEADS * MOBA_NBLK
    return pl.pallas_call(
        _moba_att_kernel,
        grid=(BATCH, MOBA_NBLK),
        in_specs=[pl.BlockSpec((MOBA_BLOCK, D_MODEL), lambda b, i: (b * MOBA_NBLK + i, 0)),
                  pl.BlockSpec((SEQ, D_MODEL), lambda b, i: (b, 0)),
                  pl.BlockSpec((1, MOBA_NBLK, D_MODEL, MOBA_BLOCK), lambda b, i: (b, 0, 0, 0)),
                  pl.BlockSpec((1, lanes, MOBA_BLOCK), lambda b, i: (b, 0, i))],
        out_specs=pl.BlockSpec((MOBA_BLOCK, D_MODEL), lambda b, i: (b * MOBA_NBLK + i, 0)),
        out_shape=jax.ShapeDtypeStruct((M_TOKENS, D_MODEL), BF16),
        scratch_shapes=[pltpu.VMEM((HEADS, HEAD_DIM, MOBA_BLOCK), F32),
                        pltpu.VMEM((HEADS, MOBA_BLOCK, MOBA_BLOCK), F32)],
        compiler_params=_params(("parallel", "arbitrary")),
        name="moba_att",
    )(q_scaled, k_rot, v_t, bias)


HGRN_TILE = 256
HGRN_GROUP = 2


def _hgrn_constants():
    c = HGRN_CHUNK
    idx = np.arange(c)
    tri = (idx[None, :] <= idx[:, None]).astype(np.float32)
    rows = [tri]
    masks = []
    for lvl in range(HGRN_LEVELS):
        half = 1 << lvl
        group = idx // (2 * half)
        upper = (idx & half) != 0
        rows.append(tri[group * (2 * half) + half - 1])
        masks.append(((group[:, None] == group[None, :]) & upper[:, None] & ~upper[None, :])
                     .astype(np.float32))
    masks.append(np.eye(c, dtype=np.float32))
    return np.concatenate(rows, axis=0), np.stack(masks)


def _hgrn_kernel(q_ref, lf_ref, k_ref, v_ref, og_ref, nw_ref, pm_ref, mask_ref, o_ref, state_ref):
    c = HGRN_CHUNK

    @pl.when(pl.program_id(2) == 0)
    def _():
        state_ref[...] = jnp.zeros_like(state_ref)

    n_ch = HGRN_TILE // c
    units = [(ch, h) for ch in range(n_ch) for h in range(HGRN_GROUP)]
    rows = lambda ch: slice(ch * c, (ch + 1) * c)
    lanes = lambda h: slice(h * HEAD_DIM, (h + 1) * HEAD_DIM)
    chunks_on_lanes = lambda t: jnp.concatenate([t[rows(ch), :] for ch in range(n_ch)], axis=1)

    lf = lf_ref[...]
    hi = lf.astype(BF16)
    rem = lf - hi.astype(F32)
    mid = rem.astype(BF16)
    lo = (rem - mid.astype(F32)).astype(BF16)
    split = jnp.concatenate([chunks_on_lanes(hi), chunks_on_lanes(mid), chunks_on_lanes(lo)], axis=0)
    sums = jnp.dot(pm_ref[...], split, preferred_element_type=F32)

    ops = []
    for u, (ch, h) in enumerate(units):
        col = slice(u * HEAD_DIM, (u + 1) * HEAD_DIM)
        g = sums[0:c, col]
        qf = q_ref[rows(ch), lanes(h)].astype(F32)
        kk = k_ref[rows(ch), lanes(h)].astype(F32)
        pairs = [(q_ref[rows(ch), lanes(h)], k_ref[rows(ch), lanes(h)])]
        for lvl in range(HGRN_LEVELS):
            x = jnp.exp(-jnp.abs(g - sums[(lvl + 1) * c:(lvl + 2) * c, col]))
            pairs.append(((qf * x).astype(BF16), (kk * x).astype(BF16)))
        g_last = g[c - 1:c, :]
        ops.append(dict(pairs=pairs,
                        q_in=(qf * jnp.exp(g)).astype(BF16),
                        k_out=(kk * jnp.exp(g_last - g)).astype(BF16),
                        decay=jnp.exp(g_last),
                        v=v_ref[rows(ch), lanes(h)]))

    level_masks = [mask_ref[HGRN_LEVELS]] + [mask_ref[lvl] for lvl in range(HGRN_LEVELS)]
    for op in ops:
        a = None
        for (qt, kt), mask in zip(op["pairs"], level_masks):
            term = lax.dot_general(qt, kt, NT_DIMS, preferred_element_type=F32) * mask
            a = term if a is None else a + term
        op["a"] = a.astype(BF16)
    for op in ops:
        op["o"] = jnp.dot(op["a"], op["v"], preferred_element_type=F32)
        op["update"] = lax.dot_general(op["v"], op["k_out"], TN_DIMS, preferred_element_type=F32)

    for h in range(HGRN_GROUP):
        state_t = state_ref[h]
        for ch in range(n_ch):
            op = ops[ch * HGRN_GROUP + h]
            op["state"] = state_t.astype(BF16)
            state_t = state_t * op["decay"] + op["update"]
        state_ref[h] = state_t
    for (ch, h), op in zip(units, ops):
        o = op["o"] + lax.dot_general(op["q_in"], op["state"], NT_DIMS, preferred_element_type=F32)
        ms = jnp.mean(o * o, axis=-1, keepdims=True)
        y = o * lax.rsqrt(ms + RMS_EPS) * nw_ref[:, lanes(h)] * og_ref[rows(ch), lanes(h)]
        o_ref[rows(ch), lanes(h)] = y.astype(o_ref.dtype)


def _hgrn(qg, logf, key, val, norm_w):
    pm, masks = _hgrn_constants()
    pm = np.tile(pm, (1, 3))
    n_t = SEQ // HGRN_TILE
    width = HGRN_GROUP * HEAD_DIM
    groups = HEADS // HGRN_GROUP
    tile = lambda off: pl.BlockSpec((HGRN_TILE, width), lambda b, g, t: (b * n_t + t, g + off))
    return pl.pallas_call(
        _hgrn_kernel,
        grid=(BATCH, groups, n_t),
        in_specs=[tile(0), tile(0), tile(0), tile(0), tile(groups),
                  pl.BlockSpec((1, width), lambda b, g, t: (0, g)),
                  _resident(pm.shape), _resident(masks.shape)],
        out_specs=tile(0),
        out_shape=jax.ShapeDtypeStruct((M_TOKENS, D_MODEL), BF16),
        scratch_shapes=[pltpu.VMEM((HGRN_GROUP, HEAD_DIM, HEAD_DIM), F32)],
        compiler_params=_params(("parallel", "parallel", "arbitrary")),
        name="hgrn",
    )(qg, logf, key, val, qg, norm_w, jnp.asarray(pm, BF16), jnp.asarray(masks, F32))


MIX_TM = 512


def _mix_kernel(ya_ref, yb_ref, gate_ref, x_ref, wa_ref, wb_ref, wo_ref, lnw_ref, lnb_ref,
                o_ref, ob_ref):
    za = jnp.dot(ya_ref[...], wa_ref[...], preferred_element_type=F32)
    zb = jnp.dot(yb_ref[...], wb_ref[...], preferred_element_type=F32)
    merged = gate_ref[:, :D_MODEL] * za + gate_ref[:, D_MODEL:] * zb
    mixed = jnp.dot(merged.astype(BF16), wo_ref[...], preferred_element_type=F32)
    y = _layer_norm(DN_ALPHA * x_ref[...] + mixed, lnw_ref[...], lnb_ref[...])
    o_ref[...] = y
    ob_ref[...] = y.astype(BF16)


def _mix(ya, yb, gates, x, wa, wb, wo, lnw, lnb):
    d = D_MODEL
    row = lambda n: pl.BlockSpec((MIX_TM, n), lambda i: (i, 0))
    return pl.pallas_call(
        _mix_kernel,
        grid=(M_TOKENS // MIX_TM,),
        in_specs=[row(d), row(d), row(2 * d), row(d),
                  _resident((d, d)), _resident((d, d)), _resident((d, d)),
                  _resident((1, d)), _resident((1, d))],
        out_specs=[row(d), row(d)],
        out_shape=[jax.ShapeDtypeStruct((M_TOKENS, d), F32),
                   jax.ShapeDtypeStruct((M_TOKENS, d), BF16)],
        compiler_params=_params(("parallel",)),
        name="mix",
    )(ya, yb, gates, x, wa, wb, wo, lnw, lnb)


FFN_TM = 512
FFN_CHUNK = 1408


def _ffn_kernel(xb_ref, x_ref, wg_ref, wu_ref, wd_ref, lnw_ref, lnb_ref, o_ref):
    xb = xb_ref[...]
    y = jnp.zeros((FFN_TM, D_MODEL), F32)
    for c in range(D_FF // FFN_CHUNK):
        cs = slice(c * FFN_CHUNK, (c + 1) * FFN_CHUNK)
        hg = jnp.dot(xb, wg_ref[:, cs], preferred_element_type=F32)
        hu = jnp.dot(xb, wu_ref[:, cs], preferred_element_type=F32)
        act = (hg * _sigmoid(hg) * hu).astype(BF16)
        y = y + jnp.dot(act, wd_ref[cs, :], preferred_element_type=F32)
    o_ref[...] = _layer_norm(DN_ALPHA * x_ref[...] + y, lnw_ref[...], lnb_ref[...])


def _ffn(xb, x, wg, wu, wd, lnw, lnb):
    d = D_MODEL
    row = lambda n: pl.BlockSpec((FFN_TM, n), lambda i: (i, 0))
    return pl.pallas_call(
        _ffn_kernel,
        grid=(M_TOKENS // FFN_TM,),
        in_specs=[row(d), row(d), _resident((d, D_FF)), _resident((d, D_FF)),
                  _resident((D_FF, d)), _resident((1, d)), _resident((1, d))],
        out_specs=row(d),
        out_shape=jax.ShapeDtypeStruct((M_TOKENS, d), F32),
        compiler_params=_params(("parallel",)),
        name="ffn",
    )(xb, x, wg, wu, wd, lnw, lnb)


def _rope_tables():
    half = HEAD_DIM // 2
    inv_freq = ROPE_THETA ** (-jnp.arange(half, dtype=F32) / half)
    ang = jnp.arange(SEQ, dtype=F32)[:, None] * inv_freq[None, :]
    cos, sin = jnp.cos(ang), jnp.sin(ang)
    return jnp.concatenate([cos, cos], axis=1), jnp.concatenate([-sin, sin], axis=1)


def _gate_weights(k_mean):
    km = k_mean.reshape(BATCH, MOBA_NBLK, HEADS, HEAD_DIM).transpose(0, 2, 3, 1)
    eye = jnp.eye(HEADS, dtype=F32)
    out = km[:, :, :, :, None] * eye[None, :, None, None, :]
    out = out.reshape(BATCH, D_MODEL, HEADS * MOBA_NBLK)
    hi = out.astype(BF16)
    lo = (out - hi.astype(F32)).astype(BF16)
    return jnp.concatenate([hi, hi, lo], axis=1)


def _layer(x2, w_in, lb_logits, hgrn_norm_w, w_branch_a, w_branch_b, b_gate, w_out,
           ln1_w, ln1_b, w_ffn_in, w_ffn_down, ln2_w, ln2_b):
    d = D_MODEL
    wseg = lambda a, b: w_in[:, a:b].astype(BF16)
    row2 = lambda v: v.reshape(1, -1).astype(F32)
    cos, sin = _rope_tables()

    w_qg = jnp.concatenate([w_in[:, 0:d], w_in[:, 3 * d:4 * d]], axis=1).astype(BF16)
    qg, logf, key, val = _proj_hgrn(x2, w_qg, wseg(d, 2 * d), wseg(2 * d, 3 * d),
                                    lb_logits.astype(F32))
    y_a = _hgrn(qg, logf, key, val, row2(hgrn_norm_w))

    k_rot, k_mean, v_t, gates = _proj_kv(x2, wseg(5 * d, 6 * d), wseg(6 * d, 7 * d),
                                         wseg(7 * d, 9 * d), cos, sin, row2(b_gate))
    q_scaled, bias = _proj_q(x2, wseg(4 * d, 5 * d), cos, sin, _gate_weights(k_mean))
    y_b = _moba_attention(q_scaled, k_rot, v_t, bias)

    x1, x1b = _mix(y_a, y_b, gates, x2, w_branch_a.astype(BF16), w_branch_b.astype(BF16),
                   w_out.astype(BF16), row2(ln1_w), row2(ln1_b))
    return _ffn(x1b, x1, w_ffn_in[:, :D_FF].astype(BF16), w_ffn_in[:, D_FF:].astype(BF16),
                w_ffn_down.astype(BF16), row2(ln2_w), row2(ln2_b))


def kernel(x, w_in, lb_logits, hgrn_norm_w, w_branch_a, w_branch_b, b_gate, w_out, ln1_w, ln1_b,
           w_ffn_in, w_ffn_down, ln2_w, ln2_b):
    assert DEPTH == 1
    h = x.reshape(M_TOKENS, D_MODEL)
    h = _layer(h, w_in[0], lb_logits, hgrn_norm_w[0], w_branch_a[0], w_branch_b[0],
               b_gate[0], w_out[0], ln1_w[0], ln1_b[0], w_ffn_in[0], w_ffn_down[0],
               ln2_w[0], ln2_b[0])
    return h.reshape(BATCH, SEQ, D_MODEL)
```

```python
import numpy as np
import jax
import jax.numpy as jnp
from jax import lax
from jax.experimental import pallas as pl
from jax.experimental.pallas import tpu as pltpu

D_MODEL = 1024
BATCH = 4
SEQ = 4096
DEPTH = 1
HEADS = 8
HEAD_DIM = 128
HGRN_CHUNK = 64
HGRN_LEVELS = 6
MOBA_BLOCK = 256
MOBA_NBLK = SEQ // MOBA_BLOCK
MOBA_TOPK = 3
ROPE_THETA = 10000.0
D_FF = 2816
DN_ALPHA = (2.0 * DEPTH) ** 0.25
LN_EPS = 1e-5
RMS_EPS = 1e-6
M_TOKENS = BATCH * SEQ

MASK_VALUE = -1e30
LOG2_E = 1.4426950408889634
VMEM_LIMIT = 56 * 1024 * 1024

F32 = jnp.float32
BF16 = jnp.bfloat16
NT_DIMS = (((1,), (1,)), ((), ()))
TN_DIMS = (((0,), (0,)), ((), ()))


def _params(semantics):
    return pltpu.CompilerParams(dimension_semantics=semantics, vmem_limit_bytes=VMEM_LIMIT)


def _resident(shape):
    return pl.BlockSpec(shape, lambda *_: (0,) * len(shape), pipeline_mode=pl.Buffered(1))


def _sigmoid(z):
    return 1.0 / (1.0 + jnp.exp(-z))


def _layer_norm(r, w, b):
    mu = jnp.mean(r, axis=-1, keepdims=True)
    d = r - mu
    var = jnp.mean(d * d, axis=-1, keepdims=True)
    return d * lax.rsqrt(var + LN_EPS) * w + b


def _rope(t, cos, sin_signed):
    outs = []
    for h in range(HEADS):
        th = t[:, h * HEAD_DIM:(h + 1) * HEAD_DIM]
        outs.append(th * cos + pltpu.roll(th, HEAD_DIM // 2, 1) * sin_signed)
    return jnp.concatenate(outs, axis=1)


PROJ_TM = 512


def _proj_hgrn_kernel(x_ref, wqg_ref, wf_ref, wi_ref, lbl_ref, qg_ref, logf_ref, key_ref, val_ref):
    xb = x_ref[...].astype(BF16)
    acc = jnp.dot(xb, wqg_ref[...], preferred_element_type=F32)
    qg_ref[...] = (acc * _sigmoid(acc)).astype(qg_ref.dtype)
    l0, l1 = lbl_ref[0:1, :], lbl_ref[1:2, :]
    top = jnp.maximum(l0, l1)
    e0, e1 = jnp.exp(l0 - top), jnp.exp(l1 - top)
    lb = e0 / (e0 + e1)
    z = jnp.dot(xb, wf_ref[...], preferred_element_type=F32)
    logf_ref[...] = jnp.log(lb + (1.0 - lb) * _sigmoid(z)) * LOG2_E
    key_ref[...] = ((1.0 - lb) * _sigmoid(-z)).astype(key_ref.dtype)
    val_ref[...] = jnp.dot(xb, wi_ref[...], preferred_element_type=F32).astype(val_ref.dtype)


def _proj_hgrn(x, w_qg, w_f, w_i, lb_logits):
    d = D_MODEL
    row = lambda n: pl.BlockSpec((PROJ_TM, n), lambda i: (i, 0))
    shape = lambda n, dt: jax.ShapeDtypeStruct((M_TOKENS, n), dt)
    return pl.pallas_call(
        _proj_hgrn_kernel,
        grid=(M_TOKENS // PROJ_TM,),
        in_specs=[row(d), _resident((d, 2 * d)), _resident((d, d)), _resident((d, d)),
                  _resident((DEPTH + 1, d))],
        out_specs=[row(2 * d), row(d), row(d), row(d)],
        out_shape=[shape(2 * d, BF16), shape(d, F32), shape(d, BF16), shape(d, BF16)],
        compiler_params=_params(("parallel",)),
        name="proj_hgrn",
    )(x, w_qg, w_f, w_i, lb_logits)


def _proj_kv_kernel(x_ref, wk_ref, wv_ref, wg_ref, cos_ref, sin_ref, bg_ref,
                    k_ref, km_ref, vt_ref, gate_ref):
    xb = x_ref[...].astype(BF16)
    rot = _rope(jnp.dot(xb, wk_ref[...], preferred_element_type=F32), cos_ref[...], sin_ref[...])
    k_ref[...] = rot.astype(k_ref.dtype)
    v = jnp.dot(xb, wv_ref[...], preferred_element_type=F32)
    for r in range(PROJ_TM // MOBA_BLOCK):
        rows = slice(r * MOBA_BLOCK, (r + 1) * MOBA_BLOCK)
        km_ref[0, r:r + 1, :] = jnp.mean(rot[rows, :], axis=0, keepdims=True)
        vt_ref[0, r] = v[rows, :].T.astype(vt_ref.dtype)
    acc = jnp.dot(xb, wg_ref[...], preferred_element_type=F32)
    gate_ref[...] = _sigmoid(acc + bg_ref[...]).astype(gate_ref.dtype)


def _proj_kv(x, w_k, w_v, w_g, cos, sin, b_gate):
    d = D_MODEL
    tiles_per_seq = SEQ // PROJ_TM
    blocks_per_tile = PROJ_TM // MOBA_BLOCK
    row = lambda n: pl.BlockSpec((PROJ_TM, n), lambda i: (i, 0))
    rope = pl.BlockSpec((PROJ_TM, HEAD_DIM), lambda i: (i % tiles_per_seq, 0))
    k_rot, k_mean, v_t, gates = pl.pallas_call(
        _proj_kv_kernel,
        grid=(M_TOKENS // PROJ_TM,),
        in_specs=[row(d), _resident((d, d)), _resident((d, d)), _resident((d, 2 * d)), rope, rope,
                  _resident((1, 2 * d))],
        out_specs=[row(d),
                   pl.BlockSpec((1, blocks_per_tile, d), lambda i: (i, 0, 0)),
                   pl.BlockSpec((1, blocks_per_tile, d, MOBA_BLOCK),
                                lambda i: (i // tiles_per_seq, i % tiles_per_seq, 0, 0)),
                   row(2 * d)],
        out_shape=[jax.ShapeDtypeStruct((M_TOKENS, d), BF16),
                   jax.ShapeDtypeStruct((M_TOKENS // PROJ_TM, blocks_per_tile, d), F32),
                   jax.ShapeDtypeStruct((BATCH, MOBA_NBLK, d, MOBA_BLOCK), BF16),
                   jax.ShapeDtypeStruct((M_TOKENS, 2 * d), BF16)],
        compiler_params=_params(("parallel",)),
        name="proj_kv",
    )(x, w_k, w_v, w_g, cos, sin, b_gate)
    return k_rot, k_mean.reshape(BATCH, MOBA_NBLK, d), v_t, gates


def _proj_q_kernel(x_ref, wq_ref, cos_ref, sin_ref, km_ref, q_ref, bias_ref):
    own = pl.program_id(1)
    q = _rope(jnp.dot(x_ref[...].astype(BF16), wq_ref[...], preferred_element_type=F32),
              cos_ref[...], sin_ref[...])
    q_ref[...] = (q * (HEAD_DIM ** -0.5 * LOG2_E)).astype(q_ref.dtype)

    q_hi = q.astype(BF16)
    q_lo = (q - q_hi.astype(F32)).astype(BF16)
    gate = jnp.dot(jnp.concatenate([q_hi, q_lo, q_hi], axis=1), km_ref[0],
                   preferred_element_type=F32)
    lanes = HEADS * MOBA_NBLK
    lane = lax.broadcasted_iota(jnp.int32, (MOBA_BLOCK, lanes), 1)
    eligible = lane < own * HEADS
    gate = jnp.where(eligible, gate, -jnp.inf)
    rank = jnp.zeros((MOBA_BLOCK, lanes), F32)
    for r in range(1, MOBA_NBLK):
        partner = pltpu.roll(gate, lanes - r * HEADS, 1)
        wrapped = lane >= lanes - r * HEADS
        beats = (partner > gate) | (wrapped & (partner == gate))
        rank = rank + beats.astype(F32)
    selected = eligible & (rank < float(MOBA_TOPK))
    bias_ref[0] = jnp.where(selected, 0.0, MASK_VALUE).T


def _proj_q(x, w_q, cos, sin, gate_weights):
    d = D_MODEL
    lanes = HEADS * MOBA_NBLK
    return pl.pallas_call(
        _proj_q_kernel,
        grid=(BATCH, MOBA_NBLK),
        in_specs=[pl.BlockSpec((MOBA_BLOCK, d), lambda b, i: (b * MOBA_NBLK + i, 0)),
                  _resident((d, d)),
                  pl.BlockSpec((MOBA_BLOCK, HEAD_DIM), lambda b, i: (i, 0)),
                  pl.BlockSpec((MOBA_BLOCK, HEAD_DIM), lambda b, i: (i, 0)),
                  pl.BlockSpec((1, 3 * d, lanes), lambda b, i: (b, 0, 0))],
        out_specs=[pl.BlockSpec((MOBA_BLOCK, d), lambda b, i: (b * MOBA_NBLK + i, 0)),
                   pl.BlockSpec((1, lanes, MOBA_BLOCK), lambda b, i: (b, 0, i))],
        out_shape=[jax.ShapeDtypeStruct((M_TOKENS, d), BF16),
                   jax.ShapeDtypeStruct((BATCH, lanes, SEQ), F32)],
        compiler_params=_params(("parallel", "parallel")),
        name="proj_q",
    )(x, w_q, cos, sin, gate_weights)


def _moba_att_kernel(q_ref, k_ref, vt_ref, bias_ref, o_ref, acc_ref, s_ref):
    own = pl.program_id(1)

    def head(h):
        return slice(h * HEAD_DIM, (h + 1) * HEAD_DIM)

    def scores(h, j):
        kj = k_ref[pl.ds(pl.multiple_of(j * MOBA_BLOCK, MOBA_BLOCK), MOBA_BLOCK), head(h)]
        return lax.dot_general(kj, q_ref[:, head(h)], NT_DIMS, preferred_element_type=F32)

    def item(h, j, s, m, l, next_scores):
        m_new = jnp.maximum(m, jnp.max(s, axis=0, keepdims=True))
        a = jnp.exp2(m - m_new)
        p = jnp.exp2(s - m_new)
        if next_scores is not None:
            s_ref[h] = next_scores()
        pv = jnp.dot(vt_ref[0, j, head(h), :], p.astype(BF16), preferred_element_type=F32)
        acc_ref[h] = a * acc_ref[h] + pv
        return m_new, a * l + jnp.sum(p, axis=0, keepdims=True)

    acc_ref[...] = jnp.zeros_like(acc_ref)
    for h in range(HEADS):
        s_ref[h] = scores(h, 0)
    m0 = jnp.full((1, MOBA_BLOCK), MASK_VALUE, F32)
    l0 = jnp.zeros((1, MOBA_BLOCK), F32)

    def body(j, carry):
        ms, ls = list(carry[0]), list(carry[1])
        for h in range(HEADS):
            s = s_ref[h] + bias_ref[0, pl.ds(j * HEADS + h, 1), :]
            ms[h], ls[h] = item(h, j, s, ms[h], ls[h], lambda h=h: scores(h, j + 1))
        return tuple(ms), tuple(ls)

    ms, ls = lax.fori_loop(0, own, body, ((m0,) * HEADS, (l0,) * HEADS))

    key_pos = lax.broadcasted_iota(jnp.int32, (MOBA_BLOCK, MOBA_BLOCK), 0)
    qry_pos = lax.broadcasted_iota(jnp.int32, (MOBA_BLOCK, MOBA_BLOCK), 1)
    causal = key_pos <= qry_pos
    for h in range(HEADS):
        s = jnp.where(causal, s_ref[h], MASK_VALUE)
        _, l = item(h, own, s, ms[h], ls[h], None)
        o_ref[:, head(h)] = (acc_ref[h] * (1.0 / l)).T.astype(o_ref.dtype)


def _moba_attention(q_scaled, k_rot, v_t, bias):
    lanes = HEADS * MOBA_NBLK
    return pl.pallas_call(
        _moba_att_kernel,
        grid=(BATCH, MOBA_NBLK),
        in_specs=[pl.BlockSpec((MOBA_BLOCK, D_MODEL), lambda b, i: (b * MOBA_NBLK + i, 0)),
                  pl.BlockSpec((SEQ, D_MODEL), lambda b, i: (b, 0)),
                  pl.BlockSpec((1, MOBA_NBLK, D_MODEL, MOBA_BLOCK), lambda b, i: (b, 0, 0, 0)),
                  pl.BlockSpec((1, lanes, MOBA_BLOCK), lambda b, i: (b, 0, i))],
        out_specs=pl.BlockSpec((MOBA_BLOCK, D_MODEL), lambda b, i: (b * MOBA_NBLK + i, 0)),
        out_shape=jax.ShapeDtypeStruct((M_TOKENS, D_MODEL), BF16),
        scratch_shapes=[pltpu.VMEM((HEADS, HEAD_DIM, MOBA_BLOCK), F32),
                        pltpu.VMEM((HEADS, MOBA_BLOCK, MOBA_BLOCK), F32)],
        compiler_params=_params(("parallel", "arbitrary")),
        name="moba_att",
    )(q_scaled, k_rot, v_t, bias)


HGRN_TILE = 256
HGRN_GROUP = 4
PM_LEVELS = (1, 2)


def _hgrn_constants():
    c = HGRN_CHUNK
    idx = np.arange(c)
    tri = (idx[None, :] <= idx[:, None]).astype(np.float32)
    rows = [tri]
    masks = []
    for lvl in range(HGRN_LEVELS):
        half = 1 << lvl
        group = idx // (2 * half)
        upper = (idx & half) != 0
        if lvl in PM_LEVELS:
            boundary = tri[group * (2 * half) + half - 1]
            rows.append(np.where(upper[:, None], tri - boundary, boundary - tri))
        masks.append(((group[:, None] == group[None, :]) & upper[:, None] & ~upper[None, :])
                     .astype(np.float32))
    masks.append(np.eye(c, dtype=np.float32))
    return np.concatenate(rows, axis=0), np.stack(masks)


def _hgrn_kernel(q_ref, lf_ref, k_ref, v_ref, og_ref, nw_ref, pm_ref, mask_ref, o_ref, state_ref):
    c = HGRN_CHUNK

    @pl.when(pl.program_id(2) == 0)
    def _():
        state_ref[...] = jnp.zeros_like(state_ref)

    n_ch = HGRN_TILE // c
    units = [(ch, h) for ch in range(n_ch) for h in range(HGRN_GROUP)]
    rows = lambda ch: slice(ch * c, (ch + 1) * c)
    lanes = lambda h: slice(h * HEAD_DIM, (h + 1) * HEAD_DIM)
    chunks_on_lanes = lambda t: jnp.concatenate([t[rows(ch), :] for ch in range(n_ch)], axis=1)

    lf = lf_ref[...]
    hi = lf.astype(BF16)
    rem = lf - hi.astype(F32)
    mid = rem.astype(BF16)
    lo = (rem - mid.astype(F32)).astype(BF16)
    split = jnp.concatenate([chunks_on_lanes(hi), chunks_on_lanes(mid), chunks_on_lanes(lo)], axis=0)
    sums = jnp.dot(pm_ref[...], split, preferred_element_type=F32)

    row = lax.broadcasted_iota(jnp.int32, (c, HEAD_DIM), 0)
    upper = [(row & (1 << lvl)) != 0 for lvl in range(HGRN_LEVELS)]

    def decayed_operands(u, ch, h):
        col = slice(u * HEAD_DIM, (u + 1) * HEAD_DIM)
        g = sums[0:c, col]
        q_bf, k_bf = q_ref[rows(ch), lanes(h)], k_ref[rows(ch), lanes(h)]
        qf, kk = q_bf.astype(F32), k_bf.astype(F32)
        zs = []
        for lvl in range(HGRN_LEVELS):
            half = 1 << lvl
            if lvl == 0:
                e = jnp.where(upper[0], lf_ref[rows(ch), lanes(h)], 0.0)
            elif lvl in PM_LEVELS:
                i = PM_LEVELS.index(lvl)
                e = sums[(i + 1) * c:(i + 2) * c, col]
            else:
                boundary = jnp.concatenate(
                    [jnp.broadcast_to(g[b + half - 1:b + half, :], (2 * half, HEAD_DIM))
                     for b in range(0, c, 2 * half)], axis=0)
                e = -jnp.abs(g - boundary)
            zs.append((jnp.where(upper[lvl], qf, kk) * jnp.exp2(e)).astype(BF16))
        g_last = g[c - 1:c, :]
        return dict(q=q_bf, k=k_bf, zs=zs,
                    q_in=(qf * jnp.exp2(g)).astype(BF16),
                    k_out=(kk * jnp.exp2(g_last - g)).astype(BF16),
                    decay=jnp.exp2(g_last),
                    v=v_ref[rows(ch), lanes(h)])

    def state_free_matmuls(op):
        score = lambda lhs, rhs: lax.dot_general(lhs, rhs, NT_DIMS,
                                                 preferred_element_type=F32).astype(BF16)
        a = score(op["q"], op["k"]) * mask_ref[HGRN_LEVELS]
        for lvl, z in enumerate(op["zs"]):
            a = a + score(z, z) * mask_ref[lvl]
        op["o"] = jnp.dot(a, op["v"], preferred_element_type=F32)
        op["update"] = lax.dot_general(op["v"], op["k_out"], TN_DIMS, preferred_element_type=F32)

    ops = {}
    for u, (ch, h) in enumerate(units):
        ops[ch, h] = decayed_operands(u, ch, h)
        if u >= 1:
            state_free_matmuls(ops[units[u - 1]])
    state_free_matmuls(ops[units[-1]])

    for h in range(HGRN_GROUP):
        state_t = state_ref[h]
        for ch in range(n_ch):
            op = ops[ch, h]
            op["state"] = state_t.astype(BF16)
            state_t = state_t * op["decay"] + op["update"]
        state_ref[h] = state_t
    for ch, h in units:
        op = ops[ch, h]
        o = op["o"] + lax.dot_general(op["q_in"], op["state"], NT_DIMS, preferred_element_type=F32)
        ms = jnp.mean(o * o, axis=-1, keepdims=True)
        y = o * lax.rsqrt(ms + RMS_EPS) * nw_ref[:, lanes(h)] * og_ref[rows(ch), lanes(h)]
        o_ref[rows(ch), lanes(h)] = y.astype(o_ref.dtype)


def _hgrn(qg, logf, key, val, norm_w):
    pm, masks = _hgrn_constants()
    pm = np.tile(pm, (1, 3))
    n_t = SEQ // HGRN_TILE
    width = HGRN_GROUP * HEAD_DIM
    groups = HEADS // HGRN_GROUP
    tile = lambda off: pl.BlockSpec((HGRN_TILE, width), lambda b, g, t: (b * n_t + t, g + off))
    return pl.pallas_call(
        _hgrn_kernel,
        grid=(BATCH, groups, n_t),
        in_specs=[tile(0), tile(0), tile(0), tile(0), tile(groups),
                  pl.BlockSpec((1, width), lambda b, g, t: (0, g)),
                  _resident(pm.shape), _resident(masks.shape)],
        out_specs=tile(0),
        out_shape=jax.ShapeDtypeStruct((M_TOKENS, D_MODEL), BF16),
        scratch_shapes=[pltpu.VMEM((HGRN_GROUP, HEAD_DIM, HEAD_DIM), F32)],
        compiler_params=_params(("parallel", "parallel", "arbitrary")),
        name="hgrn",
    )(qg, logf, key, val, qg, norm_w, jnp.asarray(pm, BF16), jnp.asarray(masks, BF16))


MIX_TM = 512


def _mix_kernel(ya_ref, yb_ref, gate_ref, x_ref, wa_ref, wb_ref, wo_ref, lnw_ref, lnb_ref,
                o_ref, ob_ref):
    za = jnp.dot(ya_ref[...], wa_ref[...], preferred_element_type=F32)
    zb = jnp.dot(yb_ref[...], wb_ref[...], preferred_element_type=F32)
    merged = gate_ref[:, :D_MODEL] * za + gate_ref[:, D_MODEL:] * zb
    mixed = jnp.dot(merged.astype(BF16), wo_ref[...], preferred_element_type=F32)
    y = _layer_norm(DN_ALPHA * x_ref[...] + mixed, lnw_ref[...], lnb_ref[...])
    o_ref[...] = y
    ob_ref[...] = y.astype(BF16)


def _mix(ya, yb, gates, x, wa, wb, wo, lnw, lnb):
    d = D_MODEL
    row = lambda n: pl.BlockSpec((MIX_TM, n), lambda i: (i, 0))
    return pl.pallas_call(
        _mix_kernel,
        grid=(M_TOKENS // MIX_TM,),
        in_specs=[row(d), row(d), row(2 * d), row(d),
                  _resident((d, d)), _resident((d, d)), _resident((d, d)),
                  _resident((1, d)), _resident((1, d))],
        out_specs=[row(d), row(d)],
        out_shape=[jax.ShapeDtypeStruct((M_TOKENS, d), F32),
                   jax.ShapeDtypeStruct((M_TOKENS, d), BF16)],
        compiler_params=_params(("parallel",)),
        name="mix",
    )(ya, yb, gates, x, wa, wb, wo, lnw, lnb)


FFN_TM = 512
FFN_CHUNK = 1408


def _ffn_kernel(xb_ref, x_ref, wg_ref, wu_ref, wd_ref, lnw_ref, lnb_ref, o_ref):
    xb = xb_ref[...]
    y = jnp.zeros((FFN_TM, D_MODEL), F32)
    for c in range(D_FF // FFN_CHUNK):
        cs = slice(c * FFN_CHUNK, (c + 1) * FFN_CHUNK)
        hg = jnp.dot(xb, wg_ref[:, cs], preferred_element_type=F32)
        hu = jnp.dot(xb, wu_ref[:, cs], preferred_element_type=F32)
        act = (hg * _sigmoid(hg) * hu).astype(BF16)
        y = y + jnp.dot(act, wd_ref[cs, :], preferred_element_type=F32)
    o_ref[...] = _layer_norm(DN_ALPHA * x_ref[...] + y, lnw_ref[...], lnb_ref[...])


def _ffn(xb, x, wg, wu, wd, lnw, lnb):
    d = D_MODEL
    row = lambda n: pl.BlockSpec((FFN_TM, n), lambda i: (i, 0))
    return pl.pallas_call(
        _ffn_kernel,
        grid=(M_TOKENS // FFN_TM,),
        in_specs=[row(d), row(d), _resident((d, D_FF)), _resident((d, D_FF)),
                  _resident((D_FF, d)), _resident((1, d)), _resident((1, d))],
        out_specs=row(d),
        out_shape=jax.ShapeDtypeStruct((M_TOKENS, d), F32),
        compiler_params=_params(("parallel",)),
        name="ffn",
    )(xb, x, wg, wu, wd, lnw, lnb)


def _rope_tables():
    half = HEAD_DIM // 2
    inv_freq = ROPE_THETA ** (-jnp.arange(half, dtype=F32) / half)
    ang = jnp.arange(SEQ, dtype=F32)[:, None] * inv_freq[None, :]
    cos, sin = jnp.cos(ang), jnp.sin(ang)
    return jnp.concatenate([cos, cos], axis=1), jnp.concatenate([-sin, sin], axis=1)


def _gate_weights(k_mean):
    km = k_mean.reshape(BATCH, MOBA_NBLK, HEADS, HEAD_DIM).transpose(0, 2, 3, 1)
    eye = jnp.eye(HEADS, dtype=F32)
    out = km[:, :, :, :, None] * eye[None, :, None, None, :]
    out = out.reshape(BATCH, D_MODEL, HEADS * MOBA_NBLK)
    hi = out.astype(BF16)
    lo = (out - hi.astype(F32)).astype(BF16)
    return jnp.concatenate([hi, hi, lo], axis=1)


def _layer(x2, w_in, lb_logits, hgrn_norm_w, w_branch_a, w_branch_b, b_gate, w_out,
           ln1_w, ln1_b, w_ffn_in, w_ffn_down, ln2_w, ln2_b):
    d = D_MODEL
    wseg = lambda a, b: w_in[:, a:b].astype(BF16)
    row2 = lambda v: v.reshape(1, -1).astype(F32)
    cos, sin = _rope_tables()

    w_qg = jnp.concatenate([w_in[:, 0:d], w_in[:, 3 * d:4 * d]], axis=1).astype(BF16)
    qg, logf, key, val = _proj_hgrn(x2, w_qg, wseg(d, 2 * d), wseg(2 * d, 3 * d),
                                    lb_logits.astype(F32))
    y_a = _hgrn(qg, logf, key, val, row2(hgrn_norm_w))

    k_rot, k_mean, v_t, gates = _proj_kv(x2, wseg(5 * d, 6 * d), wseg(6 * d, 7 * d),
                                         wseg(7 * d, 9 * d), cos, sin, row2(b_gate))
    q_scaled, bias = _proj_q(x2, wseg(4 * d, 5 * d), cos, sin, _gate_weights(k_mean))
    y_b = _moba_attention(q_scaled, k_rot, v_t, bias)

    x1, x1b = _mix(y_a, y_b, gates, x2, w_branch_a.astype(BF16), w_branch_b.astype(BF16),
                   w_out.astype(BF16), row2(ln1_w), row2(ln1_b))
    return _ffn(x1b, x1, w_ffn_in[:, :D_FF].astype(BF16), w_ffn_in[:, D_FF:].astype(BF16),
                w_ffn_down.astype(BF16), row2(ln2_w), row2(ln2_b))


def kernel(x, w_in, lb_logits, hgrn_norm_w, w_branch_a, w_branch_b, b_gate, w_out, ln1_w, ln1_b,
           w_ffn_in, w_ffn_down, ln2_w, ln2_b):
    assert DEPTH == 1
    h = x.reshape(M_TOKENS, D_MODEL)
    h = _layer(h, w_in[0], lb_logits, hgrn_norm_w[0], w_branch_a[0], w_branch_b[0],
               b_gate[0], w_out[0], ln1_w[0], ln1_b[0], w_ffn_in[0], w_ffn_down[0],
               ln2_w[0], ln2_b[0])
    return h.reshape(BATCH, SEQ, D_MODEL)
```

```python
import numpy as np
import jax
import jax.numpy as jnp
from jax import lax
from jax.experimental import pallas as pl
from jax.experimental.pallas import tpu as pltpu

D_MODEL = 1024
BATCH = 4
SEQ = 4096
DEPTH = 1
HEADS = 8
HEAD_DIM = 128
HGRN_CHUNK = 64
HGRN_LEVELS = 6
MOBA_BLOCK = 256
MOBA_NBLK = SEQ // MOBA_BLOCK
MOBA_TOPK = 3
ROPE_THETA = 10000.0
D_FF = 2816
DN_ALPHA = (2.0 * DEPTH) ** 0.25
LN_EPS = 1e-5
RMS_EPS = 1e-6
M_TOKENS = BATCH * SEQ

MASK_VALUE = -1e30
LOG2_E = 1.4426950408889634
VMEM_LIMIT = 56 * 1024 * 1024

F32 = jnp.float32
BF16 = jnp.bfloat16
NT_DIMS = (((1,), (1,)), ((), ()))
TN_DIMS = (((0,), (0,)), ((), ()))


def _params(semantics):
    return pltpu.CompilerParams(dimension_semantics=semantics, vmem_limit_bytes=VMEM_LIMIT)


def _resident(shape):
    return pl.BlockSpec(shape, lambda *_: (0,) * len(shape), pipeline_mode=pl.Buffered(1))


def _sigmoid(z):
    return 1.0 / (1.0 + jnp.exp(-z))


def _layer_norm(r, w, b):
    mu = jnp.mean(r, axis=-1, keepdims=True)
    d = r - mu
    var = jnp.mean(d * d, axis=-1, keepdims=True)
    return d * lax.rsqrt(var + LN_EPS) * w + b


def _rope(t, cos, sin_signed):
    outs = []
    for h in range(HEADS):
        th = t[:, h * HEAD_DIM:(h + 1) * HEAD_DIM]
        outs.append(th * cos + pltpu.roll(th, HEAD_DIM // 2, 1) * sin_signed)
    return jnp.concatenate(outs, axis=1)


def _proj_kernel(x_ref, wq_ref, wk_ref, wv_ref, wg_ref, wqg_ref, wf_ref, wi_ref, cos_ref, sin_ref,
                 bg_ref, lbl_ref,
                 q_ref, bias_ref, k_ref, vt_ref, gate_ref, qg_ref, logf_ref, key_ref, val_ref,
                 km_ref):
    own = pl.program_id(1)
    lanes = HEADS * MOBA_NBLK

    @pl.when(own == 0)
    def _():
        km_ref[...] = jnp.zeros_like(km_ref)

    xb = x_ref[...].astype(BF16)
    q = _rope(jnp.dot(xb, wq_ref[...], preferred_element_type=F32), cos_ref[...], sin_ref[...])
    q_ref[...] = (q * (HEAD_DIM ** -0.5 * LOG2_E)).astype(q_ref.dtype)

    split = lambda t: (t.astype(BF16), (t - t.astype(BF16).astype(F32)).astype(BF16))
    q_hi, q_lo = split(q)
    km_hi, km_lo = split(km_ref[...])
    gate = lax.dot_general(jnp.concatenate([q_hi, q_lo, q_hi], axis=1),
                           jnp.concatenate([km_hi, km_hi, km_lo], axis=1), NT_DIMS,
                           preferred_element_type=F32)
    lane = lax.broadcasted_iota(jnp.int32, (MOBA_BLOCK, lanes), 1)
    eligible = lane < own * HEADS
    gate = jnp.where(eligible, gate, -jnp.inf)
    rank = jnp.zeros((MOBA_BLOCK, lanes), F32)
    for r in range(1, MOBA_NBLK):
        partner = pltpu.roll(gate, lanes - r * HEADS, 1)
        wrapped = lane >= lanes - r * HEADS
        beats = (partner > gate) | (wrapped & (partner == gate))
        rank = rank + beats.astype(F32)
    selected = eligible & (rank < float(MOBA_TOPK))
    bias_ref[0] = jnp.where(selected, 0.0, MASK_VALUE).T

    rot = _rope(jnp.dot(xb, wk_ref[...], preferred_element_type=F32), cos_ref[...], sin_ref[...])
    k_ref[...] = rot.astype(k_ref.dtype)
    mean = jnp.mean(rot, axis=0, keepdims=True)
    col_head = lax.broadcasted_iota(jnp.int32, (HEADS, D_MODEL), 1) // HEAD_DIM
    row_head = lax.broadcasted_iota(jnp.int32, (HEADS, D_MODEL), 0)
    km_ref[pl.ds(pl.multiple_of(own * HEADS, HEADS), HEADS), :] = jnp.where(
        col_head == row_head, jnp.broadcast_to(mean, (HEADS, D_MODEL)), 0.0)
    v = jnp.dot(xb, wv_ref[...], preferred_element_type=F32)
    vt_ref[0, 0] = v.T.astype(vt_ref.dtype)
    acc = jnp.dot(xb, wg_ref[...], preferred_element_type=F32)
    gate_ref[...] = _sigmoid(acc + bg_ref[...]).astype(gate_ref.dtype)

    acc = jnp.dot(xb, wqg_ref[...], preferred_element_type=F32)
    qg_ref[...] = (acc * _sigmoid(acc)).astype(qg_ref.dtype)
    l0, l1 = lbl_ref[0:1, :], lbl_ref[1:2, :]
    top = jnp.maximum(l0, l1)
    e0, e1 = jnp.exp(l0 - top), jnp.exp(l1 - top)
    lb = e0 / (e0 + e1)
    z = jnp.dot(xb, wf_ref[...], preferred_element_type=F32)
    logf_ref[...] = jnp.log(lb + (1.0 - lb) * _sigmoid(z)) * LOG2_E
    key_ref[...] = ((1.0 - lb) * _sigmoid(-z)).astype(key_ref.dtype)
    val_ref[...] = jnp.dot(xb, wi_ref[...], preferred_element_type=F32).astype(val_ref.dtype)


def _proj(x, w_q, w_k, w_v, w_g, w_qg, w_f, w_i, cos, sin, b_gate, lb_logits):
    d = D_MODEL
    lanes = HEADS * MOBA_NBLK
    row = lambda n: pl.BlockSpec((MOBA_BLOCK, n), lambda b, i: (b * MOBA_NBLK + i, 0))
    rope = pl.BlockSpec((MOBA_BLOCK, HEAD_DIM), lambda b, i: (i, 0))
    shape = lambda n, dt: jax.ShapeDtypeStruct((M_TOKENS, n), dt)
    return pl.pallas_call(
        _proj_kernel,
        grid=(BATCH, MOBA_NBLK),
        in_specs=[row(d), _resident((d, d)), _resident((d, d)), _resident((d, d)),
                  _resident((d, 2 * d)), _resident((d, 2 * d)), _resident((d, d)), _resident((d, d)),
                  rope, rope, _resident((1, 2 * d)), _resident((DEPTH + 1, d))],
        out_specs=[row(d),
                   pl.BlockSpec((1, lanes, MOBA_BLOCK), lambda b, i: (b, 0, i)),
                   row(d),
                   pl.BlockSpec((1, 1, d, MOBA_BLOCK), lambda b, i: (b, i, 0, 0)),
                   row(2 * d), row(2 * d), row(d), row(d), row(d)],
        out_shape=[shape(d, BF16),
                   jax.ShapeDtypeStruct((BATCH, lanes, SEQ), F32),
                   shape(d, BF16),
                   jax.ShapeDtypeStruct((BATCH, MOBA_NBLK, d, MOBA_BLOCK), BF16),
                   shape(2 * d, BF16), shape(2 * d, BF16), shape(d, F32), shape(d, BF16),
                   shape(d, BF16)],
        scratch_shapes=[pltpu.VMEM((lanes, d), F32)],
        compiler_params=_params(("arbitrary", "arbitrary")),
        name="proj",
    )(x, w_q, w_k, w_v, w_g, w_qg, w_f, w_i, cos, sin, b_gate, lb_logits)


def _moba_att_kernel(q_ref, k_ref, vt_ref, bias_ref, o_ref, acc_ref, s_ref):
    own = pl.program_id(1)

    def head(h):
        return slice(h * HEAD_DIM, (h + 1) * HEAD_DIM)

    def scores(h, j):
        kj = k_ref[pl.ds(pl.multiple_of(j * MOBA_BLOCK, MOBA_BLOCK), MOBA_BLOCK), head(h)]
        return lax.dot_general(kj, q_ref[:, head(h)], NT_DIMS, preferred_element_type=F32)

    def item(h, j, s, m, l, next_scores):
        m_new = jnp.maximum(m, jnp.max(s, axis=0, keepdims=True))
        a = jnp.exp2(m - m_new)
        p = jnp.exp2(s - m_new)
        if next_scores is not None:
            s_ref[h] = next_scores()
        pv = jnp.dot(vt_ref[0, j, head(h), :], p.astype(BF16), preferred_element_type=F32)
        acc_ref[h] = a * acc_ref[h] + pv
        return m_new, a * l + jnp.sum(p, axis=0, keepdims=True)

    acc_ref[...] = jnp.zeros_like(acc_ref)
    for h in range(HEADS):
        s_ref[h] = scores(h, 0)
    m0 = jnp.full((1, MOBA_BLOCK), MASK_VALUE, F32)
    l0 = jnp.zeros((1, MOBA_BLOCK), F32)

    def body(j, carry):
        ms, ls = list(carry[0]), list(carry[1])
        for h in range(HEADS):
            s = s_ref[h] + bias_ref[0, pl.ds(j * HEADS + h, 1), :]
            ms[h], ls[h] = item(h, j, s, ms[h], ls[h], lambda h=h: scores(h, j + 1))
        return tuple(ms), tuple(ls)

    ms, ls = lax.fori_loop(0, own, body, ((m0,) * HEADS, (l0,) * HEADS))

    key_pos = lax.broadcasted_iota(jnp.int32, (MOBA_BLOCK, MOBA_BLOCK), 0)
    qry_pos = lax.broadcasted_iota(jnp.int32, (MOBA_BLOCK, MOBA_BLOCK), 1)
    causal = key_pos <= qry_pos
    for h in range(HEADS):
        s = jnp.where(causal, s_ref[h], MASK_VALUE)
        _, l = item(h, own, s, ms[h], ls[h], None)
        o_ref[:, head(h)] = (acc_ref[h] * (1.0 / l)).T.astype(o_ref.dtype)


def _moba_attention(q_scaled, k_rot, v_t, bias):
    lanes = HEADS * MOBA_NBLK
    return pl.pallas_call(
        _moba_att_kernel,
        grid=(BATCH, MOBA_NBLK),
        in_specs=[pl.BlockSpec((MOBA_BLOCK, D_MODEL), lambda b, i: (b * MOBA_NBLK + i, 0)),
                  pl.BlockSpec((SEQ, D_MODEL), lambda b, i: (b, 0)),
                  pl.BlockSpec((1, MOBA_NBLK, D_MODEL, MOBA_BLOCK), lambda b, i: (b, 0, 0, 0)),
                  pl.BlockSpec((1, lanes, MOBA_BLOCK), lambda b, i: (b, 0, i))],
        out_specs=pl.BlockSpec((MOBA_BLOCK, D_MODEL), lambda b, i: (b * MOBA_NBLK + i, 0)),
        out_shape=jax.ShapeDtypeStruct((M_TOKENS, D_MODEL), BF16),
        scratch_shapes=[pltpu.VMEM((HEADS, HEAD_DIM, MOBA_BLOCK), F32),
                        pltpu.VMEM((HEADS, MOBA_BLOCK, MOBA_BLOCK), F32)],
        compiler_params=_params(("parallel", "arbitrary")),
        name="moba_att",
    )(q_scaled, k_rot, v_t, bias)


HGRN_TILE = 256
HGRN_GROUP = 4
PM_LEVELS = (1, 2)


def _hgrn_constants():
    c = HGRN_CHUNK
    idx = np.arange(c)
    tri = (idx[None, :] <= idx[:, None]).astype(np.float32)
    rows = [tri]
    masks = []
    for lvl in range(HGRN_LEVELS):
        half = 1 << lvl
        group = idx // (2 * half)
        upper = (idx & half) != 0
        if lvl in PM_LEVELS:
            boundary = tri[group * (2 * half) + half - 1]
            rows.append(np.where(upper[:, None], tri - boundary, boundary - tri))
        masks.append(((group[:, None] == group[None, :]) & upper[:, None] & ~upper[None, :])
                     .astype(np.float32))
    masks.append(np.eye(c, dtype=np.float32))
    return np.concatenate(rows, axis=0), np.stack(masks)


def _hgrn_kernel(q_ref, lf_ref, k_ref, v_ref, og_ref, nw_ref, pm_ref, mask_ref, o_ref, state_ref):
    c = HGRN_CHUNK

    @pl.when(pl.program_id(2) == 0)
    def _():
        state_ref[...] = jnp.zeros_like(state_ref)

    n_ch = HGRN_TILE // c
    units = [(ch, h) for ch in range(n_ch) for h in range(HGRN_GROUP)]
    rows = lambda ch: slice(ch * c, (ch + 1) * c)
    lanes = lambda h: slice(h * HEAD_DIM, (h + 1) * HEAD_DIM)
    chunks_on_lanes = lambda t: jnp.concatenate([t[rows(ch), :] for ch in range(n_ch)], axis=1)

    lf = lf_ref[...]
    hi = lf.astype(BF16)
    rem = lf - hi.astype(F32)
    mid = rem.astype(BF16)
    lo = (rem - mid.astype(F32)).astype(BF16)
    split = jnp.concatenate([chunks_on_lanes(hi), chunks_on_lanes(mid), chunks_on_lanes(lo)], axis=0)
    sums = jnp.dot(pm_ref[...], split, preferred_element_type=F32)

    row = lax.broadcasted_iota(jnp.int32, (c, HEAD_DIM), 0)
    upper = [(row & (1 << lvl)) != 0 for lvl in range(HGRN_LEVELS)]

    def decayed_operands(u, ch, h):
        col = slice(u * HEAD_DIM, (u + 1) * HEAD_DIM)
        g = sums[0:c, col]
        q_bf, k_bf = q_ref[rows(ch), lanes(h)], k_ref[rows(ch), lanes(h)]
        qf, kk = q_bf.astype(F32), k_bf.astype(F32)
        zs = []
        for lvl in range(HGRN_LEVELS):
            half = 1 << lvl
            if lvl == 0:
                e = jnp.where(upper[0], lf_ref[rows(ch), lanes(h)], 0.0)
            elif lvl in PM_LEVELS:
                i = PM_LEVELS.index(lvl)
                e = sums[(i + 1) * c:(i + 2) * c, col]
            else:
                boundary = jnp.concatenate(
                    [jnp.broadcast_to(g[b + half - 1:b + half, :], (2 * half, HEAD_DIM))
                     for b in range(0, c, 2 * half)], axis=0)
                e = -jnp.abs(g - boundary)
            zs.append((jnp.where(upper[lvl], qf, kk) * jnp.exp2(e)).astype(BF16))
        g_last = g[c - 1:c, :]
        return dict(q=q_bf, k=k_bf, zs=zs,
                    q_in=(qf * jnp.exp2(g)).astype(BF16),
                    k_out=(kk * jnp.exp2(g_last - g)).astype(BF16),
                    decay=jnp.exp2(g_last),
                    v=v_ref[rows(ch), lanes(h)])

    def state_free_matmuls(op):
        score = lambda lhs, rhs: lax.dot_general(lhs, rhs, NT_DIMS,
                                                 preferred_element_type=F32).astype(BF16)
        a = score(op["q"], op["k"]) * mask_ref[HGRN_LEVELS]
        for lvl, z in enumerate(op["zs"]):
            a = a + score(z, z) * mask_ref[lvl]
        op["o"] = jnp.dot(a, op["v"], preferred_element_type=F32)
        op["update"] = lax.dot_general(op["v"], op["k_out"], TN_DIMS, preferred_element_type=F32)

    ops = {}
    for u, (ch, h) in enumerate(units):
        ops[ch, h] = decayed_operands(u, ch, h)
        if u >= 1:
            state_free_matmuls(ops[units[u - 1]])
    state_free_matmuls(ops[units[-1]])

    for h in range(HGRN_GROUP):
        state_t = state_ref[h]
        for ch in range(n_ch):
            op = ops[ch, h]
            op["state"] = state_t.astype(BF16)
            state_t = state_t * op["decay"] + op["update"]
        state_ref[h] = state_t
    for ch, h in units:
        op = ops[ch, h]
        o = op["o"] + lax.dot_general(op["q_in"], op["state"], NT_DIMS, preferred_element_type=F32)
        ms = jnp.mean(o * o, axis=-1, keepdims=True)
        y = o * lax.rsqrt(ms + RMS_EPS) * nw_ref[:, lanes(h)] * og_ref[rows(ch), lanes(h)]
        o_ref[rows(ch), lanes(h)] = y.astype(o_ref.dtype)


def _hgrn(qg, logf, key, val, norm_w):
    pm, masks = _hgrn_constants()
    pm = np.tile(pm, (1, 3))
    n_t = SEQ // HGRN_TILE
    width = HGRN_GROUP * HEAD_DIM
    groups = HEADS // HGRN_GROUP
    tile = lambda off: pl.BlockSpec((HGRN_TILE, width), lambda b, g, t: (b * n_t + t, g + off))
    return pl.pallas_call(
        _hgrn_kernel,
        grid=(BATCH, groups, n_t),
        in_specs=[tile(0), tile(0), tile(0), tile(0), tile(groups),
                  pl.BlockSpec((1, width), lambda b, g, t: (0, g)),
                  _resident(pm.shape), _resident(masks.shape)],
        out_specs=tile(0),
        out_shape=jax.ShapeDtypeStruct((M_TOKENS, D_MODEL), BF16),
        scratch_shapes=[pltpu.VMEM((HGRN_GROUP, HEAD_DIM, HEAD_DIM), F32)],
        compiler_params=_params(("parallel", "parallel", "arbitrary")),
        name="hgrn",
    )(qg, logf, key, val, qg, norm_w, jnp.asarray(pm, BF16), jnp.asarray(masks, BF16))


MIX_TM = 512


def _mix_kernel(ya_ref, yb_ref, gate_ref, x_ref, wa_ref, wb_ref, wo_ref, lnw_ref, lnb_ref,
                o_ref, ob_ref):
    za = jnp.dot(ya_ref[...], wa_ref[...], preferred_element_type=F32)
    zb = jnp.dot(yb_ref[...], wb_ref[...], preferred_element_type=F32)
    merged = gate_ref[:, :D_MODEL] * za + gate_ref[:, D_MODEL:] * zb
    mixed = jnp.dot(merged.astype(BF16), wo_ref[...], preferred_element_type=F32)
    y = _layer_norm(DN_ALPHA * x_ref[...] + mixed, lnw_ref[...], lnb_ref[...])
    o_ref[...] = y
    ob_ref[...] = y.astype(BF16)


def _mix(ya, yb, gates, x, wa, wb, wo, lnw, lnb):
    d = D_MODEL
    row = lambda n: pl.BlockSpec((MIX_TM, n), lambda i: (i, 0))
    return pl.pallas_call(
        _mix_kernel,
        grid=(M_TOKENS // MIX_TM,),
        in_specs=[row(d), row(d), row(2 * d), row(d),
                  _resident((d, d)), _resident((d, d)), _resident((d, d)),
                  _resident((1, d)), _resident((1, d))],
        out_specs=[row(d), row(d)],
        out_shape=[jax.ShapeDtypeStruct((M_TOKENS, d), F32),
                   jax.ShapeDtypeStruct((M_TOKENS, d), BF16)],
        compiler_params=_params(("parallel",)),
        name="mix",
    )(ya, yb, gates, x, wa, wb, wo, lnw, lnb)


FFN_TM = 512
FFN_CHUNK = 1408


def _ffn_kernel(xb_ref, x_ref, wg_ref, wu_ref, wd_ref, lnw_ref, lnb_ref, o_ref):
    xb = xb_ref[...]
    y = jnp.zeros((FFN_TM, D_MODEL), F32)
    for c in range(D_FF // FFN_CHUNK):
        cs = slice(c * FFN_CHUNK, (c + 1) * FFN_CHUNK)
        hg = jnp.dot(xb, wg_ref[:, cs], preferred_element_type=F32)
        hu = jnp.dot(xb, wu_ref[:, cs], preferred_element_type=F32)
        act = (hg * _sigmoid(hg) * hu).astype(BF16)
        y = y + jnp.dot(act, wd_ref[cs, :], preferred_element_type=F32)
    o_ref[...] = _layer_norm(DN_ALPHA * x_ref[...] + y, lnw_ref[...], lnb_ref[...])


def _ffn(xb, x, wg, wu, wd, lnw, lnb):
    d = D_MODEL
    row = lambda n: pl.BlockSpec((FFN_TM, n), lambda i: (i, 0))
    return pl.pallas_call(
        _ffn_kernel,
        grid=(M_TOKENS // FFN_TM,),
        in_specs=[row(d), row(d), _resident((d, D_FF)), _resident((d, D_FF)),
                  _resident((D_FF, d)), _resident((1, d)), _resident((1, d))],
        out_specs=row(d),
        out_shape=jax.ShapeDtypeStruct((M_TOKENS, d), F32),
        compiler_params=_params(("parallel",)),
        name="ffn",
    )(xb, x, wg, wu, wd, lnw, lnb)


def _rope_tables():
    half = HEAD_DIM // 2
    inv_freq = ROPE_THETA ** (-jnp.arange(half, dtype=F32) / half)
    ang = jnp.arange(SEQ, dtype=F32)[:, None] * inv_freq[None, :]
    cos, sin = jnp.cos(ang), jnp.sin(ang)
    return jnp.concatenate([cos, cos], axis=1), jnp.concatenate([-sin, sin], axis=1)


def _layer(x2, w_in, lb_logits, hgrn_norm_w, w_branch_a, w_branch_b, b_gate, w_out,
           ln1_w, ln1_b, w_ffn_in, w_ffn_down, ln2_w, ln2_b):
    d = D_MODEL
    wseg = lambda a, b: w_in[:, a:b].astype(BF16)
    row2 = lambda v: v.reshape(1, -1).astype(F32)
    cos, sin = _rope_tables()

    w_qg = jnp.concatenate([w_in[:, 0:d], w_in[:, 3 * d:4 * d]], axis=1).astype(BF16)
    q_scaled, bias, k_rot, v_t, gates, qg, logf, key, val = _proj(
        x2, wseg(4 * d, 5 * d), wseg(5 * d, 6 * d), wseg(6 * d, 7 * d), wseg(7 * d, 9 * d),
        w_qg, wseg(d, 2 * d), wseg(2 * d, 3 * d), cos, sin, row2(b_gate), lb_logits.astype(F32))
    y_a = _hgrn(qg, logf, key, val, row2(hgrn_norm_w))
    y_b = _moba_attention(q_scaled, k_rot, v_t, bias)

    x1, x1b = _mix(y_a, y_b, gates, x2, w_branch_a.astype(BF16), w_branch_b.astype(BF16),
                   w_out.astype(BF16), row2(ln1_w), row2(ln1_b))
    return _ffn(x1b, x1, w_ffn_in[:, :D_FF].astype(BF16), w_ffn_in[:, D_FF:].astype(BF16),
                w_ffn_down.astype(BF16), row2(ln2_w), row2(ln2_b))


def kernel(x, w_in, lb_logits, hgrn_norm_w, w_branch_a, w_branch_b, b_gate, w_out, ln1_w, ln1_b,
           w_ffn_in, w_ffn_down, ln2_w, ln2_b):
    assert DEPTH == 1
    h = x.reshape(M_TOKENS, D_MODEL)
    h = _layer(h, w_in[0], lb_logits, hgrn_norm_w[0], w_branch_a[0], w_branch_b[0],
               b_gate[0], w_out[0], ln1_w[0], ln1_b[0], w_ffn_in[0], w_ffn_down[0],
               ln2_w[0], ln2_b[0])
    return h.reshape(BATCH, SEQ, D_MODEL)
```

```python
import numpy as np
import jax
import jax.numpy as jnp
from jax import lax
from jax.experimental import pallas as pl
from jax.experimental.pallas import tpu as pltpu

D_MODEL = 1024
BATCH = 4
SEQ = 4096
DEPTH = 1
HEADS = 8
HEAD_DIM = 128
HGRN_CHUNK = 64
HGRN_LEVELS = 6
MOBA_BLOCK = 256
MOBA_NBLK = SEQ // MOBA_BLOCK
MOBA_TOPK = 3
ROPE_THETA = 10000.0
D_FF = 2816
DN_ALPHA = (2.0 * DEPTH) ** 0.25
LN_EPS = 1e-5
RMS_EPS = 1e-6
M_TOKENS = BATCH * SEQ

MASK_VALUE = -1e30
LOG2_E = 1.4426950408889634
VMEM_LIMIT = 56 * 1024 * 1024

F32 = jnp.float32
BF16 = jnp.bfloat16
NT_DIMS = (((1,), (1,)), ((), ()))
TN_DIMS = (((0,), (0,)), ((), ()))


def _params(semantics):
    return pltpu.CompilerParams(dimension_semantics=semantics, vmem_limit_bytes=VMEM_LIMIT)


def _resident(shape):
    return pl.BlockSpec(shape, lambda *_: (0,) * len(shape), pipeline_mode=pl.Buffered(1))


def _sigmoid(z):
    return 1.0 / (1.0 + jnp.exp(-z))


def _layer_norm(r, w, b):
    mu = jnp.mean(r, axis=-1, keepdims=True)
    d = r - mu
    var = jnp.mean(d * d, axis=-1, keepdims=True)
    return d * lax.rsqrt(var + LN_EPS) * w + b


def _rope(t, cos, sin_signed):
    outs = []
    for h in range(HEADS):
        th = t[:, h * HEAD_DIM:(h + 1) * HEAD_DIM]
        outs.append(th * cos + pltpu.roll(th, HEAD_DIM // 2, 1) * sin_signed)
    return jnp.concatenate(outs, axis=1)


def _proj_kernel(x_ref, wq_ref, wk_ref, wv_ref, wg_ref, wqg_ref, wf_ref, wi_ref, cos_ref, sin_ref,
                 bg_ref, lbl_ref, nw_ref, pm_ref, mask_ref,
                 q_ref, bias_ref, k_ref, vt_ref, gate_ref, ya_ref,
                 km_ref, state_ref, hq_ref, hog_ref, hlf_ref, hk_ref, hv_ref):
    own = pl.program_id(1)
    lanes = HEADS * MOBA_NBLK

    @pl.when(own == 0)
    def _():
        km_ref[...] = jnp.zeros_like(km_ref)
        state_ref[...] = jnp.zeros_like(state_ref)

    xb = x_ref[...].astype(BF16)

    acc = jnp.dot(xb, wqg_ref[...], preferred_element_type=F32)
    act = acc * _sigmoid(acc)
    hq_ref[...] = act[:, :D_MODEL].astype(hq_ref.dtype)
    hog_ref[...] = act[:, D_MODEL:].astype(hog_ref.dtype)
    l0, l1 = lbl_ref[0:1, :], lbl_ref[1:2, :]
    top = jnp.maximum(l0, l1)
    e0, e1 = jnp.exp(l0 - top), jnp.exp(l1 - top)
    lb = e0 / (e0 + e1)
    z = jnp.dot(xb, wf_ref[...], preferred_element_type=F32)
    hlf_ref[...] = jnp.log(lb + (1.0 - lb) * _sigmoid(z)) * LOG2_E
    hk_ref[...] = ((1.0 - lb) * _sigmoid(-z)).astype(hk_ref.dtype)
    hv_ref[...] = jnp.dot(xb, wi_ref[...], preferred_element_type=F32).astype(hv_ref.dtype)
    _hgrn_tile(hq_ref, hlf_ref, hk_ref, hv_ref, hog_ref, nw_ref, pm_ref, mask_ref, ya_ref, state_ref)

    q = _rope(jnp.dot(xb, wq_ref[...], preferred_element_type=F32), cos_ref[...], sin_ref[...])
    q_ref[...] = (q * (HEAD_DIM ** -0.5 * LOG2_E)).astype(q_ref.dtype)

    split = lambda t: (t.astype(BF16), (t - t.astype(BF16).astype(F32)).astype(BF16))
    q_hi, q_lo = split(q)
    km_hi, km_lo = split(km_ref[...])
    gate = lax.dot_general(jnp.concatenate([q_hi, q_lo, q_hi], axis=1),
                           jnp.concatenate([km_hi, km_hi, km_lo], axis=1), NT_DIMS,
                           preferred_element_type=F32)
    lane = lax.broadcasted_iota(jnp.int32, (MOBA_BLOCK, lanes), 1)
    eligible = lane < own * HEADS
    gate = jnp.where(eligible, gate, -jnp.inf)
    rank = jnp.zeros((MOBA_BLOCK, lanes), F32)
    for r in range(1, MOBA_NBLK):
        partner = pltpu.roll(gate, lanes - r * HEADS, 1)
        wrapped = lane >= lanes - r * HEADS
        beats = (partner > gate) | (wrapped & (partner == gate))
        rank = rank + beats.astype(F32)
    selected = eligible & (rank < float(MOBA_TOPK))
    bias_ref[0] = jnp.where(selected, 0.0, MASK_VALUE).T

    rot = _rope(jnp.dot(xb, wk_ref[...], preferred_element_type=F32), cos_ref[...], sin_ref[...])
    k_ref[...] = rot.astype(k_ref.dtype)
    mean = jnp.mean(rot, axis=0, keepdims=True)
    col_head = lax.broadcasted_iota(jnp.int32, (HEADS, D_MODEL), 1) // HEAD_DIM
    row_head = lax.broadcasted_iota(jnp.int32, (HEADS, D_MODEL), 0)
    km_ref[pl.ds(pl.multiple_of(own * HEADS, HEADS), HEADS), :] = jnp.where(
        col_head == row_head, jnp.broadcast_to(mean, (HEADS, D_MODEL)), 0.0)
    v = jnp.dot(xb, wv_ref[...], preferred_element_type=F32)
    vt_ref[0, 0] = v.T.astype(vt_ref.dtype)
    acc = jnp.dot(xb, wg_ref[...], preferred_element_type=F32)
    gate_ref[...] = _sigmoid(acc + bg_ref[...]).astype(gate_ref.dtype)


def _proj(x, w_q, w_k, w_v, w_g, w_qg, w_f, w_i, cos, sin, b_gate, lb_logits, norm_w):
    d = D_MODEL
    lanes = HEADS * MOBA_NBLK
    pm, masks = _hgrn_constants()
    pm = np.tile(pm, (1, 3))
    row = lambda n: pl.BlockSpec((MOBA_BLOCK, n), lambda b, i: (b * MOBA_NBLK + i, 0))
    rope = pl.BlockSpec((MOBA_BLOCK, HEAD_DIM), lambda b, i: (i, 0))
    shape = lambda n, dt: jax.ShapeDtypeStruct((M_TOKENS, n), dt)
    tile = lambda dt: pltpu.VMEM((MOBA_BLOCK, d), dt)
    return pl.pallas_call(
        _proj_kernel,
        grid=(BATCH, MOBA_NBLK),
        in_specs=[row(d), _resident((d, d)), _resident((d, d)), _resident((d, d)),
                  _resident((d, 2 * d)), _resident((d, 2 * d)), _resident((d, d)), _resident((d, d)),
                  rope, rope, _resident((1, 2 * d)), _resident((DEPTH + 1, d)),
                  _resident((1, d)), _resident(pm.shape), _resident(masks.shape)],
        out_specs=[row(d),
                   pl.BlockSpec((1, lanes, MOBA_BLOCK), lambda b, i: (b, 0, i)),
                   row(d),
                   pl.BlockSpec((1, 1, d, MOBA_BLOCK), lambda b, i: (b, i, 0, 0)),
                   row(2 * d), row(d)],
        out_shape=[shape(d, BF16),
                   jax.ShapeDtypeStruct((BATCH, lanes, SEQ), F32),
                   shape(d, BF16),
                   jax.ShapeDtypeStruct((BATCH, MOBA_NBLK, d, MOBA_BLOCK), BF16),
                   shape(2 * d, BF16), shape(d, BF16)],
        scratch_shapes=[pltpu.VMEM((lanes, d), F32),
                        pltpu.VMEM((HEADS, HEAD_DIM, HEAD_DIM), F32),
                        tile(BF16), tile(BF16), tile(F32), tile(BF16), tile(BF16)],
        compiler_params=_params(("arbitrary", "arbitrary")),
        name="proj",
    )(x, w_q, w_k, w_v, w_g, w_qg, w_f, w_i, cos, sin, b_gate, lb_logits, norm_w,
      jnp.asarray(pm, BF16), jnp.asarray(masks, BF16))


def _moba_att_kernel(q_ref, k_ref, vt_ref, bias_ref, o_ref, acc_ref, s_ref):
    own = pl.program_id(1)

    def head(h):
        return slice(h * HEAD_DIM, (h + 1) * HEAD_DIM)

    def scores(h, j):
        kj = k_ref[pl.ds(pl.multiple_of(j * MOBA_BLOCK, MOBA_BLOCK), MOBA_BLOCK), head(h)]
        return lax.dot_general(kj, q_ref[:, head(h)], NT_DIMS, preferred_element_type=F32)

    def item(h, j, s, m, l, next_scores):
        m_new = jnp.maximum(m, jnp.max(s, axis=0, keepdims=True))
        a = jnp.exp2(m - m_new)
        p = jnp.exp2(s - m_new)
        if next_scores is not None:
            s_ref[h] = next_scores()
        pv = jnp.dot(vt_ref[0, j, head(h), :], p.astype(BF16), preferred_element_type=F32)
        acc_ref[h] = a * acc_ref[h] + pv
        return m_new, a * l + jnp.sum(p, axis=0, keepdims=True)

    acc_ref[...] = jnp.zeros_like(acc_ref)
    for h in range(HEADS):
        s_ref[h] = scores(h, 0)
    m0 = jnp.full((1, MOBA_BLOCK), MASK_VALUE, F32)
    l0 = jnp.zeros((1, MOBA_BLOCK), F32)

    def body(j, carry):
        ms, ls = list(carry[0]), list(carry[1])
        for h in range(HEADS):
            s = s_ref[h] + bias_ref[0, pl.ds(j * HEADS + h, 1), :]
            ms[h], ls[h] = item(h, j, s, ms[h], ls[h], lambda h=h: scores(h, j + 1))
        return tuple(ms), tuple(ls)

    ms, ls = lax.fori_loop(0, own, body, ((m0,) * HEADS, (l0,) * HEADS))

    key_pos = lax.broadcasted_iota(jnp.int32, (MOBA_BLOCK, MOBA_BLOCK), 0)
    qry_pos = lax.broadcasted_iota(jnp.int32, (MOBA_BLOCK, MOBA_BLOCK), 1)
    causal = key_pos <= qry_pos
    for h in range(HEADS):
        s = jnp.where(causal, s_ref[h], MASK_VALUE)
        _, l = item(h, own, s, ms[h], ls[h], None)
        o_ref[:, head(h)] = (acc_ref[h] * (1.0 / l)).T.astype(o_ref.dtype)


def _moba_attention(q_scaled, k_rot, v_t, bias):
    lanes = HEADS * MOBA_NBLK
    return pl.pallas_call(
        _moba_att_kernel,
        grid=(BATCH, MOBA_NBLK),
        in_specs=[pl.BlockSpec((MOBA_BLOCK, D_MODEL), lambda b, i: (b * MOBA_NBLK + i, 0)),
                  pl.BlockSpec((SEQ, D_MODEL), lambda b, i: (b, 0)),
                  pl.BlockSpec((1, MOBA_NBLK, D_MODEL, MOBA_BLOCK), lambda b, i: (b, 0, 0, 0)),
                  pl.BlockSpec((1, lanes, MOBA_BLOCK), lambda b, i: (b, 0, i))],
        out_specs=pl.BlockSpec((MOBA_BLOCK, D_MODEL), lambda b, i: (b * MOBA_NBLK + i, 0)),
        out_shape=jax.ShapeDtypeStruct((M_TOKENS, D_MODEL), BF16),
        scratch_shapes=[pltpu.VMEM((HEADS, HEAD_DIM, MOBA_BLOCK), F32),
                        pltpu.VMEM((HEADS, MOBA_BLOCK, MOBA_BLOCK), F32)],
        compiler_params=_params(("parallel", "arbitrary")),
        name="moba_att",
    )(q_scaled, k_rot, v_t, bias)


HGRN_TILE = MOBA_BLOCK
PM_LEVELS = (1, 2)


def _hgrn_constants():
    c = HGRN_CHUNK
    idx = np.arange(c)
    tri = (idx[None, :] <= idx[:, None]).astype(np.float32)
    rows = [tri]
    masks = []
    for lvl in range(HGRN_LEVELS):
        half = 1 << lvl
        group = idx // (2 * half)
        upper = (idx & half) != 0
        if lvl in PM_LEVELS:
            boundary = tri[group * (2 * half) + half - 1]
            rows.append(np.where(upper[:, None], tri - boundary, boundary - tri))
        masks.append(((group[:, None] == group[None, :]) & upper[:, None] & ~upper[None, :])
                     .astype(np.float32))
    masks.append(np.eye(c, dtype=np.float32))
    return np.concatenate(rows, axis=0), np.stack(masks)


def _hgrn_tile(q_ref, lf_ref, k_ref, v_ref, og_ref, nw_ref, pm_ref, mask_ref, o_ref, state_ref):
    c = HGRN_CHUNK
    n_ch = HGRN_TILE // c
    units = [(ch, h) for ch in range(n_ch) for h in range(HEADS)]
    rows = lambda ch: slice(ch * c, (ch + 1) * c)
    lanes = lambda h: slice(h * HEAD_DIM, (h + 1) * HEAD_DIM)
    chunks_on_lanes = lambda t: jnp.concatenate([t[rows(ch), :] for ch in range(n_ch)], axis=1)

    lf = lf_ref[...]
    hi = lf.astype(BF16)
    rem = lf - hi.astype(F32)
    mid = rem.astype(BF16)
    lo = (rem - mid.astype(F32)).astype(BF16)
    split = jnp.concatenate([chunks_on_lanes(hi), chunks_on_lanes(mid), chunks_on_lanes(lo)], axis=0)
    sums = jnp.dot(pm_ref[...], split, preferred_element_type=F32)

    row = lax.broadcasted_iota(jnp.int32, (c, HEAD_DIM), 0)
    upper = [(row & (1 << lvl)) != 0 for lvl in range(HGRN_LEVELS)]

    def decayed_operands(u, ch, h):
        col = slice(u * HEAD_DIM, (u + 1) * HEAD_DIM)
        g = sums[0:c, col]
        q_bf, k_bf = q_ref[rows(ch), lanes(h)], k_ref[rows(ch), lanes(h)]
        qf, kk = q_bf.astype(F32), k_bf.astype(F32)
        zs = []
        for lvl in range(HGRN_LEVELS):
            half = 1 << lvl
            if lvl == 0:
                e = jnp.where(upper[0], lf_ref[rows(ch), lanes(h)], 0.0)
            elif lvl in PM_LEVELS:
                i = PM_LEVELS.index(lvl)
                e = sums[(i + 1) * c:(i + 2) * c, col]
            else:
                boundary = jnp.concatenate(
                    [jnp.broadcast_to(g[b + half - 1:b + half, :], (2 * half, HEAD_DIM))
                     for b in range(0, c, 2 * half)], axis=0)
                e = -jnp.abs(g - boundary)
            zs.append((jnp.where(upper[lvl], qf, kk) * jnp.exp2(e)).astype(BF16))
        g_last = g[c - 1:c, :]
        return dict(q=q_bf, k=k_bf, zs=zs,
                    q_in=(qf * jnp.exp2(g)).astype(BF16),
                    k_out=(kk * jnp.exp2(g_last - g)).astype(BF16),
                    decay=jnp.exp2(g_last),
                    v=v_ref[rows(ch), lanes(h)])

    def state_free_matmuls(op):
        score = lambda lhs, rhs: lax.dot_general(lhs, rhs, NT_DIMS,
                                                 preferred_element_type=F32).astype(BF16)
        a = score(op["q"], op["k"]) * mask_ref[HGRN_LEVELS]
        for lvl, z in enumerate(op["zs"]):
            a = a + score(z, z) * mask_ref[lvl]
        op["o"] = jnp.dot(a, op["v"], preferred_element_type=F32)
        op["update"] = lax.dot_general(op["v"], op["k_out"], TN_DIMS, preferred_element_type=F32)

    ops = {}
    for u, (ch, h) in enumerate(units):
        ops[ch, h] = decayed_operands(u, ch, h)
        if u >= 1:
            state_free_matmuls(ops[units[u - 1]])
    state_free_matmuls(ops[units[-1]])

    for h in range(HEADS):
        state_t = state_ref[h]
        for ch in range(n_ch):
            op = ops[ch, h]
            op["state"] = state_t.astype(BF16)
            state_t = state_t * op["decay"] + op["update"]
        state_ref[h] = state_t
    for ch, h in units:
        op = ops[ch, h]
        o = op["o"] + lax.dot_general(op["q_in"], op["state"], NT_DIMS, preferred_element_type=F32)
        ms = jnp.mean(o * o, axis=-1, keepdims=True)
        y = o * lax.rsqrt(ms + RMS_EPS) * nw_ref[:, lanes(h)] * og_ref[rows(ch), lanes(h)]
        o_ref[rows(ch), lanes(h)] = y.astype(o_ref.dtype)


MIX_TM = 512


def _mix_kernel(ya_ref, yb_ref, gate_ref, x_ref, wa_ref, wb_ref, wo_ref, lnw_ref, lnb_ref,
                o_ref, ob_ref):
    za = jnp.dot(ya_ref[...], wa_ref[...], preferred_element_type=F32)
    zb = jnp.dot(yb_ref[...], wb_ref[...], preferred_element_type=F32)
    merged = gate_ref[:, :D_MODEL] * za + gate_ref[:, D_MODEL:] * zb
    mixed = jnp.dot(merged.astype(BF16), wo_ref[...], preferred_element_type=F32)
    y = _layer_norm(DN_ALPHA * x_ref[...] + mixed, lnw_ref[...], lnb_ref[...])
    o_ref[...] = y
    ob_ref[...] = y.astype(BF16)


def _mix(ya, yb, gates, x, wa, wb, wo, lnw, lnb):
    d = D_MODEL
    row = lambda n: pl.BlockSpec((MIX_TM, n), lambda i: (i, 0))
    return pl.pallas_call(
        _mix_kernel,
        grid=(M_TOKENS // MIX_TM,),
        in_specs=[row(d), row(d), row(2 * d), row(d),
                  _resident((d, d)), _resident((d, d)), _resident((d, d)),
                  _resident((1, d)), _resident((1, d))],
        out_specs=[row(d), row(d)],
        out_shape=[jax.ShapeDtypeStruct((M_TOKENS, d), F32),
                   jax.ShapeDtypeStruct((M_TOKENS, d), BF16)],
        compiler_params=_params(("parallel",)),
        name="mix",
    )(ya, yb, gates, x, wa, wb, wo, lnw, lnb)


FFN_TM = 512
FFN_CHUNK = 1408


def _ffn_kernel(xb_ref, x_ref, wg_ref, wu_ref, wd_ref, lnw_ref, lnb_ref, o_ref):
    xb = xb_ref[...]
    y = jnp.zeros((FFN_TM, D_MODEL), F32)
    for c in range(D_FF // FFN_CHUNK):
        cs = slice(c * FFN_CHUNK, (c + 1) * FFN_CHUNK)
        hg = jnp.dot(xb, wg_ref[:, cs], preferred_element_type=F32)
        hu = jnp.dot(xb, wu_ref[:, cs], preferred_element_type=F32)
        act = (hg * _sigmoid(hg) * hu).astype(BF16)
        y = y + jnp.dot(act, wd_ref[cs, :], preferred_element_type=F32)
    o_ref[...] = _layer_norm(DN_ALPHA * x_ref[...] + y, lnw_ref[...], lnb_ref[...])


def _ffn(xb, x, wg, wu, wd, lnw, lnb):
    d = D_MODEL
    row = lambda n: pl.BlockSpec((FFN_TM, n), lambda i: (i, 0))
    return pl.pallas_call(
        _ffn_kernel,
        grid=(M_TOKENS // FFN_TM,),
        in_specs=[row(d), row(d), _resident((d, D_FF)), _resident((d, D_FF)),
                  _resident((D_FF, d)), _resident((1, d)), _resident((1, d))],
        out_specs=row(d),
        out_shape=jax.ShapeDtypeStruct((M_TOKENS, d), F32),
        compiler_params=_params(("parallel",)),
        name="ffn",
    )(xb, x, wg, wu, wd, lnw, lnb)


def _rope_tables():
    half = HEAD_DIM // 2
    inv_freq = ROPE_THETA ** (-jnp.arange(half, dtype=F32) / half)
    ang = jnp.arange(SEQ, dtype=F32)[:, None] * inv_freq[None, :]
    cos, sin = jnp.cos(ang), jnp.sin(ang)
    return jnp.concatenate([cos, cos], axis=1), jnp.concatenate([-sin, sin], axis=1)


def _layer(x2, w_in, lb_logits, hgrn_norm_w, w_branch_a, w_branch_b, b_gate, w_out,
           ln1_w, ln1_b, w_ffn_in, w_ffn_down, ln2_w, ln2_b):
    d = D_MODEL
    wseg = lambda a, b: w_in[:, a:b].astype(BF16)
    row2 = lambda v: v.reshape(1, -1).astype(F32)
    cos, sin = _rope_tables()

    w_qg = jnp.concatenate([w_in[:, 0:d], w_in[:, 3 * d:4 * d]], axis=1).astype(BF16)
    q_scaled, bias, k_rot, v_t, gates, y_a = _proj(
        x2, wseg(4 * d, 5 * d), wseg(5 * d, 6 * d), wseg(6 * d, 7 * d), wseg(7 * d, 9 * d),
        w_qg, wseg(d, 2 * d), wseg(2 * d, 3 * d), cos, sin, row2(b_gate), lb_logits.astype(F32),
        row2(hgrn_norm_w))
    y_b = _moba_attention(q_scaled, k_rot, v_t, bias)

    x1, x1b = _mix(y_a, y_b, gates, x2, w_branch_a.astype(BF16), w_branch_b.astype(BF16),
                   w_out.astype(BF16), row2(ln1_w), row2(ln1_b))
    return _ffn(x1b, x1, w_ffn_in[:, :D_FF].astype(BF16), w_ffn_in[:, D_FF:].astype(BF16),
                w_ffn_down.astype(BF16), row2(ln2_w), row2(ln2_b))


def kernel(x, w_in, lb_logits, hgrn_norm_w, w_branch_a, w_branch_b, b_gate, w_out, ln1_w, ln1_b,
           w_ffn_in, w_ffn_down, ln2_w, ln2_b):
    assert DEPTH == 1
    h = x.reshape(M_TOKENS, D_MODEL)
    h = _layer(h, w_in[0], lb_logits, hgrn_norm_w[0], w_branch_a[0], w_branch_b[0],
               b_gate[0], w_out[0], ln1_w[0], ln1_b[0], w_ffn_in[0], w_ffn_down[0],
               ln2_w[0], ln2_b[0])
    return h.reshape(BATCH, SEQ, D_MODEL)
```

```python
import numpy as np
import jax
import jax.numpy as jnp
from jax import lax
from jax.experimental import pallas as pl
from jax.experimental.pallas import tpu as pltpu

D_MODEL = 1024
BATCH = 4
SEQ = 4096
DEPTH = 1
HEADS = 8
HEAD_DIM = 128
HGRN_CHUNK = 64
HGRN_LEVELS = 6
MOBA_BLOCK = 256
MOBA_NBLK = SEQ // MOBA_BLOCK
MOBA_TOPK = 3
ROPE_THETA = 10000.0
D_FF = 2816
DN_ALPHA = (2.0 * DEPTH) ** 0.25
LN_EPS = 1e-5
RMS_EPS = 1e-6
M_TOKENS = BATCH * SEQ

MASK_VALUE = -1e30
VT_ROWS = HEAD_DIM + 16
LOG2_E = 1.4426950408889634
VMEM_LIMIT = 56 * 1024 * 1024

F32 = jnp.float32
BF16 = jnp.bfloat16
NT_DIMS = (((1,), (1,)), ((), ()))
TN_DIMS = (((0,), (0,)), ((), ()))


def _params(semantics):
    return pltpu.CompilerParams(dimension_semantics=semantics, vmem_limit_bytes=VMEM_LIMIT)


def _resident(shape):
    return pl.BlockSpec(shape, lambda *_: (0,) * len(shape), pipeline_mode=pl.Buffered(1))


def _sigmoid(z):
    return 1.0 / (1.0 + jnp.exp(-z))


def _layer_norm(r, w, b):
    mu = jnp.mean(r, axis=-1, keepdims=True)
    d = r - mu
    var = jnp.mean(d * d, axis=-1, keepdims=True)
    return d * lax.rsqrt(var + LN_EPS) * w + b


def _rope(t, cos, sin_signed):
    outs = []
    for h in range(HEADS):
        th = t[:, h * HEAD_DIM:(h + 1) * HEAD_DIM]
        outs.append(th * cos + pltpu.roll(th, HEAD_DIM // 2, 1) * sin_signed)
    return jnp.concatenate(outs, axis=1)


PROJ_SLAB = 256


def _proj_kernel(x_ref, wq_ref, wk_ref, wv_ref, wg_ref, wqg_ref, wf_ref, wi_ref, cos_ref, sin_ref,
                 bg_ref, lbl_ref, nw_ref, pm_ref, mask_ref,
                 q_ref, bias_ref, k_ref, vt_ref, gate_ref, ya_ref,
                 km_ref, state_ref, hq_ref, hog_ref, hlf_ref, hk_ref, hv_ref, xb_ref, qf_ref):
    own = pl.program_id(1)
    lanes = HEADS * MOBA_NBLK

    @pl.when(own == 0)
    def _():
        km_ref[...] = jnp.zeros_like(km_ref)
        state_ref[...] = jnp.zeros_like(state_ref)

    xb_ref[...] = x_ref[...].astype(BF16)
    head = lambda h: slice(h * HEAD_DIM, (h + 1) * HEAD_DIM)
    slab = lambda i: slice(i * PROJ_SLAB, (i + 1) * PROJ_SLAB)
    n_slabs = D_MODEL // PROJ_SLAB
    project = lambda w_ref, cols: jnp.dot(xb_ref[...], w_ref[:, cols], preferred_element_type=F32)
    rope = lambda t: t * cos_ref[...] + pltpu.roll(t, HEAD_DIM // 2, 1) * sin_ref[...]


    def hgrn_silu(i):
        def run():
            acc = project(wqg_ref, slab(i))
            dst = hq_ref if i < n_slabs else hog_ref
            dst[:, slab(i % n_slabs)] = (acc * _sigmoid(acc)).astype(dst.dtype)
        return run

    def hgrn_forget(i):
        def run():
            l0, l1 = lbl_ref[0:1, slab(i)], lbl_ref[1:2, slab(i)]
            top = jnp.maximum(l0, l1)
            e0, e1 = jnp.exp(l0 - top), jnp.exp(l1 - top)
            lb = e0 / (e0 + e1)
            z = project(wf_ref, slab(i))
            hlf_ref[:, slab(i)] = jnp.log(lb + (1.0 - lb) * _sigmoid(z)) * LOG2_E
            hk_ref[:, slab(i)] = ((1.0 - lb) * _sigmoid(-z)).astype(hk_ref.dtype)
        return run

    def hgrn_value(i):
        def run():
            hv_ref[:, slab(i)] = project(wi_ref, slab(i)).astype(hv_ref.dtype)
        return run

    heads_of = lambda i: range(i * PROJ_SLAB // HEAD_DIM, (i + 1) * PROJ_SLAB // HEAD_DIM)
    in_slab = lambda i, h: slice((h - heads_of(i)[0]) * HEAD_DIM, (h - heads_of(i)[0] + 1) * HEAD_DIM)

    def moba_query(i):
        def run():
            acc = project(wq_ref, slab(i))
            for h in heads_of(i):
                qh = rope(acc[:, in_slab(i, h)])
                qf_ref[:, head(h)] = qh
                q_ref[:, head(h)] = (qh * (HEAD_DIM ** -0.5 * LOG2_E)).astype(q_ref.dtype)
        return run

    def moba_choice():
        split = lambda t: (t.astype(BF16), (t - t.astype(BF16).astype(F32)).astype(BF16))
        q_hi, q_lo = split(qf_ref[...])
        km_hi, km_lo = split(km_ref[...])
        gate = lax.dot_general(jnp.concatenate([q_hi, q_lo, q_hi], axis=1),
                               jnp.concatenate([km_hi, km_hi, km_lo], axis=1), NT_DIMS,
                               preferred_element_type=F32)
        lane = lax.broadcasted_iota(jnp.int32, (MOBA_BLOCK, lanes), 1)
        eligible = lane < own * HEADS
        gate = jnp.where(eligible, gate, -jnp.inf)
        rank = jnp.zeros((MOBA_BLOCK, lanes), F32)
        for r in range(1, MOBA_NBLK):
            partner = pltpu.roll(gate, lanes - r * HEADS, 1)
            wrapped = lane >= lanes - r * HEADS
            beats = (partner > gate) | (wrapped & (partner == gate))
            rank = rank + beats.astype(F32)
        selected = eligible & (rank < float(MOBA_TOPK))
        bias_ref[0] = jnp.where(selected, 0.0, MASK_VALUE).T

    def moba_key(i):
        def run():
            acc = project(wk_ref, slab(i))
            for h in heads_of(i):
                rot = rope(acc[:, in_slab(i, h)])
                k_ref[:, head(h)] = rot.astype(k_ref.dtype)
                mean = jnp.mean(rot, axis=0, keepdims=True)
                zeros = lambda n: [jnp.zeros((1, n * HEAD_DIM), F32)] if n else []
                km_ref[pl.ds(own * HEADS + h, 1), :] = jnp.concatenate(
                    zeros(h) + [mean] + zeros(HEADS - 1 - h), axis=1)
        return run

    def moba_value(i):
        def run():
            vt = project(wv_ref, slab(i)).T.astype(vt_ref.dtype)
            pad_row = lax.broadcasted_iota(jnp.int32, (VT_ROWS - HEAD_DIM, MOBA_BLOCK), 0)
            ones_row = jnp.where(pad_row == 0, 1.0, 0.0).astype(vt_ref.dtype)
            for h in heads_of(i):
                vt_ref[0, 0, h * VT_ROWS:h * VT_ROWS + HEAD_DIM, :] = vt[in_slab(i, h), :]
                vt_ref[0, 0, h * VT_ROWS + HEAD_DIM:(h + 1) * VT_ROWS, :] = ones_row
        return run

    def merge_gate(i):
        def run():
            acc = project(wg_ref, slab(i)) + bg_ref[:, slab(i)]
            gate_ref[:, slab(i)] = _sigmoid(acc).astype(gate_ref.dtype)
        return run

    hgrn_inputs = ([hgrn_silu(i) for i in range(2 * n_slabs)]
                   + [hgrn_forget(i) for i in range(n_slabs)]
                   + [hgrn_value(i) for i in range(n_slabs)])
    queries = [moba_query(i) for i in range(n_slabs)] + [moba_choice]
    for i, piece in enumerate(hgrn_inputs):
        piece()
        if i < len(queries):
            queries[i]()

    rest = ([moba_key(i) for i in range(n_slabs)] + [moba_value(i) for i in range(n_slabs)]
            + [merge_gate(i) for i in range(2 * n_slabs)])
    calls = []

    def fill():
        calls.append(None)
        if rest and len(calls) % 2 == 0:
            rest.pop(0)()

    _hgrn_tile(hq_ref, hlf_ref, hk_ref, hv_ref, hog_ref, nw_ref, pm_ref, mask_ref, ya_ref,
               state_ref, fill)
    while rest:
        fill()


def _proj(x, w_q, w_k, w_v, w_g, w_qg, w_f, w_i, cos, sin, b_gate, lb_logits, norm_w):
    d = D_MODEL
    lanes = HEADS * MOBA_NBLK
    pm, masks = _hgrn_constants()
    pm = np.tile(pm, (1, 3))
    row = lambda n: pl.BlockSpec((MOBA_BLOCK, n), lambda b, i: (b * MOBA_NBLK + i, 0))
    rope = pl.BlockSpec((MOBA_BLOCK, HEAD_DIM), lambda b, i: (i, 0))
    shape = lambda n, dt: jax.ShapeDtypeStruct((M_TOKENS, n), dt)
    tile = lambda dt: pltpu.VMEM((MOBA_BLOCK, d), dt)
    return pl.pallas_call(
        _proj_kernel,
        grid=(BATCH, MOBA_NBLK),
        in_specs=[row(d), _resident((d, d)), _resident((d, d)), _resident((d, d)),
                  _resident((d, 2 * d)), _resident((d, 2 * d)), _resident((d, d)), _resident((d, d)),
                  rope, rope, _resident((1, 2 * d)), _resident((DEPTH + 1, d)),
                  _resident((1, d)), _resident(pm.shape), _resident(masks.shape)],
        out_specs=[row(d),
                   pl.BlockSpec((1, lanes, MOBA_BLOCK), lambda b, i: (b, 0, i)),
                   row(d),
                   pl.BlockSpec((1, 1, HEADS * VT_ROWS, MOBA_BLOCK), lambda b, i: (b, i, 0, 0)),
                   row(2 * d), row(d)],
        out_shape=[shape(d, BF16),
                   jax.ShapeDtypeStruct((BATCH, lanes, SEQ), F32),
                   shape(d, BF16),
                   jax.ShapeDtypeStruct((BATCH, MOBA_NBLK, HEADS * VT_ROWS, MOBA_BLOCK), BF16),
                   shape(2 * d, BF16), shape(d, BF16)],
        scratch_shapes=[pltpu.VMEM((lanes, d), F32),
                        pltpu.VMEM((HEADS, HEAD_DIM, HEAD_DIM), F32),
                        tile(BF16), tile(BF16), tile(F32), tile(BF16), tile(BF16),
                        tile(BF16), tile(F32)],
        compiler_params=_params(("arbitrary", "arbitrary")),
        name="proj",
    )(x, w_q, w_k, w_v, w_g, w_qg, w_f, w_i, cos, sin, b_gate, lb_logits, norm_w,
      jnp.asarray(pm, BF16), jnp.asarray(masks, BF16))


def _moba_att_kernel(q_ref, k_ref, vt_ref, bias_ref, o_ref, acc_ref, s_ref):
    own = pl.program_id(1)

    def head(h):
        return slice(h * HEAD_DIM, (h + 1) * HEAD_DIM)

    def scores(h, j):
        kj = k_ref[pl.ds(pl.multiple_of(j * MOBA_BLOCK, MOBA_BLOCK), MOBA_BLOCK), head(h)]
        return lax.dot_general(kj, q_ref[:, head(h)], NT_DIMS, preferred_element_type=F32)

    def item(h, j, s, m, l, next_scores, bias=None):
        top = jnp.max(s, axis=0, keepdims=True)
        m_new = jnp.maximum(m, top if bias is None else top + bias)
        a = jnp.exp2(m - m_new)
        p = jnp.exp2(s - (m_new if bias is None else m_new - 2.0 * bias))
        if next_scores is not None:
            s_ref[h] = next_scores()
        pv = jnp.dot(vt_ref[0, j, h * VT_ROWS:(h + 1) * VT_ROWS, :], p.astype(BF16),
                     preferred_element_type=F32)
        acc_ref[h] = a * acc_ref[h] + pv[:HEAD_DIM]
        return m_new, a * l + pv[HEAD_DIM:HEAD_DIM + 1]

    acc_ref[...] = jnp.zeros_like(acc_ref)
    for h in range(HEADS):
        s_ref[h] = scores(h, 0)
    m0 = jnp.full((1, MOBA_BLOCK), MASK_VALUE, F32)
    l0 = jnp.zeros((1, MOBA_BLOCK), F32)

    def body(j, carry):
        ms, ls = list(carry[0]), list(carry[1])
        for h in range(HEADS):
            ms[h], ls[h] = item(h, j, s_ref[h], ms[h], ls[h], lambda h=h: scores(h, j + 1),
                                bias_ref[0, pl.ds(j * HEADS + h, 1), :])
        return tuple(ms), tuple(ls)

    ms, ls = lax.fori_loop(0, own, body, ((m0,) * HEADS, (l0,) * HEADS))

    key_pos = lax.broadcasted_iota(jnp.int32, (MOBA_BLOCK, MOBA_BLOCK), 0)
    qry_pos = lax.broadcasted_iota(jnp.int32, (MOBA_BLOCK, MOBA_BLOCK), 1)
    causal = key_pos <= qry_pos
    for h in range(HEADS):
        s = jnp.where(causal, s_ref[h], MASK_VALUE)
        _, l = item(h, own, s, ms[h], ls[h], None)
        o_ref[:, head(h)] = (acc_ref[h] * (1.0 / l)).T.astype(o_ref.dtype)


def _moba_attention(q_scaled, k_rot, v_t, bias):
    lanes = HEADS * MOBA_NBLK
    return pl.pallas_call(
        _moba_att_kernel,
        grid=(BATCH, MOBA_NBLK),
        in_specs=[pl.BlockSpec((MOBA_BLOCK, D_MODEL), lambda b, i: (b * MOBA_NBLK + i, 0)),
                  pl.BlockSpec((SEQ, D_MODEL), lambda b, i: (b, 0)),
                  pl.BlockSpec((1, MOBA_NBLK, HEADS * VT_ROWS, MOBA_BLOCK),
                               lambda b, i: (b, 0, 0, 0)),
                  pl.BlockSpec((1, lanes, MOBA_BLOCK), lambda b, i: (b, 0, i))],
        out_specs=pl.BlockSpec((MOBA_BLOCK, D_MODEL), lambda b, i: (b * MOBA_NBLK + i, 0)),
        out_shape=jax.ShapeDtypeStruct((M_TOKENS, D_MODEL), BF16),
        scratch_shapes=[pltpu.VMEM((HEADS, HEAD_DIM, MOBA_BLOCK), F32),
                        pltpu.VMEM((HEADS, MOBA_BLOCK, MOBA_BLOCK), F32)],
        compiler_params=_params(("parallel", "arbitrary")),
        name="moba_att",
    )(q_scaled, k_rot, v_t, bias)


HGRN_TILE = MOBA_BLOCK
PM_LEVELS = (1, 2)


def _hgrn_constants():
    c = HGRN_CHUNK
    idx = np.arange(c)
    tri = (idx[None, :] <= idx[:, None]).astype(np.float32)
    rows = [tri]
    masks = []
    for lvl in range(HGRN_LEVELS):
        half = 1 << lvl
        group = idx // (2 * half)
        upper = (idx & half) != 0
        if lvl in PM_LEVELS:
            boundary = tri[group * (2 * half) + half - 1]
            rows.append(np.where(upper[:, None], tri - boundary, boundary - tri))
        masks.append(((group[:, None] == group[None, :]) & upper[:, None] & ~upper[None, :])
                     .astype(np.float32))
    masks.append(np.eye(c, dtype=np.float32))
    return np.concatenate(rows, axis=0), np.stack(masks)


def _hgrn_tile(q_ref, lf_ref, k_ref, v_ref, og_ref, nw_ref, pm_ref, mask_ref, o_ref, state_ref,
               fill):
    c = HGRN_CHUNK
    n_ch = HGRN_TILE // c
    units = [(ch, h) for ch in range(n_ch) for h in range(HEADS)]
    rows = lambda ch: slice(ch * c, (ch + 1) * c)
    lanes = lambda h: slice(h * HEAD_DIM, (h + 1) * HEAD_DIM)
    chunks_on_lanes = lambda t: jnp.concatenate([t[rows(ch), :] for ch in range(n_ch)], axis=1)

    lf = lf_ref[...]
    hi = lf.astype(BF16)
    rem = lf - hi.astype(F32)
    mid = rem.astype(BF16)
    lo = (rem - mid.astype(F32)).astype(BF16)
    split = jnp.concatenate([chunks_on_lanes(hi), chunks_on_lanes(mid), chunks_on_lanes(lo)], axis=0)
    sums = jnp.dot(pm_ref[...], split, preferred_element_type=F32)

    row = lax.broadcasted_iota(jnp.int32, (c, HEAD_DIM), 0)
    upper = [(row & (1 << lvl)) != 0 for lvl in range(HGRN_LEVELS)]

    def decayed_operands(u, ch, h):
        col = slice(u * HEAD_DIM, (u + 1) * HEAD_DIM)
        g = sums[0:c, col]
        q_bf, k_bf = q_ref[rows(ch), lanes(h)], k_ref[rows(ch), lanes(h)]
        qf, kk = q_bf.astype(F32), k_bf.astype(F32)
        zs = []
        for lvl in range(HGRN_LEVELS):
            half = 1 << lvl
            if lvl == 0:
                e = jnp.where(upper[0], lf_ref[rows(ch), lanes(h)], 0.0)
            elif lvl in PM_LEVELS:
                i = PM_LEVELS.index(lvl)
                e = sums[(i + 1) * c:(i + 2) * c, col]
            else:
                boundary = jnp.concatenate(
                    [jnp.broadcast_to(g[b + half - 1:b + half, :], (2 * half, HEAD_DIM))
                     for b in range(0, c, 2 * half)], axis=0)
                e = -jnp.abs(g - boundary)
            zs.append((jnp.where(upper[lvl], qf, kk) * jnp.exp2(e)).astype(BF16))
        g_last = g[c - 1:c, :]
        return dict(q=q_bf, k=k_bf, zs=zs,
                    q_in=(qf * jnp.exp2(g)).astype(BF16),
                    k_out=(kk * jnp.exp2(g_last - g)).astype(BF16),
                    decay=jnp.exp2(g_last),
                    v=v_ref[rows(ch), lanes(h)])

    def state_free_matmuls(op):
        score = lambda lhs, rhs: lax.dot_general(lhs, rhs, NT_DIMS,
                                                 preferred_element_type=F32).astype(BF16)
        a = score(op["q"], op["k"]) * mask_ref[HGRN_LEVELS]
        for lvl, z in enumerate(op["zs"]):
            a = a + score(z, z) * mask_ref[lvl]
        op["o"] = jnp.dot(a, op["v"], preferred_element_type=F32)
        op["update"] = lax.dot_general(op["v"], op["k_out"], TN_DIMS, preferred_element_type=F32)

    ops = {}
    for u, (ch, h) in enumerate(units):
        ops[ch, h] = decayed_operands(u, ch, h)
        if u >= 1:
            state_free_matmuls(ops[units[u - 1]])
        fill()
    state_free_matmuls(ops[units[-1]])

    for h in range(HEADS):
        state_t = state_ref[h]
        for ch in range(n_ch):
            op = ops[ch, h]
            op["state"] = state_t.astype(BF16)
            state_t = state_t * op["decay"] + op["update"]
        state_ref[h] = state_t
    for ch, h in units:
        op = ops[ch, h]
        o = op["o"] + lax.dot_general(op["q_in"], op["state"], NT_DIMS, preferred_element_type=F32)
        ms = jnp.mean(o * o, axis=-1, keepdims=True)
        y = o * lax.rsqrt(ms + RMS_EPS) * nw_ref[:, lanes(h)] * og_ref[rows(ch), lanes(h)]
        o_ref[rows(ch), lanes(h)] = y.astype(o_ref.dtype)


MIX_TM = 512


def _mix_kernel(ya_ref, yb_ref, gate_ref, x_ref, wa_ref, wb_ref, wo_ref, lnw_ref, lnb_ref,
                o_ref, ob_ref):
    za = jnp.dot(ya_ref[...], wa_ref[...], preferred_element_type=F32)
    zb = jnp.dot(yb_ref[...], wb_ref[...], preferred_element_type=F32)
    merged = gate_ref[:, :D_MODEL] * za + gate_ref[:, D_MODEL:] * zb
    mixed = jnp.dot(merged.astype(BF16), wo_ref[...], preferred_element_type=F32)
    y = _layer_norm(DN_ALPHA * x_ref[...] + mixed, lnw_ref[...], lnb_ref[...])
    o_ref[...] = y
    ob_ref[...] = y.astype(BF16)


def _mix(ya, yb, gates, x, wa, wb, wo, lnw, lnb):
    d = D_MODEL
    row = lambda n: pl.BlockSpec((MIX_TM, n), lambda i: (i, 0))
    return pl.pallas_call(
        _mix_kernel,
        grid=(M_TOKENS // MIX_TM,),
        in_specs=[row(d), row(d), row(2 * d), row(d),
                  _resident((d, d)), _resident((d, d)), _resident((d, d)),
                  _resident((1, d)), _resident((1, d))],
        out_specs=[row(d), row(d)],
        out_shape=[jax.ShapeDtypeStruct((M_TOKENS, d), F32),
                   jax.ShapeDtypeStruct((M_TOKENS, d), BF16)],
        compiler_params=_params(("parallel",)),
        name="mix",
    )(ya, yb, gates, x, wa, wb, wo, lnw, lnb)


FFN_TM = 512
FFN_CHUNK = 1408


def _ffn_kernel(xb_ref, x_ref, wg_ref, wu_ref, wd_ref, lnw_ref, lnb_ref, o_ref):
    xb = xb_ref[...]
    y = jnp.zeros((FFN_TM, D_MODEL), F32)
    for c in range(D_FF // FFN_CHUNK):
        cs = slice(c * FFN_CHUNK, (c + 1) * FFN_CHUNK)
        hg = jnp.dot(xb, wg_ref[:, cs], preferred_element_type=F32)
        hu = jnp.dot(xb, wu_ref[:, cs], preferred_element_type=F32)
        act = (hg * _sigmoid(hg) * hu).astype(BF16)
        y = y + jnp.dot(act, wd_ref[cs, :], preferred_element_type=F32)
    o_ref[...] = _layer_norm(DN_ALPHA * x_ref[...] + y, lnw_ref[...], lnb_ref[...])


def _ffn(xb, x, wg, wu, wd, lnw, lnb):
    d = D_MODEL
    row = lambda n: pl.BlockSpec((FFN_TM, n), lambda i: (i, 0))
    return pl.pallas_call(
        _ffn_kernel,
        grid=(M_TOKENS // FFN_TM,),
        in_specs=[row(d), row(d), _resident((d, D_FF)), _resident((d, D_FF)),
                  _resident((D_FF, d)), _resident((1, d)), _resident((1, d))],
        out_specs=row(d),
        out_shape=jax.ShapeDtypeStruct((M_TOKENS, d), F32),
        compiler_params=_params(("parallel",)),
        name="ffn",
    )(xb, x, wg, wu, wd, lnw, lnb)


def _rope_tables():
    half = HEAD_DIM // 2
    inv_freq = ROPE_THETA ** (-jnp.arange(half, dtype=F32) / half)
    ang = jnp.arange(SEQ, dtype=F32)[:, None] * inv_freq[None, :]
    cos, sin = jnp.cos(ang), jnp.sin(ang)
    return jnp.concatenate([cos, cos], axis=1), jnp.concatenate([-sin, sin], axis=1)


def _layer(x2, w_in, lb_logits, hgrn_norm_w, w_branch_a, w_branch_b, b_gate, w_out,
           ln1_w, ln1_b, w_ffn_in, w_ffn_down, ln2_w, ln2_b):
    d = D_MODEL
    wseg = lambda a, b: w_in[:, a:b].astype(BF16)
    row2 = lambda v: v.reshape(1, -1).astype(F32)
    cos, sin = _rope_tables()

    w_qg = jnp.concatenate([w_in[:, 0:d], w_in[:, 3 * d:4 * d]], axis=1).astype(BF16)
    q_scaled, bias, k_rot, v_t, gates, y_a = _proj(
        x2, wseg(4 * d, 5 * d), wseg(5 * d, 6 * d), wseg(6 * d, 7 * d), wseg(7 * d, 9 * d),
        w_qg, wseg(d, 2 * d), wseg(2 * d, 3 * d), cos, sin, row2(b_gate), lb_logits.astype(F32),
        row2(hgrn_norm_w))
    y_b = _moba_attention(q_scaled, k_rot, v_t, bias)

    x1, x1b = _mix(y_a, y_b, gates, x2, w_branch_a.astype(BF16), w_branch_b.astype(BF16),
                   w_out.astype(BF16), row2(ln1_w), row2(ln1_b))
    return _ffn(x1b, x1, w_ffn_in[:, :D_FF].astype(BF16), w_ffn_in[:, D_FF:].astype(BF16),
                w_ffn_down.astype(BF16), row2(ln2_w), row2(ln2_b))


def kernel(x, w_in, lb_logits, hgrn_norm_w, w_branch_a, w_branch_b, b_gate, w_out, ln1_w, ln1_b,
           w_ffn_in, w_ffn_down, ln2_w, ln2_b):
    assert DEPTH == 1
    h = x.reshape(M_TOKENS, D_MODEL)
    h = _layer(h, w_in[0], lb_logits, hgrn_norm_w[0], w_branch_a[0], w_branch_b[0],
               b_gate[0], w_out[0], ln1_w[0], ln1_b[0], w_ffn_in[0], w_ffn_down[0],
               ln2_w[0], ln2_b[0])
    return h.reshape(BATCH, SEQ, D_MODEL)
```

```python
import numpy as np
import jax
import jax.numpy as jnp
from jax import lax
from jax.experimental import pallas as pl
from jax.experimental.pallas import tpu as pltpu

D_MODEL = 1024
BATCH = 4
SEQ = 4096
DEPTH = 1
HEADS = 8
HEAD_DIM = 128
HGRN_CHUNK = 64
HGRN_LEVELS = 6
MOBA_BLOCK = 256
MOBA_NBLK = SEQ // MOBA_BLOCK
MOBA_TOPK = 3
ROPE_THETA = 10000.0
D_FF = 2816
DN_ALPHA = (2.0 * DEPTH) ** 0.25
LN_EPS = 1e-5
RMS_EPS = 1e-6
M_TOKENS = BATCH * SEQ

MASK_VALUE = -1e30
VT_ROWS = HEAD_DIM + 16
LOG2_E = 1.4426950408889634
VMEM_LIMIT = 56 * 1024 * 1024

F32 = jnp.float32
BF16 = jnp.bfloat16
NT_DIMS = (((1,), (1,)), ((), ()))
TN_DIMS = (((0,), (0,)), ((), ()))


def _params(semantics):
    return pltpu.CompilerParams(dimension_semantics=semantics, vmem_limit_bytes=VMEM_LIMIT)


def _resident(shape):
    return pl.BlockSpec(shape, lambda *_: (0,) * len(shape), pipeline_mode=pl.Buffered(1))


def _sigmoid(z):
    return 1.0 / (1.0 + jnp.exp(-z))


def _layer_norm(r, w, b):
    mu = jnp.mean(r, axis=-1, keepdims=True)
    d = r - mu
    var = jnp.mean(d * d, axis=-1, keepdims=True)
    return d * lax.rsqrt(var + LN_EPS) * w + b


PROJ_SLAB = 256


def _proj_kernel(x_ref, wq_ref, wk_ref, wv_ref, wg_ref, wqg_ref, wf_ref, wi_ref, freq_ref,
                 bg_ref, lbl_ref, nw_ref, pm_ref, mask_ref,
                 q_ref, bias_ref, k_ref, vt_ref, gate_ref, ya_ref,
                 km_ref, state_ref, hq_ref, hog_ref, hlf_ref, hk_ref, hv_ref, xb_ref, qf_ref,
                 cos_all_ref, sin_all_ref):
    own = pl.program_id(1)
    lanes = HEADS * MOBA_NBLK

    @pl.when(own == 0)
    def _():
        km_ref[...] = jnp.zeros_like(km_ref)
        state_ref[...] = jnp.zeros_like(state_ref)

    xb_ref[...] = x_ref[...].astype(BF16)
    block_rows = pl.ds(pl.multiple_of(own * MOBA_BLOCK, MOBA_BLOCK), MOBA_BLOCK)

    @pl.when(pl.program_id(0) == 0)
    def _():
        position = own * MOBA_BLOCK + lax.broadcasted_iota(jnp.int32, (MOBA_BLOCK, HEAD_DIM), 0)
        angle = position.astype(F32) * freq_ref[...]
        first_half = lax.broadcasted_iota(jnp.int32, (MOBA_BLOCK, HEAD_DIM), 1) < HEAD_DIM // 2
        cos_all_ref[block_rows, :] = jnp.cos(angle)
        sin_all_ref[block_rows, :] = jnp.where(first_half, -jnp.sin(angle), jnp.sin(angle))

    cos_ref = cos_all_ref.at[block_rows, :]
    sin_ref = sin_all_ref.at[block_rows, :]
    head = lambda h: slice(h * HEAD_DIM, (h + 1) * HEAD_DIM)
    slab = lambda i: slice(i * PROJ_SLAB, (i + 1) * PROJ_SLAB)
    n_slabs = D_MODEL // PROJ_SLAB
    project = lambda w_ref, cols: jnp.dot(xb_ref[...], w_ref[:, cols], preferred_element_type=F32)
    rope = lambda t: t * cos_ref[...] + pltpu.roll(t, HEAD_DIM // 2, 1) * sin_ref[...]


    def hgrn_silu(i):
        def run():
            acc = project(wqg_ref, slab(i))
            dst = hq_ref if i < n_slabs else hog_ref
            dst[:, slab(i % n_slabs)] = (acc * _sigmoid(acc)).astype(dst.dtype)
        return run

    def hgrn_forget(i):
        def run():
            l0, l1 = lbl_ref[0:1, slab(i)], lbl_ref[1:2, slab(i)]
            top = jnp.maximum(l0, l1)
            e0, e1 = jnp.exp(l0 - top), jnp.exp(l1 - top)
            lb = e0 / (e0 + e1)
            z = project(wf_ref, slab(i))
            hlf_ref[:, slab(i)] = jnp.log(lb + (1.0 - lb) * _sigmoid(z)) * LOG2_E
            hk_ref[:, slab(i)] = ((1.0 - lb) * _sigmoid(-z)).astype(hk_ref.dtype)
        return run

    def hgrn_value(i):
        def run():
            hv_ref[:, slab(i)] = project(wi_ref, slab(i)).astype(hv_ref.dtype)
        return run

    heads_of = lambda i: range(i * PROJ_SLAB // HEAD_DIM, (i + 1) * PROJ_SLAB // HEAD_DIM)
    in_slab = lambda i, h: slice((h - heads_of(i)[0]) * HEAD_DIM, (h - heads_of(i)[0] + 1) * HEAD_DIM)

    def moba_query(i):
        def run():
            acc = project(wq_ref, slab(i))
            for h in heads_of(i):
                qh = rope(acc[:, in_slab(i, h)])
                qf_ref[:, head(h)] = qh
                q_ref[:, head(h)] = (qh * (HEAD_DIM ** -0.5 * LOG2_E)).astype(q_ref.dtype)
        return run

    def moba_choice():
        split = lambda t: (t.astype(BF16), (t - t.astype(BF16).astype(F32)).astype(BF16))
        q_hi, q_lo = split(qf_ref[...])
        km_hi, km_lo = split(km_ref[...])
        gate = lax.dot_general(jnp.concatenate([q_hi, q_lo, q_hi], axis=1),
                               jnp.concatenate([km_hi, km_hi, km_lo], axis=1), NT_DIMS,
                               preferred_element_type=F32)
        lane = lax.broadcasted_iota(jnp.int32, (MOBA_BLOCK, lanes), 1)
        eligible = lane < own * HEADS
        gate = jnp.where(eligible, gate, -jnp.inf)
        rank = jnp.zeros((MOBA_BLOCK, lanes), F32)
        for r in range(1, MOBA_NBLK):
            partner = pltpu.roll(gate, lanes - r * HEADS, 1)
            wrapped = lane >= lanes - r * HEADS
            beats = (partner > gate) | (wrapped & (partner == gate))
            rank = rank + beats.astype(F32)
        selected = eligible & (rank < float(MOBA_TOPK))
        bias_ref[0] = jnp.where(selected, 0.0, MASK_VALUE).T

    def moba_key(i):
        def run():
            acc = project(wk_ref, slab(i))
            for h in heads_of(i):
                rot = rope(acc[:, in_slab(i, h)])
                k_ref[:, head(h)] = rot.astype(k_ref.dtype)
                mean = jnp.mean(rot, axis=0, keepdims=True)
                zeros = lambda n: [jnp.zeros((1, n * HEAD_DIM), F32)] if n else []
                km_ref[pl.ds(own * HEADS + h, 1), :] = jnp.concatenate(
                    zeros(h) + [mean] + zeros(HEADS - 1 - h), axis=1)
        return run

    def moba_value(i):
        def run():
            vt = project(wv_ref, slab(i)).T.astype(vt_ref.dtype)
            pad_row = lax.broadcasted_iota(jnp.int32, (VT_ROWS - HEAD_DIM, MOBA_BLOCK), 0)
            ones_row = jnp.where(pad_row == 0, 1.0, 0.0).astype(vt_ref.dtype)
            for h in heads_of(i):
                vt_ref[0, 0, h * VT_ROWS:h * VT_ROWS + HEAD_DIM, :] = vt[in_slab(i, h), :]
                vt_ref[0, 0, h * VT_ROWS + HEAD_DIM:(h + 1) * VT_ROWS, :] = ones_row
        return run

    def merge_gate(i):
        def run():
            acc = project(wg_ref, slab(i)) + bg_ref[:, slab(i)]
            gate_ref[:, slab(i)] = _sigmoid(acc).astype(gate_ref.dtype)
        return run

    hgrn_inputs = ([hgrn_silu(i) for i in range(2 * n_slabs)]
                   + [hgrn_forget(i) for i in range(n_slabs)]
                   + [hgrn_value(i) for i in range(n_slabs)])
    queries = [moba_query(i) for i in range(n_slabs)] + [moba_choice]
    for i, piece in enumerate(hgrn_inputs):
        piece()
        if i < len(queries):
            queries[i]()

    rest = ([moba_key(i) for i in range(n_slabs)] + [moba_value(i) for i in range(n_slabs)]
            + [merge_gate(i) for i in range(2 * n_slabs)])
    calls = []

    def fill():
        calls.append(None)
        if rest and len(calls) % 2 == 0:
            rest.pop(0)()

    _hgrn_tile(hq_ref, hlf_ref, hk_ref, hv_ref, hog_ref, nw_ref, pm_ref, mask_ref, ya_ref,
               state_ref, fill)
    while rest:
        fill()


def _proj(x, w_q, w_k, w_v, w_g, w_qg, w_f, w_i, rope_freq, b_gate, lb_logits, norm_w):
    d = D_MODEL
    lanes = HEADS * MOBA_NBLK
    pm, masks = _hgrn_constants()
    pm = np.tile(pm, (1, 3))
    row = lambda n: pl.BlockSpec((MOBA_BLOCK, n), lambda b, i: (b * MOBA_NBLK + i, 0))
    shape = lambda n, dt: jax.ShapeDtypeStruct((M_TOKENS, n), dt)
    tile = lambda dt: pltpu.VMEM((MOBA_BLOCK, d), dt)
    return pl.pallas_call(
        _proj_kernel,
        grid=(BATCH, MOBA_NBLK),
        in_specs=[row(d), _resident((d, d)), _resident((d, d)), _resident((d, d)),
                  _resident((d, 2 * d)), _resident((d, 2 * d)), _resident((d, d)), _resident((d, d)),
                  _resident((1, HEAD_DIM)), _resident((1, 2 * d)), _resident((DEPTH + 1, d)),
                  _resident((1, d)), _resident(pm.shape), _resident(masks.shape)],
        out_specs=[row(d),
                   pl.BlockSpec((1, lanes, MOBA_BLOCK), lambda b, i: (b, 0, i)),
                   row(d),
                   pl.BlockSpec((1, 1, HEADS * VT_ROWS, MOBA_BLOCK), lambda b, i: (b, i, 0, 0)),
                   row(2 * d), row(d)],
        out_shape=[shape(d, BF16),
                   jax.ShapeDtypeStruct((BATCH, lanes, SEQ), F32),
                   shape(d, BF16),
                   jax.ShapeDtypeStruct((BATCH, MOBA_NBLK, HEADS * VT_ROWS, MOBA_BLOCK), BF16),
                   shape(2 * d, BF16), shape(d, BF16)],
        scratch_shapes=[pltpu.VMEM((lanes, d), F32),
                        pltpu.VMEM((HEADS, HEAD_DIM, HEAD_DIM), F32),
                        tile(BF16), tile(BF16), tile(F32), tile(BF16), tile(BF16),
                        tile(BF16), tile(F32),
                        pltpu.VMEM((SEQ, HEAD_DIM), F32), pltpu.VMEM((SEQ, HEAD_DIM), F32)],
        compiler_params=_params(("arbitrary", "arbitrary")),
        name="proj",
    )(x, w_q, w_k, w_v, w_g, w_qg, w_f, w_i, rope_freq, b_gate, lb_logits, norm_w,
      jnp.asarray(pm, BF16), jnp.asarray(masks, BF16))


def _moba_att_kernel(q_ref, k_ref, vt_ref, bias_ref, o_ref, acc_ref, s_ref):
    own = pl.program_id(1)

    def head(h):
        return slice(h * HEAD_DIM, (h + 1) * HEAD_DIM)

    def scores(h, j):
        kj = k_ref[pl.ds(pl.multiple_of(j * MOBA_BLOCK, MOBA_BLOCK), MOBA_BLOCK), head(h)]
        return lax.dot_general(kj, q_ref[:, head(h)], NT_DIMS, preferred_element_type=F32)

    def item(h, j, s, m, l, next_scores, bias=None):
        top = jnp.max(s, axis=0, keepdims=True)
        m_new = jnp.maximum(m, top if bias is None else top + bias)
        a = jnp.exp2(m - m_new)
        p = jnp.exp2(s - (m_new if bias is None else m_new - 2.0 * bias))
        if next_scores is not None:
            s_ref[h] = next_scores()
        pv = jnp.dot(vt_ref[0, j, h * VT_ROWS:(h + 1) * VT_ROWS, :], p.astype(BF16),
                     preferred_element_type=F32)
        acc_ref[h] = a * acc_ref[h] + pv[:HEAD_DIM]
        return m_new, a * l + pv[HEAD_DIM:HEAD_DIM + 1]

    acc_ref[...] = jnp.zeros_like(acc_ref)
    for h in range(HEADS):
        s_ref[h] = scores(h, 0)
    m0 = jnp.full((1, MOBA_BLOCK), MASK_VALUE, F32)
    l0 = jnp.zeros((1, MOBA_BLOCK), F32)

    def body(j, carry):
        ms, ls = list(carry[0]), list(carry[1])
        for h in range(HEADS):
            ms[h], ls[h] = item(h, j, s_ref[h], ms[h], ls[h], lambda h=h: scores(h, j + 1),
                                bias_ref[0, pl.ds(j * HEADS + h, 1), :])
        return tuple(ms), tuple(ls)

    ms, ls = lax.fori_loop(0, own, body, ((m0,) * HEADS, (l0,) * HEADS))

    key_pos = lax.broadcasted_iota(jnp.int32, (MOBA_BLOCK, MOBA_BLOCK), 0)
    qry_pos = lax.broadcasted_iota(jnp.int32, (MOBA_BLOCK, MOBA_BLOCK), 1)
    causal = key_pos <= qry_pos
    for h in range(HEADS):
        s = jnp.where(causal, s_ref[h], MASK_VALUE)
        _, l = item(h, own, s, ms[h], ls[h], None)
        o_ref[:, head(h)] = (acc_ref[h] * (1.0 / l)).T.astype(o_ref.dtype)


def _moba_attention(q_scaled, k_rot, v_t, bias):
    lanes = HEADS * MOBA_NBLK
    return pl.pallas_call(
        _moba_att_kernel,
        grid=(BATCH, MOBA_NBLK),
        in_specs=[pl.BlockSpec((MOBA_BLOCK, D_MODEL), lambda b, i: (b * MOBA_NBLK + i, 0)),
                  pl.BlockSpec((SEQ, D_MODEL), lambda b, i: (b, 0)),
                  pl.BlockSpec((1, MOBA_NBLK, HEADS * VT_ROWS, MOBA_BLOCK),
                               lambda b, i: (b, 0, 0, 0)),
                  pl.BlockSpec((1, lanes, MOBA_BLOCK), lambda b, i: (b, 0, i))],
        out_specs=pl.BlockSpec((MOBA_BLOCK, D_MODEL), lambda b, i: (b * MOBA_NBLK + i, 0)),
        out_shape=jax.ShapeDtypeStruct((M_TOKENS, D_MODEL), BF16),
        scratch_shapes=[pltpu.VMEM((HEADS, HEAD_DIM, MOBA_BLOCK), F32),
                        pltpu.VMEM((HEADS, MOBA_BLOCK, MOBA_BLOCK), F32)],
        compiler_params=_params(("parallel", "arbitrary")),
        name="moba_att",
    )(q_scaled, k_rot, v_t, bias)


HGRN_TILE = MOBA_BLOCK
PM_LEVELS = (1, 2)


def _hgrn_constants():
    c = HGRN_CHUNK
    idx = np.arange(c)
    tri = (idx[None, :] <= idx[:, None]).astype(np.float32)
    rows = [tri]
    masks = []
    for lvl in range(HGRN_LEVELS):
        half = 1 << lvl
        group = idx // (2 * half)
        upper = (idx & half) != 0
        if lvl in PM_LEVELS:
            boundary = tri[group * (2 * half) + half - 1]
            rows.append(np.where(upper[:, None], tri - boundary, boundary - tri))
        masks.append(((group[:, None] == group[None, :]) & upper[:, None] & ~upper[None, :])
                     .astype(np.float32))
    masks.append(np.eye(c, dtype=np.float32))
    return np.concatenate(rows, axis=0), np.stack(masks)


def _hgrn_tile(q_ref, lf_ref, k_ref, v_ref, og_ref, nw_ref, pm_ref, mask_ref, o_ref, state_ref,
               fill):
    c = HGRN_CHUNK
    n_ch = HGRN_TILE // c
    units = [(ch, h) for ch in range(n_ch) for h in range(HEADS)]
    rows = lambda ch: slice(ch * c, (ch + 1) * c)
    lanes = lambda h: slice(h * HEAD_DIM, (h + 1) * HEAD_DIM)
    chunks_on_lanes = lambda t: jnp.concatenate([t[rows(ch), :] for ch in range(n_ch)], axis=1)

    lf = lf_ref[...]
    hi = lf.astype(BF16)
    rem = lf - hi.astype(F32)
    mid = rem.astype(BF16)
    lo = (rem - mid.astype(F32)).astype(BF16)
    split = jnp.concatenate([chunks_on_lanes(hi), chunks_on_lanes(mid), chunks_on_lanes(lo)], axis=0)
    sums = jnp.dot(pm_ref[...], split, preferred_element_type=F32)

    row = lax.broadcasted_iota(jnp.int32, (c, HEAD_DIM), 0)
    upper = [(row & (1 << lvl)) != 0 for lvl in range(HGRN_LEVELS)]

    def decayed_operands(u, ch, h):
        col = slice(u * HEAD_DIM, (u + 1) * HEAD_DIM)
        g = sums[0:c, col]
        q_bf, k_bf = q_ref[rows(ch), lanes(h)], k_ref[rows(ch), lanes(h)]
        qf, kk = q_bf.astype(F32), k_bf.astype(F32)
        zs = []
        for lvl in range(HGRN_LEVELS):
            half = 1 << lvl
            if lvl == 0:
                e = jnp.where(upper[0], lf_ref[rows(ch), lanes(h)], 0.0)
            elif lvl in PM_LEVELS:
                i = PM_LEVELS.index(lvl)
                e = sums[(i + 1) * c:(i + 2) * c, col]
            else:
                boundary = jnp.concatenate(
                    [jnp.broadcast_to(g[b + half - 1:b + half, :], (2 * half, HEAD_DIM))
                     for b in range(0, c, 2 * half)], axis=0)
                e = -jnp.abs(g - boundary)
            zs.append((jnp.where(upper[lvl], qf, kk) * jnp.exp2(e)).astype(BF16))
        g_last = g[c - 1:c, :]
        return dict(q=q_bf, k=k_bf, zs=zs,
                    q_in=(qf * jnp.exp2(g)).astype(BF16),
                    k_out=(kk * jnp.exp2(g_last - g)).astype(BF16),
                    decay=jnp.exp2(g_last),
                    v=v_ref[rows(ch), lanes(h)])

    def state_free_matmuls(op):
        score = lambda lhs, rhs: lax.dot_general(lhs, rhs, NT_DIMS,
                                                 preferred_element_type=F32).astype(BF16)
        a = score(op["q"], op["k"]) * mask_ref[HGRN_LEVELS]
        for lvl, z in enumerate(op["zs"]):
            a = a + score(z, z) * mask_ref[lvl]
        op["o"] = jnp.dot(a, op["v"], preferred_element_type=F32)
        op["update"] = lax.dot_general(op["v"], op["k_out"], TN_DIMS, preferred_element_type=F32)

    ops = {}
    for u, (ch, h) in enumerate(units):
        ops[ch, h] = decayed_operands(u, ch, h)
        if u >= 1:
            state_free_matmuls(ops[units[u - 1]])
        fill()
    state_free_matmuls(ops[units[-1]])

    for h in range(HEADS):
        state_t = state_ref[h]
        for ch in range(n_ch):
            op = ops[ch, h]
            op["state"] = state_t.astype(BF16)
            state_t = state_t * op["decay"] + op["update"]
        state_ref[h] = state_t
    for ch, h in units:
        op = ops[ch, h]
        o = op["o"] + lax.dot_general(op["q_in"], op["state"], NT_DIMS, preferred_element_type=F32)
        ms = jnp.mean(o * o, axis=-1, keepdims=True)
        y = o * lax.rsqrt(ms + RMS_EPS) * nw_ref[:, lanes(h)] * og_ref[rows(ch), lanes(h)]
        o_ref[rows(ch), lanes(h)] = y.astype(o_ref.dtype)


MIX_TM = 512


def _mix_kernel(ya_ref, yb_ref, gate_ref, x_ref, wa_ref, wb_ref, wo_ref, lnw_ref, lnb_ref,
                o_ref, ob_ref):
    half = MIX_TM // 2
    first, second = slice(0, half), slice(half, MIX_TM)

    def merged(rows, za):
        zb = jnp.dot(yb_ref[rows, :], wb_ref[...], preferred_element_type=F32)
        return (gate_ref[rows, :D_MODEL] * za + gate_ref[rows, D_MODEL:] * zb).astype(BF16)

    def pre_norm(rows, za):
        mixed = jnp.dot(merged(rows, za), wo_ref[...], preferred_element_type=F32)
        return DN_ALPHA * x_ref[rows, :] + mixed

    def finish(rows, r):
        y = _layer_norm(r, lnw_ref[...], lnb_ref[...])
        o_ref[rows, :] = y
        ob_ref[rows, :] = y.astype(BF16)

    branch_a = lambda rows: jnp.dot(ya_ref[rows, :], wa_ref[...], preferred_element_type=F32)
    r_first = pre_norm(first, branch_a(first))
    za_second = branch_a(second)
    finish(first, r_first)
    finish(second, pre_norm(second, za_second))


def _mix(ya, yb, gates, x, wa, wb, wo, lnw, lnb):
    d = D_MODEL
    row = lambda n: pl.BlockSpec((MIX_TM, n), lambda i: (i, 0))
    return pl.pallas_call(
        _mix_kernel,
        grid=(M_TOKENS // MIX_TM,),
        in_specs=[row(d), row(d), row(2 * d), row(d),
                  _resident((d, d)), _resident((d, d)), _resident((d, d)),
                  _resident((1, d)), _resident((1, d))],
        out_specs=[row(d), row(d)],
        out_shape=[jax.ShapeDtypeStruct((M_TOKENS, d), F32),
                   jax.ShapeDtypeStruct((M_TOKENS, d), BF16)],
        compiler_params=_params(("parallel",)),
        name="mix",
    )(ya, yb, gates, x, wa, wb, wo, lnw, lnb)


FFN_TM = 512
FFN_CHUNK = 1408


def _ffn_kernel(xb_ref, x_ref, wg_ref, wu_ref, wd_ref, lnw_ref, lnb_ref, o_ref):
    half = FFN_TM // 2
    n_chunks = D_FF // FFN_CHUNK

    def chunk(rows, c):
        cs = slice(c * FFN_CHUNK, (c + 1) * FFN_CHUNK)
        xb = xb_ref[rows, :]
        hg = jnp.dot(xb, wg_ref[:, cs], preferred_element_type=F32)
        hu = jnp.dot(xb, wu_ref[:, cs], preferred_element_type=F32)
        act = (hg * _sigmoid(hg) * hu).astype(BF16)
        return jnp.dot(act, wd_ref[cs, :], preferred_element_type=F32)

    def finish(rows, y):
        o_ref[rows, :] = _layer_norm(DN_ALPHA * x_ref[rows, :] + y, lnw_ref[...], lnb_ref[...])

    first, second = slice(0, half), slice(half, FFN_TM)
    y_first = sum(chunk(first, c) for c in range(n_chunks))
    y_second = chunk(second, 0)
    finish(first, y_first)
    for c in range(1, n_chunks):
        y_second = y_second + chunk(second, c)
    finish(second, y_second)


def _ffn(xb, x, wg, wu, wd, lnw, lnb):
    d = D_MODEL
    row = lambda n: pl.BlockSpec((FFN_TM, n), lambda i: (i, 0))
    return pl.pallas_call(
        _ffn_kernel,
        grid=(M_TOKENS // FFN_TM,),
        in_specs=[row(d), row(d), _resident((d, D_FF)), _resident((d, D_FF)),
                  _resident((D_FF, d)), _resident((1, d)), _resident((1, d))],
        out_specs=row(d),
        out_shape=jax.ShapeDtypeStruct((M_TOKENS, d), F32),
        compiler_params=_params(("parallel",)),
        name="ffn",
    )(xb, x, wg, wu, wd, lnw, lnb)


def _rope_frequencies():
    half = HEAD_DIM // 2
    inv_freq = ROPE_THETA ** (-jnp.arange(half, dtype=F32) / half)
    return jnp.concatenate([inv_freq, inv_freq]).reshape(1, HEAD_DIM)


def _layer(x2, w_in, lb_logits, hgrn_norm_w, w_branch_a, w_branch_b, b_gate, w_out,
           ln1_w, ln1_b, w_ffn_in, w_ffn_down, ln2_w, ln2_b):
    d = D_MODEL
    wseg = lambda a, b: w_in[:, a:b].astype(BF16)
    row2 = lambda v: v.reshape(1, -1).astype(F32)

    w_qg = jnp.concatenate([w_in[:, 0:d], w_in[:, 3 * d:4 * d]], axis=1).astype(BF16)
    q_scaled, bias, k_rot, v_t, gates, y_a = _proj(
        x2, wseg(4 * d, 5 * d), wseg(5 * d, 6 * d), wseg(6 * d, 7 * d), wseg(7 * d, 9 * d),
        w_qg, wseg(d, 2 * d), wseg(2 * d, 3 * d), _rope_frequencies(), row2(b_gate),
        lb_logits.astype(F32),
        row2(hgrn_norm_w))
    y_b = _moba_attention(q_scaled, k_rot, v_t, bias)

    x1, x1b = _mix(y_a, y_b, gates, x2, w_branch_a.astype(BF16), w_branch_b.astype(BF16),
                   w_out.astype(BF16), row2(ln1_w), row2(ln1_b))
    return _ffn(x1b, x1, w_ffn_in[:, :D_FF].astype(BF16), w_ffn_in[:, D_FF:].astype(BF16),
                w_ffn_down.astype(BF16), row2(ln2_w), row2(ln2_b))


def kernel(x, w_in, lb_logits, hgrn_norm_w, w_branch_a, w_branch_b, b_gate, w_out, ln1_w, ln1_b,
           w_ffn_in, w_ffn_down, ln2_w, ln2_b):
    assert DEPTH == 1
    h = x.reshape(M_TOKENS, D_MODEL)
    h = _layer(h, w_in[0], lb_logits, hgrn_norm_w[0], w_branch_a[0], w_branch_b[0],
               b_gate[0], w_out[0], ln1_w[0], ln1_b[0], w_ffn_in[0], w_ffn_down[0],
               ln2_w[0], ln2_b[0])
    return h.reshape(BATCH, SEQ, D_MODEL)
```

```python
import numpy as np
import jax
import jax.numpy as jnp
from jax import lax
from jax.experimental import pallas as pl
from jax.experimental.pallas import tpu as pltpu

D_MODEL = 1024
BATCH = 4
SEQ = 4096
DEPTH = 1
HEADS = 8
HEAD_DIM = 128
HGRN_CHUNK = 64
HGRN_LEVELS = 6
MOBA_BLOCK = 256
MOBA_NBLK = SEQ // MOBA_BLOCK
MOBA_TOPK = 3
ROPE_THETA = 10000.0
D_FF = 2816
DN_ALPHA = (2.0 * DEPTH) ** 0.25
LN_EPS = 1e-5
RMS_EPS = 1e-6
M_TOKENS = BATCH * SEQ

MASK_VALUE = -1e30
VT_ROWS = HEAD_DIM + 16
LOG2_E = 1.4426950408889634
VMEM_LIMIT = 56 * 1024 * 1024

F32 = jnp.float32
BF16 = jnp.bfloat16
NT_DIMS = (((1,), (1,)), ((), ()))
TN_DIMS = (((0,), (0,)), ((), ()))


def _params(semantics):
    return pltpu.CompilerParams(dimension_semantics=semantics, vmem_limit_bytes=VMEM_LIMIT)


def _resident(shape):
    return pl.BlockSpec(shape, lambda *_: (0,) * len(shape), pipeline_mode=pl.Buffered(1))


def _sigmoid(z):
    return 1.0 / (1.0 + jnp.exp(-z))


def _layer_norm(r, w, b):
    mu = jnp.mean(r, axis=-1, keepdims=True)
    d = r - mu
    var = jnp.mean(d * d, axis=-1, keepdims=True)
    return d * lax.rsqrt(var + LN_EPS) * w + b


PROJ_SLAB = 256
N_PROJ = 9 * D_MODEL
COL_HGRN_Q, COL_HGRN_F, COL_HGRN_V, COL_HGRN_GATE = 0, D_MODEL, 2 * D_MODEL, 3 * D_MODEL
COL_MOBA_Q, COL_MOBA_K, COL_MOBA_V, COL_MERGE_GATE = 4 * D_MODEL, 5 * D_MODEL, 6 * D_MODEL, 7 * D_MODEL


def _proj_kernel(x_ref, w_ref, freq_ref,
                 bg_ref, lbl_ref, nw_ref, pm_ref, mask_ref,
                 q_ref, bias_ref, k_ref, vt_ref, gate_ref, ya_ref,
                 km_ref, state_ref, hq_ref, hog_ref, hlf_ref, hk_ref, hv_ref, xb_ref, qf_ref,
                 cos_all_ref, sin_all_ref):
    own = pl.program_id(1)
    lanes = HEADS * MOBA_NBLK

    @pl.when(own == 0)
    def _():
        km_ref[...] = jnp.zeros_like(km_ref)
        state_ref[...] = jnp.zeros_like(state_ref)

    xb_ref[...] = x_ref[...].astype(BF16)
    block_rows = pl.ds(pl.multiple_of(own * MOBA_BLOCK, MOBA_BLOCK), MOBA_BLOCK)

    @pl.when(pl.program_id(0) == 0)
    def _():
        position = own * MOBA_BLOCK + lax.broadcasted_iota(jnp.int32, (MOBA_BLOCK, HEAD_DIM), 0)
        angle = position.astype(F32) * freq_ref[...]
        first_half = lax.broadcasted_iota(jnp.int32, (MOBA_BLOCK, HEAD_DIM), 1) < HEAD_DIM // 2
        cos_all_ref[block_rows, :] = jnp.cos(angle)
        sin_all_ref[block_rows, :] = jnp.where(first_half, -jnp.sin(angle), jnp.sin(angle))

    cos_ref = cos_all_ref.at[block_rows, :]
    sin_ref = sin_all_ref.at[block_rows, :]
    head = lambda h: slice(h * HEAD_DIM, (h + 1) * HEAD_DIM)
    slab = lambda i: slice(i * PROJ_SLAB, (i + 1) * PROJ_SLAB)
    n_slabs = D_MODEL // PROJ_SLAB

    def project(first_col, i):
        cols = slice(first_col + i * PROJ_SLAB, first_col + (i + 1) * PROJ_SLAB)
        return jnp.dot(xb_ref[...], w_ref[:, cols], preferred_element_type=F32)

    rope = lambda t: t * cos_ref[...] + pltpu.roll(t, HEAD_DIM // 2, 1) * sin_ref[...]


    def hgrn_silu(i):
        def run():
            acc = project(COL_HGRN_Q if i < n_slabs else COL_HGRN_GATE, i % n_slabs)
            dst = hq_ref if i < n_slabs else hog_ref
            dst[:, slab(i % n_slabs)] = (acc * _sigmoid(acc)).astype(dst.dtype)
        return run

    def hgrn_forget(i):
        def run():
            l0, l1 = lbl_ref[0:1, slab(i)], lbl_ref[1:2, slab(i)]
            top = jnp.maximum(l0, l1)
            e0, e1 = jnp.exp(l0 - top), jnp.exp(l1 - top)
            lb = e0 / (e0 + e1)
            z = project(COL_HGRN_F, i)
            hlf_ref[:, slab(i)] = jnp.log(lb + (1.0 - lb) * _sigmoid(z)) * LOG2_E
            hk_ref[:, slab(i)] = ((1.0 - lb) * _sigmoid(-z)).astype(hk_ref.dtype)
        return run

    def hgrn_value(i):
        def run():
            hv_ref[:, slab(i)] = project(COL_HGRN_V, i).astype(hv_ref.dtype)
        return run

    heads_of = lambda i: range(i * PROJ_SLAB // HEAD_DIM, (i + 1) * PROJ_SLAB // HEAD_DIM)
    in_slab = lambda i, h: slice((h - heads_of(i)[0]) * HEAD_DIM, (h - heads_of(i)[0] + 1) * HEAD_DIM)

    def moba_query(i):
        def run():
            acc = project(COL_MOBA_Q, i)
            for h in heads_of(i):
                qh = rope(acc[:, in_slab(i, h)])
                qf_ref[:, head(h)] = qh
                q_ref[:, head(h)] = (qh * (HEAD_DIM ** -0.5 * LOG2_E)).astype(q_ref.dtype)
        return run

    def moba_choice():
        split = lambda t: (t.astype(BF16), (t - t.astype(BF16).astype(F32)).astype(BF16))
        q_hi, q_lo = split(qf_ref[...])
        km_hi, km_lo = split(km_ref[...])
        gate = lax.dot_general(jnp.concatenate([q_hi, q_lo, q_hi], axis=1),
                               jnp.concatenate([km_hi, km_hi, km_lo], axis=1), NT_DIMS,
                               preferred_element_type=F32)
        lane = lax.broadcasted_iota(jnp.int32, (MOBA_BLOCK, lanes), 1)
        eligible = lane < own * HEADS
        gate = jnp.where(eligible, gate, -jnp.inf)
        rank = jnp.zeros((MOBA_BLOCK, lanes), F32)
        for r in range(1, MOBA_NBLK):
            partner = pltpu.roll(gate, lanes - r * HEADS, 1)
            wrapped = lane >= lanes - r * HEADS
            beats = (partner > gate) | (wrapped & (partner == gate))
            rank = rank + beats.astype(F32)
        selected = eligible & (rank < float(MOBA_TOPK))
        bias_ref[0] = jnp.where(selected, 0.0, MASK_VALUE).T

    def moba_key(i):
        def run():
            acc = project(COL_MOBA_K, i)
            for h in heads_of(i):
                rot = rope(acc[:, in_slab(i, h)])
                k_ref[:, head(h)] = rot.astype(k_ref.dtype)
                mean = jnp.mean(rot, axis=0, keepdims=True)
                zeros = lambda n: [jnp.zeros((1, n * HEAD_DIM), F32)] if n else []
                km_ref[pl.ds(own * HEADS + h, 1), :] = jnp.concatenate(
                    zeros(h) + [mean] + zeros(HEADS - 1 - h), axis=1)
        return run

    def moba_value(i):
        def run():
            vt = project(COL_MOBA_V, i).T.astype(vt_ref.dtype)
            pad_row = lax.broadcasted_iota(jnp.int32, (VT_ROWS - HEAD_DIM, MOBA_BLOCK), 0)
            ones_row = jnp.where(pad_row == 0, 1.0, 0.0).astype(vt_ref.dtype)
            for h in heads_of(i):
                vt_ref[0, 0, h * VT_ROWS:h * VT_ROWS + HEAD_DIM, :] = vt[in_slab(i, h), :]
                vt_ref[0, 0, h * VT_ROWS + HEAD_DIM:(h + 1) * VT_ROWS, :] = ones_row
        return run

    def merge_gate(i):
        def run():
            acc = project(COL_MERGE_GATE, i) + bg_ref[:, slab(i)]
            gate_ref[:, slab(i)] = _sigmoid(acc).astype(gate_ref.dtype)
        return run

    hgrn_inputs = ([hgrn_silu(i) for i in range(2 * n_slabs)]
                   + [hgrn_forget(i) for i in range(n_slabs)]
                   + [hgrn_value(i) for i in range(n_slabs)])
    queries = [moba_query(i) for i in range(n_slabs)] + [moba_choice]
    for i, piece in enumerate(hgrn_inputs):
        piece()
        if i < len(queries):
            queries[i]()

    rest = ([moba_key(i) for i in range(n_slabs)] + [moba_value(i) for i in range(n_slabs)]
            + [merge_gate(i) for i in range(2 * n_slabs)])
    calls = []

    def fill():
        calls.append(None)
        if rest and len(calls) % 2 == 0:
            rest.pop(0)()

    _hgrn_tile(hq_ref, hlf_ref, hk_ref, hv_ref, hog_ref, nw_ref, pm_ref, mask_ref, ya_ref,
               state_ref, fill)
    while rest:
        fill()


def _proj(x, w_in, rope_freq, b_gate, lb_logits, norm_w):
    d = D_MODEL
    lanes = HEADS * MOBA_NBLK
    pm, masks = _hgrn_constants()
    pm = np.tile(pm, (1, 3))
    row = lambda n: pl.BlockSpec((MOBA_BLOCK, n), lambda b, i: (b * MOBA_NBLK + i, 0))
    shape = lambda n, dt: jax.ShapeDtypeStruct((M_TOKENS, n), dt)
    tile = lambda dt: pltpu.VMEM((MOBA_BLOCK, d), dt)
    return pl.pallas_call(
        _proj_kernel,
        grid=(BATCH, MOBA_NBLK),
        in_specs=[row(d), _resident((d, N_PROJ)), _resident((1, HEAD_DIM)), _resident((1, 2 * d)), _resident((DEPTH + 1, d)),
                  _resident((1, d)), _resident(pm.shape), _resident(masks.shape)],
        out_specs=[row(d),
                   pl.BlockSpec((1, lanes, MOBA_BLOCK), lambda b, i: (b, 0, i)),
                   row(d),
                   pl.BlockSpec((1, 1, HEADS * VT_ROWS, MOBA_BLOCK), lambda b, i: (b, i, 0, 0)),
                   row(2 * d), row(d)],
        out_shape=[shape(d, BF16),
                   jax.ShapeDtypeStruct((BATCH, lanes, SEQ), F32),
                   shape(d, BF16),
                   jax.ShapeDtypeStruct((BATCH, MOBA_NBLK, HEADS * VT_ROWS, MOBA_BLOCK), BF16),
                   shape(2 * d, BF16), shape(d, BF16)],
        scratch_shapes=[pltpu.VMEM((lanes, d), F32),
                        pltpu.VMEM((HEADS, HEAD_DIM, HEAD_DIM), F32),
                        tile(BF16), tile(BF16), tile(F32), tile(BF16), tile(BF16),
                        tile(BF16), tile(F32),
                        pltpu.VMEM((SEQ, HEAD_DIM), F32), pltpu.VMEM((SEQ, HEAD_DIM), F32)],
        compiler_params=_params(("arbitrary", "arbitrary")),
        name="proj",
    )(x, w_in, rope_freq, b_gate, lb_logits, norm_w,
      jnp.asarray(pm, BF16), jnp.asarray(masks, BF16))


def _moba_att_kernel(q_ref, k_ref, vt_ref, bias_ref, o_ref, acc_ref, s_ref):
    own = pl.program_id(1)

    def head(h):
        return slice(h * HEAD_DIM, (h + 1) * HEAD_DIM)

    def scores(h, j):
        kj = k_ref[pl.ds(pl.multiple_of(j * MOBA_BLOCK, MOBA_BLOCK), MOBA_BLOCK), head(h)]
        return lax.dot_general(kj, q_ref[:, head(h)], NT_DIMS, preferred_element_type=F32)

    def item(h, j, s, m, l, next_scores, bias=None):
        top = jnp.max(s, axis=0, keepdims=True)
        m_new = jnp.maximum(m, top if bias is None else top + bias)
        a = jnp.exp2(m - m_new)
        p = jnp.exp2(s - (m_new if bias is None else m_new - 2.0 * bias))
        if next_scores is not None:
            s_ref[h] = next_scores()
        pv = jnp.dot(vt_ref[0, j, h * VT_ROWS:(h + 1) * VT_ROWS, :], p.astype(BF16),
                     preferred_element_type=F32)
        acc_ref[h] = a * acc_ref[h] + pv[:HEAD_DIM]
        return m_new, a * l + pv[HEAD_DIM:HEAD_DIM + 1]

    acc_ref[...] = jnp.zeros_like(acc_ref)
    for h in range(HEADS):
        s_ref[h] = scores(h, 0)
    m0 = jnp.full((1, MOBA_BLOCK), MASK_VALUE, F32)
    l0 = jnp.zeros((1, MOBA_BLOCK), F32)

    def body(j, carry):
        ms, ls = list(carry[0]), list(carry[1])
        for h in range(HEADS):
            ms[h], ls[h] = item(h, j, s_ref[h], ms[h], ls[h], lambda h=h: scores(h, j + 1),
                                bias_ref[0, pl.ds(j * HEADS + h, 1), :])
        return tuple(ms), tuple(ls)

    pairs = own // 2
    carry = lax.fori_loop(0, pairs, lambda t, c: body(2 * t + 1, body(2 * t, c)),
                          ((m0,) * HEADS, (l0,) * HEADS))
    ms, ls = lax.fori_loop(2 * pairs, own, body, carry)

    key_pos = lax.broadcasted_iota(jnp.int32, (MOBA_BLOCK, MOBA_BLOCK), 0)
    qry_pos = lax.broadcasted_iota(jnp.int32, (MOBA_BLOCK, MOBA_BLOCK), 1)
    causal = key_pos <= qry_pos
    for h in range(HEADS):
        s = jnp.where(causal, s_ref[h], MASK_VALUE)
        _, l = item(h, own, s, ms[h], ls[h], None)
        o_ref[:, head(h)] = (acc_ref[h] * (1.0 / l)).T.astype(o_ref.dtype)


def _moba_attention(q_scaled, k_rot, v_t, bias):
    lanes = HEADS * MOBA_NBLK
    return pl.pallas_call(
        _moba_att_kernel,
        grid=(BATCH, MOBA_NBLK),
        in_specs=[pl.BlockSpec((MOBA_BLOCK, D_MODEL), lambda b, i: (b * MOBA_NBLK + i, 0)),
                  pl.BlockSpec((SEQ, D_MODEL), lambda b, i: (b, 0)),
                  pl.BlockSpec((1, MOBA_NBLK, HEADS * VT_ROWS, MOBA_BLOCK),
                               lambda b, i: (b, 0, 0, 0)),
                  pl.BlockSpec((1, lanes, MOBA_BLOCK), lambda b, i: (b, 0, i))],
        out_specs=pl.BlockSpec((MOBA_BLOCK, D_MODEL), lambda b, i: (b * MOBA_NBLK + i, 0)),
        out_shape=jax.ShapeDtypeStruct((M_TOKENS, D_MODEL), BF16),
        scratch_shapes=[pltpu.VMEM((HEADS, HEAD_DIM, MOBA_BLOCK), F32),
                        pltpu.VMEM((HEADS, MOBA_BLOCK, MOBA_BLOCK), F32)],
        compiler_params=_params(("parallel", "arbitrary")),
        name="moba_att",
    )(q_scaled, k_rot, v_t, bias)


HGRN_TILE = MOBA_BLOCK
PM_LEVELS = (1, 2)


def _hgrn_constants():
    c = HGRN_CHUNK
    idx = np.arange(c)
    tri = (idx[None, :] <= idx[:, None]).astype(np.float32)
    rows = [tri]
    masks = []
    for lvl in range(HGRN_LEVELS):
        half = 1 << lvl
        group = idx // (2 * half)
        upper = (idx & half) != 0
        if lvl in PM_LEVELS:
            boundary = tri[group * (2 * half) + half - 1]
            rows.append(np.where(upper[:, None], tri - boundary, boundary - tri))
        masks.append(((group[:, None] == group[None, :]) & upper[:, None] & ~upper[None, :])
                     .astype(np.float32))
    masks.append(np.eye(c, dtype=np.float32))
    return np.concatenate(rows, axis=0), np.stack(masks)


def _hgrn_tile(q_ref, lf_ref, k_ref, v_ref, og_ref, nw_ref, pm_ref, mask_ref, o_ref, state_ref,
               fill):
    c = HGRN_CHUNK
    n_ch = HGRN_TILE // c
    units = [(ch, h) for ch in range(n_ch) for h in range(HEADS)]
    rows = lambda ch: slice(ch * c, (ch + 1) * c)
    lanes = lambda h: slice(h * HEAD_DIM, (h + 1) * HEAD_DIM)
    chunks_on_lanes = lambda t: jnp.concatenate([t[rows(ch), :] for ch in range(n_ch)], axis=1)

    lf = lf_ref[...]
    hi = lf.astype(BF16)
    rem = lf - hi.astype(F32)
    mid = rem.astype(BF16)
    lo = (rem - mid.astype(F32)).astype(BF16)
    split = jnp.concatenate([chunks_on_lanes(hi), chunks_on_lanes(mid), chunks_on_lanes(lo)], axis=0)
    sums = jnp.dot(pm_ref[...], split, preferred_element_type=F32)

    row = lax.broadcasted_iota(jnp.int32, (c, HEAD_DIM), 0)
    upper = [(row & (1 << lvl)) != 0 for lvl in range(HGRN_LEVELS)]

    def decayed_operands(u, ch, h):
        col = slice(u * HEAD_DIM, (u + 1) * HEAD_DIM)
        g = sums[0:c, col]
        q_bf, k_bf = q_ref[rows(ch), lanes(h)], k_ref[rows(ch), lanes(h)]
        qf, kk = q_bf.astype(F32), k_bf.astype(F32)
        zs = []
        for lvl in range(HGRN_LEVELS):
            half = 1 << lvl
            if lvl == 0:
                e = jnp.where(upper[0], lf_ref[rows(ch), lanes(h)], 0.0)
            elif lvl in PM_LEVELS:
                i = PM_LEVELS.index(lvl)
                e = sums[(i + 1) * c:(i + 2) * c, col]
            else:
                boundary = jnp.concatenate(
                    [jnp.broadcast_to(g[b + half - 1:b + half, :], (2 * half, HEAD_DIM))
                     for b in range(0, c, 2 * half)], axis=0)
                e = -jnp.abs(g - boundary)
            zs.append((jnp.where(upper[lvl], qf, kk) * jnp.exp2(e)).astype(BF16))
        g_last = g[c - 1:c, :]
        return dict(q=q_bf, k=k_bf, zs=zs,
                    q_in=(qf * jnp.exp2(g)).astype(BF16),
                    k_out=(kk * jnp.exp2(g_last - g)).astype(BF16),
                    decay=jnp.exp2(g_last),
                    v=v_ref[rows(ch), lanes(h)])

    def state_free_matmuls(op):
        score = lambda lhs, rhs: lax.dot_general(lhs, rhs, NT_DIMS,
                                                 preferred_element_type=F32).astype(BF16)
        a = score(op["q"], op["k"]) * mask_ref[HGRN_LEVELS]
        for lvl, z in enumerate(op["zs"]):
            a = a + score(z, z) * mask_ref[lvl]
        op["o"] = jnp.dot(a, op["v"], preferred_element_type=F32)
        op["update"] = lax.dot_general(op["v"], op["k_out"], TN_DIMS, preferred_element_type=F32)

    ops = {}
    for u, (ch, h) in enumerate(units):
        ops[ch, h] = decayed_operands(u, ch, h)
        if u >= 1:
            state_free_matmuls(ops[units[u - 1]])
        fill()
    state_free_matmuls(ops[units[-1]])

    for h in range(HEADS):
        state_t = state_ref[h]
        for ch in range(n_ch):
            op = ops[ch, h]
            op["state"] = state_t.astype(BF16)
            state_t = state_t * op["decay"] + op["update"]
        state_ref[h] = state_t
    for ch, h in units:
        op = ops[ch, h]
        o = op["o"] + lax.dot_general(op["q_in"], op["state"], NT_DIMS, preferred_element_type=F32)
        ms = jnp.mean(o * o, axis=-1, keepdims=True)
        y = o * lax.rsqrt(ms + RMS_EPS) * nw_ref[:, lanes(h)] * og_ref[rows(ch), lanes(h)]
        o_ref[rows(ch), lanes(h)] = y.astype(o_ref.dtype)


MIX_TM = 512


def _mix_kernel(ya_ref, yb_ref, gate_ref, x_ref, wa_ref, wb_ref, wo_ref, lnw_ref, lnb_ref,
                o_ref, ob_ref):
    half = MIX_TM // 2
    first, second = slice(0, half), slice(half, MIX_TM)

    def merged(rows, za):
        zb = jnp.dot(yb_ref[rows, :], wb_ref[...], preferred_element_type=F32)
        return (gate_ref[rows, :D_MODEL] * za + gate_ref[rows, D_MODEL:] * zb).astype(BF16)

    def pre_norm(rows, za):
        mixed = jnp.dot(merged(rows, za), wo_ref[...], preferred_element_type=F32)
        return DN_ALPHA * x_ref[rows, :] + mixed

    def finish(rows, r):
        y = _layer_norm(r, lnw_ref[...], lnb_ref[...])
        o_ref[rows, :] = y
        ob_ref[rows, :] = y.astype(BF16)

    branch_a = lambda rows: jnp.dot(ya_ref[rows, :], wa_ref[...], preferred_element_type=F32)
    r_first = pre_norm(first, branch_a(first))
    za_second = branch_a(second)
    finish(first, r_first)
    finish(second, pre_norm(second, za_second))


def _mix(ya, yb, gates, x, wa, wb, wo, lnw, lnb):
    d = D_MODEL
    row = lambda n: pl.BlockSpec((MIX_TM, n), lambda i: (i, 0))
    return pl.pallas_call(
        _mix_kernel,
        grid=(M_TOKENS // MIX_TM,),
        in_specs=[row(d), row(d), row(2 * d), row(d),
                  _resident((d, d)), _resident((d, d)), _resident((d, d)),
                  _resident((1, d)), _resident((1, d))],
        out_specs=[row(d), row(d)],
        out_shape=[jax.ShapeDtypeStruct((M_TOKENS, d), F32),
                   jax.ShapeDtypeStruct((M_TOKENS, d), BF16)],
        compiler_params=_params(("parallel",)),
        name="mix",
    )(ya, yb, gates, x, wa, wb, wo, lnw, lnb)


FFN_TM = 512
FFN_CHUNK = 1408


def _ffn_kernel(xb_ref, x_ref, win_ref, wd_ref, lnw_ref, lnb_ref, o_ref):
    half = FFN_TM // 2
    n_chunks = D_FF // FFN_CHUNK

    def chunk(rows, c):
        cs = slice(c * FFN_CHUNK, (c + 1) * FFN_CHUNK)
        xb = xb_ref[rows, :]
        up = slice(D_FF + c * FFN_CHUNK, D_FF + (c + 1) * FFN_CHUNK)
        hg = jnp.dot(xb, win_ref[:, cs], preferred_element_type=F32)
        hu = jnp.dot(xb, win_ref[:, up], preferred_element_type=F32)
        act = (hg * _sigmoid(hg) * hu).astype(BF16)
        return jnp.dot(act, wd_ref[cs, :], preferred_element_type=F32)

    def finish(rows, y):
        o_ref[rows, :] = _layer_norm(DN_ALPHA * x_ref[rows, :] + y, lnw_ref[...], lnb_ref[...])

    first, second = slice(0, half), slice(half, FFN_TM)
    y_first = sum(chunk(first, c) for c in range(n_chunks))
    y_second = chunk(second, 0)
    finish(first, y_first)
    for c in range(1, n_chunks):
        y_second = y_second + chunk(second, c)
    finish(second, y_second)


def _ffn(xb, x, w_in, wd, lnw, lnb):
    d = D_MODEL
    row = lambda n: pl.BlockSpec((FFN_TM, n), lambda i: (i, 0))
    return pl.pallas_call(
        _ffn_kernel,
        grid=(M_TOKENS // FFN_TM,),
        in_specs=[row(d), row(d), _resident((d, 2 * D_FF)),
                  _resident((D_FF, d)), _resident((1, d)), _resident((1, d))],
        out_specs=row(d),
        out_shape=jax.ShapeDtypeStruct((M_TOKENS, d), F32),
        compiler_params=_params(("parallel",)),
        name="ffn",
    )(xb, x, w_in, wd, lnw, lnb)


def _rope_frequencies():
    half = HEAD_DIM // 2
    inv_freq = ROPE_THETA ** (-jnp.arange(half, dtype=F32) / half)
    return jnp.concatenate([inv_freq, inv_freq]).reshape(1, HEAD_DIM)


def _layer(x2, w_in, lb_logits, hgrn_norm_w, w_branch_a, w_branch_b, b_gate, w_out,
           ln1_w, ln1_b, w_ffn_in, w_ffn_down, ln2_w, ln2_b):
    d = D_MODEL
    row2 = lambda v: v.reshape(1, -1).astype(F32)

    q_scaled, bias, k_rot, v_t, gates, y_a = _proj(
        x2, w_in.astype(BF16), _rope_frequencies(), row2(b_gate), lb_logits.astype(F32),
        row2(hgrn_norm_w))
    y_b = _moba_attention(q_scaled, k_rot, v_t, bias)

    x1, x1b = _mix(y_a, y_b, gates, x2, w_branch_a.astype(BF16), w_branch_b.astype(BF16),
                   w_out.astype(BF16), row2(ln1_w), row2(ln1_b))
    return _ffn(x1b, x1, w_ffn_in.astype(BF16), w_ffn_down.astype(BF16), row2(ln2_w), row2(ln2_b))


def kernel(x, w_in, lb_logits, hgrn_norm_w, w_branch_a, w_branch_b, b_gate, w_out, ln1_w, ln1_b,
           w_ffn_in, w_ffn_down, ln2_w, ln2_b):
    assert DEPTH == 1
    h = x.reshape(M_TOKENS, D_MODEL)
    h = _layer(h, w_in[0], lb_logits, hgrn_norm_w[0], w_branch_a[0], w_branch_b[0],
               b_gate[0], w_out[0], ln1_w[0], ln1_b[0], w_ffn_in[0], w_ffn_down[0],
               ln2_w[0], ln2_b[0])
    return h.reshape(BATCH, SEQ, D_MODEL)
```

```python
import numpy as np
import jax
import jax.numpy as jnp
from jax import lax
from jax.experimental import pallas as pl
from jax.experimental.pallas import tpu as pltpu

D_MODEL = 1024
BATCH = 4
SEQ = 4096
DEPTH = 1
HEADS = 8
HEAD_DIM = 128
HGRN_CHUNK = 64
HGRN_LEVELS = 6
MOBA_BLOCK = 256
MOBA_NBLK = SEQ // MOBA_BLOCK
MOBA_TOPK = 3
ROPE_THETA = 10000.0
D_FF = 2816
DN_ALPHA = (2.0 * DEPTH) ** 0.25
LN_EPS = 1e-5
RMS_EPS = 1e-6
M_TOKENS = BATCH * SEQ

MASK_VALUE = -1e30
VT_ROWS = HEAD_DIM + 16
LOG2_E = 1.4426950408889634
VMEM_LIMIT = 56 * 1024 * 1024

F32 = jnp.float32
BF16 = jnp.bfloat16
NT_DIMS = (((1,), (1,)), ((), ()))
TN_DIMS = (((0,), (0,)), ((), ()))


def _params(semantics):
    return pltpu.CompilerParams(dimension_semantics=semantics, vmem_limit_bytes=VMEM_LIMIT)


def _resident(shape):
    return pl.BlockSpec(shape, lambda *_: (0,) * len(shape), pipeline_mode=pl.Buffered(1))


def _sigmoid(z):
    return 1.0 / (1.0 + jnp.exp(-z))


def _layer_norm(r, w, b):
    mu = jnp.mean(r, axis=-1, keepdims=True)
    d = r - mu
    var = jnp.mean(d * d, axis=-1, keepdims=True)
    return d * lax.rsqrt(var + LN_EPS) * w + b


PROJ_SLAB = 256
N_PROJ = 9 * D_MODEL
COL_HGRN_Q, COL_HGRN_F, COL_HGRN_V, COL_HGRN_GATE = 0, D_MODEL, 2 * D_MODEL, 3 * D_MODEL
COL_MOBA_Q, COL_MOBA_K, COL_MOBA_V, COL_MERGE_GATE = 4 * D_MODEL, 5 * D_MODEL, 6 * D_MODEL, 7 * D_MODEL


def _proj_kernel(x_ref, w_ref, freq_ref,
                 bg_ref, lbl_ref, nw_ref, pm_ref, mask_ref,
                 q_ref, bias_ref, k_ref, vt_ref, gate_ref, ya_ref,
                 km_ref, state_ref, hq_ref, hog_ref, hlf_ref, hk_ref, hv_ref, xb_ref, qf_ref,
                 cos_all_ref, sin_all_ref):
    own = pl.program_id(1)
    lanes = HEADS * MOBA_NBLK

    @pl.when(own == 0)
    def _():
        km_ref[...] = jnp.zeros_like(km_ref)
        state_ref[...] = jnp.zeros_like(state_ref)

    xb_ref[...] = x_ref[...].astype(BF16)
    block_rows = pl.ds(pl.multiple_of(own * MOBA_BLOCK, MOBA_BLOCK), MOBA_BLOCK)

    @pl.when(pl.program_id(0) == 0)
    def _():
        position = own * MOBA_BLOCK + lax.broadcasted_iota(jnp.int32, (MOBA_BLOCK, HEAD_DIM), 0)
        angle = position.astype(F32) * freq_ref[...]
        first_half = lax.broadcasted_iota(jnp.int32, (MOBA_BLOCK, HEAD_DIM), 1) < HEAD_DIM // 2
        cos_all_ref[block_rows, :] = jnp.cos(angle)
        sin_all_ref[block_rows, :] = jnp.where(first_half, -jnp.sin(angle), jnp.sin(angle))

    cos_ref = cos_all_ref.at[block_rows, :]
    sin_ref = sin_all_ref.at[block_rows, :]
    head = lambda h: slice(h * HEAD_DIM, (h + 1) * HEAD_DIM)
    slab = lambda i: slice(i * PROJ_SLAB, (i + 1) * PROJ_SLAB)
    n_slabs = D_MODEL // PROJ_SLAB

    def project(first_col, i):
        cols = slice(first_col + i * PROJ_SLAB, first_col + (i + 1) * PROJ_SLAB)
        return jnp.dot(xb_ref[...], w_ref[:, cols], preferred_element_type=F32)

    rope = lambda t: t * cos_ref[...] + pltpu.roll(t, HEAD_DIM // 2, 1) * sin_ref[...]


    def hgrn_silu(i):
        def run():
            acc = project(COL_HGRN_Q if i < n_slabs else COL_HGRN_GATE, i % n_slabs)
            dst = hq_ref if i < n_slabs else hog_ref
            dst[:, slab(i % n_slabs)] = (acc * _sigmoid(acc)).astype(dst.dtype)
        return run

    def hgrn_forget(i):
        def run():
            l0, l1 = lbl_ref[0:1, slab(i)], lbl_ref[1:2, slab(i)]
            top = jnp.maximum(l0, l1)
            e0, e1 = jnp.exp(l0 - top), jnp.exp(l1 - top)
            lb = e0 / (e0 + e1)
            z = project(COL_HGRN_F, i)
            hlf_ref[:, slab(i)] = jnp.log(lb + (1.0 - lb) * _sigmoid(z)) * LOG2_E
            hk_ref[:, slab(i)] = ((1.0 - lb) * _sigmoid(-z)).astype(hk_ref.dtype)
        return run

    def hgrn_value(i):
        def run():
            hv_ref[:, slab(i)] = project(COL_HGRN_V, i).astype(hv_ref.dtype)
        return run

    heads_of = lambda i: range(i * PROJ_SLAB // HEAD_DIM, (i + 1) * PROJ_SLAB // HEAD_DIM)
    in_slab = lambda i, h: slice((h - heads_of(i)[0]) * HEAD_DIM, (h - heads_of(i)[0] + 1) * HEAD_DIM)

    def moba_query(i):
        def run():
            acc = project(COL_MOBA_Q, i)
            for h in heads_of(i):
                qh = rope(acc[:, in_slab(i, h)])
                qf_ref[:, head(h)] = qh
                q_ref[:, head(h)] = (qh * (HEAD_DIM ** -0.5 * LOG2_E)).astype(q_ref.dtype)
        return run

    def moba_choice():
        split = lambda t: (t.astype(BF16), (t - t.astype(BF16).astype(F32)).astype(BF16))
        q_hi, q_lo = split(qf_ref[...])
        km_hi, km_lo = split(km_ref[...])
        gate = lax.dot_general(jnp.concatenate([q_hi, q_lo, q_hi], axis=1),
                               jnp.concatenate([km_hi, km_hi, km_lo], axis=1), NT_DIMS,
                               preferred_element_type=F32)
        lane = lax.broadcasted_iota(jnp.int32, (MOBA_BLOCK, lanes), 1)
        eligible = lane < own * HEADS
        gate = jnp.where(eligible, gate, -jnp.inf)
        rank = jnp.zeros((MOBA_BLOCK, lanes), F32)
        for r in range(1, MOBA_NBLK):
            partner = pltpu.roll(gate, lanes - r * HEADS, 1)
            wrapped = lane >= lanes - r * HEADS
            beats = (partner > gate) | (wrapped & (partner == gate))
            rank = rank + beats.astype(F32)
        selected = eligible & (rank < float(MOBA_TOPK))
        bias_ref[0] = jnp.where(selected, 0.0, MASK_VALUE).T

    def moba_key(i):
        def run():
            acc = project(COL_MOBA_K, i)
            for h in heads_of(i):
                rot = rope(acc[:, in_slab(i, h)])
                k_ref[:, head(h)] = rot.astype(k_ref.dtype)
                mean = jnp.mean(rot, axis=0, keepdims=True)
                zeros = lambda n: [jnp.zeros((1, n * HEAD_DIM), F32)] if n else []
                km_ref[pl.ds(own * HEADS + h, 1), :] = jnp.concatenate(
                    zeros(h) + [mean] + zeros(HEADS - 1 - h), axis=1)
        return run

    def moba_value(i):
        def run():
            vt = project(COL_MOBA_V, i).T.astype(vt_ref.dtype)
            pad_row = lax.broadcasted_iota(jnp.int32, (VT_ROWS - HEAD_DIM, MOBA_BLOCK), 0)
            ones_row = jnp.where(pad_row == 0, 1.0, 0.0).astype(vt_ref.dtype)
            for h in heads_of(i):
                vt_ref[0, 0, h * VT_ROWS:h * VT_ROWS + HEAD_DIM, :] = vt[in_slab(i, h), :]
                vt_ref[0, 0, h * VT_ROWS + HEAD_DIM:(h + 1) * VT_ROWS, :] = ones_row
        return run

    def merge_gate(i):
        def run():
            acc = project(COL_MERGE_GATE, i) + bg_ref[:, slab(i)]
            gate_ref[:, slab(i)] = _sigmoid(acc).astype(gate_ref.dtype)
        return run

    hgrn_inputs = ([hgrn_silu(i) for i in range(2 * n_slabs)]
                   + [hgrn_forget(i) for i in range(n_slabs)]
                   + [hgrn_value(i) for i in range(n_slabs)])
    queries = [moba_query(i) for i in range(n_slabs)] + [moba_choice]
    for i, piece in enumerate(hgrn_inputs):
        piece()
        if i < len(queries):
            queries[i]()

    rest = ([moba_key(i) for i in range(n_slabs)] + [moba_value(i) for i in range(n_slabs)]
            + [merge_gate(i) for i in range(2 * n_slabs)])
    calls = []

    def fill():
        calls.append(None)
        if rest and len(calls) % 2 == 0:
            rest.pop(0)()

    _hgrn_tile(hq_ref, hlf_ref, hk_ref, hv_ref, hog_ref, nw_ref, pm_ref, mask_ref, ya_ref,
               state_ref, fill)
    while rest:
        fill()


def _proj(x, w_in, rope_freq, b_gate, lb_logits, norm_w):
    d = D_MODEL
    lanes = HEADS * MOBA_NBLK
    pm, masks = _hgrn_constants()
    pm = np.tile(pm, (1, 3))
    row = lambda n: pl.BlockSpec((MOBA_BLOCK, n), lambda b, i: (b * MOBA_NBLK + i, 0))
    shape = lambda n, dt: jax.ShapeDtypeStruct((M_TOKENS, n), dt)
    tile = lambda dt: pltpu.VMEM((MOBA_BLOCK, d), dt)
    return pl.pallas_call(
        _proj_kernel,
        grid=(BATCH, MOBA_NBLK),
        in_specs=[row(d), _resident((d, N_PROJ)), _resident((1, HEAD_DIM)), _resident((1, 2 * d)), _resident((DEPTH + 1, d)),
                  _resident((1, d)), _resident(pm.shape), _resident(masks.shape)],
        out_specs=[row(d),
                   pl.BlockSpec((1, lanes, MOBA_BLOCK), lambda b, i: (b, 0, i)),
                   row(d),
                   pl.BlockSpec((1, 1, HEADS * VT_ROWS, MOBA_BLOCK), lambda b, i: (b, i, 0, 0)),
                   row(2 * d), row(d)],
        out_shape=[shape(d, BF16),
                   jax.ShapeDtypeStruct((BATCH, lanes, SEQ), F32),
                   shape(d, BF16),
                   jax.ShapeDtypeStruct((BATCH, MOBA_NBLK, HEADS * VT_ROWS, MOBA_BLOCK), BF16),
                   shape(2 * d, BF16), shape(d, BF16)],
        scratch_shapes=[pltpu.VMEM((lanes, d), F32),
                        pltpu.VMEM((HEADS, HEAD_DIM, HEAD_DIM), F32),
                        tile(BF16), tile(BF16), tile(F32), tile(BF16), tile(BF16),
                        tile(BF16), tile(F32),
                        pltpu.VMEM((SEQ, HEAD_DIM), F32), pltpu.VMEM((SEQ, HEAD_DIM), F32)],
        compiler_params=_params(("arbitrary", "arbitrary")),
        name="proj",
    )(x, w_in, rope_freq, b_gate, lb_logits, norm_w,
      jnp.asarray(pm, BF16), jnp.asarray(masks, BF16))


def _moba_att_kernel(q_ref, k_ref, vt_ref, bias_ref, o_ref, acc_ref, s_ref):
    own = pl.program_id(1)

    def head(h):
        return slice(h * HEAD_DIM, (h + 1) * HEAD_DIM)

    def scores(h, j):
        kj = k_ref[pl.ds(pl.multiple_of(j * MOBA_BLOCK, MOBA_BLOCK), MOBA_BLOCK), head(h)]
        return lax.dot_general(kj, q_ref[:, head(h)], NT_DIMS, preferred_element_type=F32)

    def item(h, j, s, m, l, next_scores, bias=None):
        top = jnp.max(s, axis=0, keepdims=True)
        m_new = jnp.maximum(m, top if bias is None else top + bias)
        a = jnp.exp2(m - m_new)
        p = jnp.exp2(s - (m_new if bias is None else m_new - 2.0 * bias))
        if next_scores is not None:
            s_ref[h] = next_scores()
        pv = jnp.dot(vt_ref[0, j, h * VT_ROWS:(h + 1) * VT_ROWS, :], p.astype(BF16),
                     preferred_element_type=F32)
        acc_ref[h] = a * acc_ref[h] + pv[:HEAD_DIM]
        return m_new, a * l + pv[HEAD_DIM:HEAD_DIM + 1]

    acc_ref[...] = jnp.zeros_like(acc_ref)
    for h in range(HEADS):
        s_ref[h] = scores(h, 0)
    m0 = jnp.full((1, MOBA_BLOCK), MASK_VALUE, F32)
    l0 = jnp.zeros((1, MOBA_BLOCK), F32)

    def body(j, carry):
        ms, ls = list(carry[0]), list(carry[1])
        for h in range(HEADS):
            ms[h], ls[h] = item(h, j, s_ref[h], ms[h], ls[h], lambda h=h: scores(h, j + 1),
                                bias_ref[0, pl.ds(j * HEADS + h, 1), :])
        return tuple(ms), tuple(ls)

    def several(n):
        def run(t, carry):
            for r in range(n):
                carry = body(n * t + r, carry)
            return carry
        return run

    carry = ((m0,) * HEADS, (l0,) * HEADS)
    done = 0
    for n in (4, 2, 1):
        trips = (own - done) // n
        carry = lax.fori_loop(done // n, done // n + trips, several(n), carry)
        done = done + trips * n
    ms, ls = carry

    key_pos = lax.broadcasted_iota(jnp.int32, (MOBA_BLOCK, MOBA_BLOCK), 0)
    qry_pos = lax.broadcasted_iota(jnp.int32, (MOBA_BLOCK, MOBA_BLOCK), 1)
    causal = key_pos <= qry_pos
    for h in range(HEADS):
        s = jnp.where(causal, s_ref[h], MASK_VALUE)
        _, l = item(h, own, s, ms[h], ls[h], None)
        o_ref[:, head(h)] = (acc_ref[h] * (1.0 / l)).T.astype(o_ref.dtype)


def _moba_attention(q_scaled, k_rot, v_t, bias):
    lanes = HEADS * MOBA_NBLK
    return pl.pallas_call(
        _moba_att_kernel,
        grid=(BATCH, MOBA_NBLK),
        in_specs=[pl.BlockSpec((MOBA_BLOCK, D_MODEL), lambda b, i: (b * MOBA_NBLK + i, 0)),
                  pl.BlockSpec((SEQ, D_MODEL), lambda b, i: (b, 0)),
                  pl.BlockSpec((1, MOBA_NBLK, HEADS * VT_ROWS, MOBA_BLOCK),
                               lambda b, i: (b, 0, 0, 0)),
                  pl.BlockSpec((1, lanes, MOBA_BLOCK), lambda b, i: (b, 0, i))],
        out_specs=pl.BlockSpec((MOBA_BLOCK, D_MODEL), lambda b, i: (b * MOBA_NBLK + i, 0)),
        out_shape=jax.ShapeDtypeStruct((M_TOKENS, D_MODEL), BF16),
        scratch_shapes=[pltpu.VMEM((HEADS, HEAD_DIM, MOBA_BLOCK), F32),
                        pltpu.VMEM((HEADS, MOBA_BLOCK, MOBA_BLOCK), F32)],
        compiler_params=_params(("parallel", "arbitrary")),
        name="moba_att",
    )(q_scaled, k_rot, v_t, bias)


HGRN_TILE = MOBA_BLOCK
PM_LEVELS = (1, 2)


def _hgrn_constants():
    c = HGRN_CHUNK
    idx = np.arange(c)
    tri = (idx[None, :] <= idx[:, None]).astype(np.float32)
    rows = [tri]
    masks = []
    for lvl in range(HGRN_LEVELS):
        half = 1 << lvl
        group = idx // (2 * half)
        upper = (idx & half) != 0
        if lvl in PM_LEVELS:
            boundary = tri[group * (2 * half) + half - 1]
            rows.append(np.where(upper[:, None], tri - boundary, boundary - tri))
        masks.append(((group[:, None] == group[None, :]) & upper[:, None] & ~upper[None, :])
                     .astype(np.float32))
    masks.append(np.eye(c, dtype=np.float32))
    return np.concatenate(rows, axis=0), np.stack(masks)


def _hgrn_tile(q_ref, lf_ref, k_ref, v_ref, og_ref, nw_ref, pm_ref, mask_ref, o_ref, state_ref,
               fill):
    c = HGRN_CHUNK
    n_ch = HGRN_TILE // c
    units = [(ch, h) for ch in range(n_ch) for h in range(HEADS)]
    rows = lambda ch: slice(ch * c, (ch + 1) * c)
    lanes = lambda h: slice(h * HEAD_DIM, (h + 1) * HEAD_DIM)
    chunks_on_lanes = lambda t: jnp.concatenate([t[rows(ch), :] for ch in range(n_ch)], axis=1)

    lf = lf_ref[...]
    hi = lf.astype(BF16)
    rem = lf - hi.astype(F32)
    mid = rem.astype(BF16)
    lo = (rem - mid.astype(F32)).astype(BF16)
    split = jnp.concatenate([chunks_on_lanes(hi), chunks_on_lanes(mid), chunks_on_lanes(lo)], axis=0)
    sums = jnp.dot(pm_ref[...], split, preferred_element_type=F32)

    row = lax.broadcasted_iota(jnp.int32, (c, HEAD_DIM), 0)
    upper = [(row & (1 << lvl)) != 0 for lvl in range(HGRN_LEVELS)]

    def decayed_operands(u, ch, h):
        col = slice(u * HEAD_DIM, (u + 1) * HEAD_DIM)
        g = sums[0:c, col]
        q_bf, k_bf = q_ref[rows(ch), lanes(h)], k_ref[rows(ch), lanes(h)]
        qf, kk = q_bf.astype(F32), k_bf.astype(F32)
        zs = []
        for lvl in range(HGRN_LEVELS):
            half = 1 << lvl
            if lvl == 0:
                e = jnp.where(upper[0], lf_ref[rows(ch), lanes(h)], 0.0)
            elif lvl in PM_LEVELS:
                i = PM_LEVELS.index(lvl)
                e = sums[(i + 1) * c:(i + 2) * c, col]
            else:
                boundary = jnp.concatenate(
                    [jnp.broadcast_to(g[b + half - 1:b + half, :], (2 * half, HEAD_DIM))
                     for b in range(0, c, 2 * half)], axis=0)
                e = -jnp.abs(g - boundary)
            zs.append((jnp.where(upper[lvl], qf, kk) * jnp.exp2(e)).astype(BF16))
        g_last = g[c - 1:c, :]
        return dict(q=q_bf, k=k_bf, zs=zs,
                    q_in=(qf * jnp.exp2(g)).astype(BF16),
                    k_out=(kk * jnp.exp2(g_last - g)).astype(BF16),
                    decay=jnp.exp2(g_last),
                    v=v_ref[rows(ch), lanes(h)])

    def state_free_matmuls(op):
        score = lambda lhs, rhs: lax.dot_general(lhs, rhs, NT_DIMS,
                                                 preferred_element_type=F32).astype(BF16)
        a = score(op["q"], op["k"]) * mask_ref[HGRN_LEVELS]
        for lvl, z in enumerate(op["zs"]):
            a = a + score(z, z) * mask_ref[lvl]
        op["o"] = jnp.dot(a, op["v"], preferred_element_type=F32)
        op["update"] = lax.dot_general(op["v"], op["k_out"], TN_DIMS, preferred_element_type=F32)

    ops = {}
    for u, (ch, h) in enumerate(units):
        ops[ch, h] = decayed_operands(u, ch, h)
        if u >= 1:
            state_free_matmuls(ops[units[u - 1]])
        fill()
    state_free_matmuls(ops[units[-1]])

    for h in range(HEADS):
        state_t = state_ref[h]
        for ch in range(n_ch):
            op = ops[ch, h]
            op["state"] = state_t.astype(BF16)
            state_t = state_t * op["decay"] + op["update"]
        state_ref[h] = state_t
    for ch, h in units:
        op = ops[ch, h]
        o = op["o"] + lax.dot_general(op["q_in"], op["state"], NT_DIMS, preferred_element_type=F32)
        ms = jnp.mean(o * o, axis=-1, keepdims=True)
        y = o * lax.rsqrt(ms + RMS_EPS) * nw_ref[:, lanes(h)] * og_ref[rows(ch), lanes(h)]
        o_ref[rows(ch), lanes(h)] = y.astype(o_ref.dtype)


MIX_TM = 512


def _mix_kernel(ya_ref, yb_ref, gate_ref, x_ref, wa_ref, wb_ref, wo_ref, lnw_ref, lnb_ref,
                o_ref, ob_ref):
    half = MIX_TM // 2
    first, second = slice(0, half), slice(half, MIX_TM)

    def merged(rows, za):
        zb = jnp.dot(yb_ref[rows, :], wb_ref[...], preferred_element_type=F32)
        return (gate_ref[rows, :D_MODEL] * za + gate_ref[rows, D_MODEL:] * zb).astype(BF16)

    def pre_norm(rows, za):
        mixed = jnp.dot(merged(rows, za), wo_ref[...], preferred_element_type=F32)
        return DN_ALPHA * x_ref[rows, :] + mixed

    def finish(rows, r):
        y = _layer_norm(r, lnw_ref[...], lnb_ref[...])
        o_ref[rows, :] = y
        ob_ref[rows, :] = y.astype(BF16)

    branch_a = lambda rows: jnp.dot(ya_ref[rows, :], wa_ref[...], preferred_element_type=F32)
    r_first = pre_norm(first, branch_a(first))
    za_second = branch_a(second)
    finish(first, r_first)
    finish(second, pre_norm(second, za_second))


def _mix(ya, yb, gates, x, wa, wb, wo, lnw, lnb):
    d = D_MODEL
    row = lambda n: pl.BlockSpec((MIX_TM, n), lambda i: (i, 0))
    return pl.pallas_call(
        _mix_kernel,
        grid=(M_TOKENS // MIX_TM,),
        in_specs=[row(d), row(d), row(2 * d), row(d),
                  _resident((d, d)), _resident((d, d)), _resident((d, d)),
                  _resident((1, d)), _resident((1, d))],
        out_specs=[row(d), row(d)],
        out_shape=[jax.ShapeDtypeStruct((M_TOKENS, d), F32),
                   jax.ShapeDtypeStruct((M_TOKENS, d), BF16)],
        compiler_params=_params(("parallel",)),
        name="mix",
    )(ya, yb, gates, x, wa, wb, wo, lnw, lnb)


FFN_TM = 512
FFN_CHUNK = 1408


def _ffn_kernel(xb_ref, x_ref, win_ref, wd_ref, lnw_ref, lnb_ref, o_ref):
    half = FFN_TM // 2
    n_chunks = D_FF // FFN_CHUNK

    def chunk(rows, c):
        cs = slice(c * FFN_CHUNK, (c + 1) * FFN_CHUNK)
        xb = xb_ref[rows, :]
        up = slice(D_FF + c * FFN_CHUNK, D_FF + (c + 1) * FFN_CHUNK)
        hg = jnp.dot(xb, win_ref[:, cs], preferred_element_type=F32)
        hu = jnp.dot(xb, win_ref[:, up], preferred_element_type=F32)
        act = (hg * _sigmoid(hg) * hu).astype(BF16)
        return jnp.dot(act, wd_ref[cs, :], preferred_element_type=F32)

    def finish(rows, y):
        o_ref[rows, :] = _layer_norm(DN_ALPHA * x_ref[rows, :] + y, lnw_ref[...], lnb_ref[...])

    first, second = slice(0, half), slice(half, FFN_TM)
    y_first = sum(chunk(first, c) for c in range(n_chunks))
    y_second = chunk(second, 0)
    finish(first, y_first)
    for c in range(1, n_chunks):
        y_second = y_second + chunk(second, c)
    finish(second, y_second)


def _ffn(xb, x, w_in, wd, lnw, lnb):
    d = D_MODEL
    row = lambda n: pl.BlockSpec((FFN_TM, n), lambda i: (i, 0))
    return pl.pallas_call(
        _ffn_kernel,
        grid=(M_TOKENS // FFN_TM,),
        in_specs=[row(d), row(d), _resident((d, 2 * D_FF)),
                  _resident((D_FF, d)), _resident((1, d)), _resident((1, d))],
        out_specs=row(d),
        out_shape=jax.ShapeDtypeStruct((M_TOKENS, d), F32),
        compiler_params=_params(("parallel",)),
        name="ffn",
    )(xb, x, w_in, wd, lnw, lnb)


def _rope_frequencies():
    half = HEAD_DIM // 2
    inv_freq = ROPE_THETA ** (-jnp.arange(half, dtype=F32) / half)
    return jnp.concatenate([inv_freq, inv_freq]).reshape(1, HEAD_DIM)


def _layer(x2, w_in, lb_logits, hgrn_norm_w, w_branch_a, w_branch_b, b_gate, w_out,
           ln1_w, ln1_b, w_ffn_in, w_ffn_down, ln2_w, ln2_b):
    d = D_MODEL
    row2 = lambda v: v.reshape(1, -1).astype(F32)

    q_scaled, bias, k_rot, v_t, gates, y_a = _proj(
        x2, w_in.astype(BF16), _rope_frequencies(), row2(b_gate), lb_logits.astype(F32),
        row2(hgrn_norm_w))
    y_b = _moba_attention(q_scaled, k_rot, v_t, bias)

    x1, x1b = _mix(y_a, y_b, gates, x2, w_branch_a.astype(BF16), w_branch_b.astype(BF16),
                   w_out.astype(BF16), row2(ln1_w), row2(ln1_b))
    return _ffn(x1b, x1, w_ffn_in.astype(BF16), w_ffn_down.astype(BF16), row2(ln2_w), row2(ln2_b))


def kernel(x, w_in, lb_logits, hgrn_norm_w, w_branch_a, w_branch_b, b_gate, w_out, ln1_w, ln1_b,
           w_ffn_in, w_ffn_down, ln2_w, ln2_b):
    assert DEPTH == 1
    h = x.reshape(M_TOKENS, D_MODEL)
    h = _layer(h, w_in[0], lb_logits, hgrn_norm_w[0], w_branch_a[0], w_branch_b[0],
               b_gate[0], w_out[0], ln1_w[0], ln1_b[0], w_ffn_in[0], w_ffn_down[0],
               ln2_w[0], ln2_b[0])
    return h.reshape(BATCH, SEQ, D_MODEL)
```

```python
import numpy as np
import jax
import jax.numpy as jnp
from jax import lax
from jax.experimental import pallas as pl
from jax.experimental.pallas import tpu as pltpu

D_MODEL = 1024
BATCH = 4
SEQ = 4096
DEPTH = 1
HEADS = 8
HEAD_DIM = 128
HGRN_CHUNK = 64
HGRN_LEVELS = 6
MOBA_BLOCK = 256
MOBA_NBLK = SEQ // MOBA_BLOCK
MOBA_TOPK = 3
ROPE_THETA = 10000.0
D_FF = 2816
DN_ALPHA = (2.0 * DEPTH) ** 0.25
LN_EPS = 1e-5
RMS_EPS = 1e-6
M_TOKENS = BATCH * SEQ

MASK_VALUE = -1e30
VT_ROWS = HEAD_DIM + 16
LOG2_E = 1.4426950408889634
VMEM_LIMIT = 56 * 1024 * 1024

F32 = jnp.float32
BF16 = jnp.bfloat16
NT_DIMS = (((1,), (1,)), ((), ()))
TN_DIMS = (((0,), (0,)), ((), ()))


def _params(semantics):
    return pltpu.CompilerParams(dimension_semantics=semantics, vmem_limit_bytes=VMEM_LIMIT)


def _resident(shape):
    return pl.BlockSpec(shape, lambda *_: (0,) * len(shape), pipeline_mode=pl.Buffered(1))


def _sigmoid(z):
    return 1.0 / (1.0 + jnp.exp(-z))


def _layer_norm(r, w, b):
    mu = jnp.mean(r, axis=-1, keepdims=True)
    d = r - mu
    var = jnp.mean(d * d, axis=-1, keepdims=True)
    return d * lax.rsqrt(var + LN_EPS) * w + b


PROJ_SLAB = 256
N_PROJ = 9 * D_MODEL
COL_HGRN_Q, COL_HGRN_F, COL_HGRN_V, COL_HGRN_GATE = 0, D_MODEL, 2 * D_MODEL, 3 * D_MODEL
COL_MOBA_Q, COL_MOBA_K, COL_MOBA_V, COL_MERGE_GATE = 4 * D_MODEL, 5 * D_MODEL, 6 * D_MODEL, 7 * D_MODEL


def _proj_kernel(x_ref, w_ref, freq_ref,
                 bg_ref, lbl_ref, nw_ref, pm_ref, mask_ref,
                 q_ref, bias_ref, k_ref, vt_ref, gate_ref, ya_ref,
                 km_ref, state_ref, hq_ref, hog_ref, hlf_ref, hk_ref, hv_ref, xb_ref, qf_ref,
                 cos_all_ref, sin_all_ref):
    own = pl.program_id(1)
    lanes = HEADS * MOBA_NBLK

    @pl.when(own == 0)
    def _():
        km_ref[...] = jnp.zeros_like(km_ref)
        state_ref[...] = jnp.zeros_like(state_ref)

    xb_ref[...] = x_ref[...].astype(BF16)
    block_rows = pl.ds(pl.multiple_of(own * MOBA_BLOCK, MOBA_BLOCK), MOBA_BLOCK)

    @pl.when(pl.program_id(0) == 0)
    def _():
        position = own * MOBA_BLOCK + lax.broadcasted_iota(jnp.int32, (MOBA_BLOCK, HEAD_DIM), 0)
        angle = position.astype(F32) * freq_ref[...]
        first_half = lax.broadcasted_iota(jnp.int32, (MOBA_BLOCK, HEAD_DIM), 1) < HEAD_DIM // 2
        cos_all_ref[block_rows, :] = jnp.cos(angle)
        sin_all_ref[block_rows, :] = jnp.where(first_half, -jnp.sin(angle), jnp.sin(angle))

    cos_ref = cos_all_ref.at[block_rows, :]
    sin_ref = sin_all_ref.at[block_rows, :]
    head = lambda h: slice(h * HEAD_DIM, (h + 1) * HEAD_DIM)
    slab = lambda i: slice(i * PROJ_SLAB, (i + 1) * PROJ_SLAB)
    n_slabs = D_MODEL // PROJ_SLAB

    def project(first_col, i):
        cols = slice(first_col + i * PROJ_SLAB, first_col + (i + 1) * PROJ_SLAB)
        return jnp.dot(xb_ref[...], w_ref[:, cols], preferred_element_type=F32)

    rope = lambda t: t * cos_ref[...] + pltpu.roll(t, HEAD_DIM // 2, 1) * sin_ref[...]


    def hgrn_silu(i):
        def run():
            acc = project(COL_HGRN_Q if i < n_slabs else COL_HGRN_GATE, i % n_slabs)
            dst = hq_ref if i < n_slabs else hog_ref
            dst[:, slab(i % n_slabs)] = (acc * _sigmoid(acc)).astype(dst.dtype)
        return run

    def hgrn_forget(i):
        def run():
            l0, l1 = lbl_ref[0:1, slab(i)], lbl_ref[1:2, slab(i)]
            top = jnp.maximum(l0, l1)
            e0, e1 = jnp.exp(l0 - top), jnp.exp(l1 - top)
            lb = e0 / (e0 + e1)
            z = project(COL_HGRN_F, i)
            hlf_ref[:, slab(i)] = jnp.log(lb + (1.0 - lb) * _sigmoid(z)) * LOG2_E
            hk_ref[:, slab(i)] = ((1.0 - lb) * _sigmoid(-z)).astype(hk_ref.dtype)
        return run

    def hgrn_value(i):
        def run():
            hv_ref[:, slab(i)] = project(COL_HGRN_V, i).astype(hv_ref.dtype)
        return run

    heads_of = lambda i: range(i * PROJ_SLAB // HEAD_DIM, (i + 1) * PROJ_SLAB // HEAD_DIM)
    in_slab = lambda i, h: slice((h - heads_of(i)[0]) * HEAD_DIM, (h - heads_of(i)[0] + 1) * HEAD_DIM)

    def moba_query(i):
        def run():
            acc = project(COL_MOBA_Q, i)
            for h in heads_of(i):
                qh = rope(acc[:, in_slab(i, h)])
                qf_ref[:, head(h)] = qh
                q_ref[:, head(h)] = (qh * (HEAD_DIM ** -0.5 * LOG2_E)).astype(q_ref.dtype)
        return run

    def moba_choice():
        split = lambda t: (t.astype(BF16), (t - t.astype(BF16).astype(F32)).astype(BF16))
        q_hi, q_lo = split(qf_ref[...])
        km_hi, km_lo = split(km_ref[...])
        gate = lax.dot_general(jnp.concatenate([q_hi, q_lo, q_hi], axis=1),
                               jnp.concatenate([km_hi, km_hi, km_lo], axis=1), NT_DIMS,
                               preferred_element_type=F32)
        lane = lax.broadcasted_iota(jnp.int32, (MOBA_BLOCK, lanes), 1)
        eligible = lane < own * HEADS
        gate = jnp.where(eligible, gate, -jnp.inf)
        rank = jnp.zeros((MOBA_BLOCK, lanes), F32)
        for r in range(1, MOBA_NBLK):
            partner = pltpu.roll(gate, lanes - r * HEADS, 1)
            wrapped = lane >= lanes - r * HEADS
            beats = (partner > gate) | (wrapped & (partner == gate))
            rank = rank + beats.astype(F32)
        selected = eligible & (rank < float(MOBA_TOPK))
        bias_ref[0] = jnp.where(selected, 0.0, MASK_VALUE).T

    def moba_key(i):
        def run():
            acc = project(COL_MOBA_K, i)
            for h in heads_of(i):
                rot = rope(acc[:, in_slab(i, h)])
                k_ref[:, head(h)] = rot.astype(k_ref.dtype)
                mean = jnp.mean(rot, axis=0, keepdims=True)
                zeros = lambda n: [jnp.zeros((1, n * HEAD_DIM), F32)] if n else []
                km_ref[pl.ds(own * HEADS + h, 1), :] = jnp.concatenate(
                    zeros(h) + [mean] + zeros(HEADS - 1 - h), axis=1)
        return run

    def moba_value(i):
        def run():
            vt = project(COL_MOBA_V, i).T.astype(vt_ref.dtype)
            pad_row = lax.broadcasted_iota(jnp.int32, (VT_ROWS - HEAD_DIM, MOBA_BLOCK), 0)
            ones_row = jnp.where(pad_row == 0, 1.0, 0.0).astype(vt_ref.dtype)
            for h in heads_of(i):
                vt_ref[0, 0, h * VT_ROWS:h * VT_ROWS + HEAD_DIM, :] = vt[in_slab(i, h), :]
                vt_ref[0, 0, h * VT_ROWS + HEAD_DIM:(h + 1) * VT_ROWS, :] = ones_row
        return run

    def merge_gate(i):
        def run():
            acc = project(COL_MERGE_GATE, i) + bg_ref[:, slab(i)]
            gate_ref[:, slab(i)] = _sigmoid(acc).astype(gate_ref.dtype)
        return run

    hgrn_inputs = ([hgrn_silu(i) for i in range(2 * n_slabs)]
                   + [hgrn_forget(i) for i in range(n_slabs)]
                   + [hgrn_value(i) for i in range(n_slabs)])
    queries = [moba_query(i) for i in range(n_slabs)] + [moba_choice]
    for i, piece in enumerate(hgrn_inputs):
        piece()
        if i < len(queries):
            queries[i]()

    rest = ([moba_key(i) for i in range(n_slabs)] + [moba_value(i) for i in range(n_slabs)]
            + [merge_gate(i) for i in range(2 * n_slabs)])
    calls = []

    def fill():
        calls.append(None)
        if rest and len(calls) % 2 == 0:
            rest.pop(0)()

    _hgrn_tile(hq_ref, hlf_ref, hk_ref, hv_ref, hog_ref, nw_ref, pm_ref, mask_ref, ya_ref,
               state_ref, fill)
    while rest:
        fill()


def _proj(x, w_in, rope_freq, b_gate, lb_logits, norm_w):
    d = D_MODEL
    lanes = HEADS * MOBA_NBLK
    pm, masks = _hgrn_constants()
    pm = np.tile(pm, (1, 3))
    row = lambda n: pl.BlockSpec((MOBA_BLOCK, n), lambda b, i: (b * MOBA_NBLK + i, 0))
    shape = lambda n, dt: jax.ShapeDtypeStruct((M_TOKENS, n), dt)
    tile = lambda dt: pltpu.VMEM((MOBA_BLOCK, d), dt)
    return pl.pallas_call(
        _proj_kernel,
        grid=(BATCH, MOBA_NBLK),
        in_specs=[row(d), _resident((d, N_PROJ)), _resident((1, HEAD_DIM)), _resident((1, 2 * d)), _resident((DEPTH + 1, d)),
                  _resident((1, d)), _resident(pm.shape), _resident(masks.shape)],
        out_specs=[row(d),
                   pl.BlockSpec((1, lanes, MOBA_BLOCK), lambda b, i: (b, 0, i)),
                   row(d),
                   pl.BlockSpec((1, 1, HEADS * VT_ROWS, MOBA_BLOCK), lambda b, i: (b, i, 0, 0)),
                   row(2 * d), row(d)],
        out_shape=[shape(d, BF16),
                   jax.ShapeDtypeStruct((BATCH, lanes, SEQ), F32),
                   shape(d, BF16),
                   jax.ShapeDtypeStruct((BATCH, MOBA_NBLK, HEADS * VT_ROWS, MOBA_BLOCK), BF16),
                   shape(2 * d, BF16), shape(d, BF16)],
        scratch_shapes=[pltpu.VMEM((lanes, d), F32),
                        pltpu.VMEM((HEADS, HEAD_DIM, HEAD_DIM), F32),
                        tile(BF16), tile(BF16), tile(F32), tile(BF16), tile(BF16),
                        tile(BF16), tile(F32),
                        pltpu.VMEM((SEQ, HEAD_DIM), F32), pltpu.VMEM((SEQ, HEAD_DIM), F32)],
        compiler_params=_params(("arbitrary", "arbitrary")),
        name="proj",
    )(x, w_in, rope_freq, b_gate, lb_logits, norm_w,
      jnp.asarray(pm, BF16), jnp.asarray(masks, BF16))


def _moba_att_kernel(q_ref, qn_ref, k_ref, vt_ref, bias_ref, o_ref, acc_ref, s_ref):
    own = pl.program_id(1)

    def head(h):
        return slice(h * HEAD_DIM, (h + 1) * HEAD_DIM)

    def scores(h, j):
        kj = k_ref[pl.ds(pl.multiple_of(j * MOBA_BLOCK, MOBA_BLOCK), MOBA_BLOCK), head(h)]
        return lax.dot_general(kj, q_ref[:, head(h)], NT_DIMS, preferred_element_type=F32)

    def item(h, j, s, m, l, next_scores, bias=None):
        top = jnp.max(s, axis=0, keepdims=True)
        m_new = jnp.maximum(m, top if bias is None else top + bias)
        a = jnp.exp2(m - m_new)
        p = jnp.exp2(s - (m_new if bias is None else m_new - 2.0 * bias))
        if next_scores is not None:
            s_ref[h] = next_scores()
        pv = jnp.dot(vt_ref[0, j, h * VT_ROWS:(h + 1) * VT_ROWS, :], p.astype(BF16),
                     preferred_element_type=F32)
        acc_ref[h] = a * acc_ref[h] + pv[:HEAD_DIM]
        return m_new, a * l + pv[HEAD_DIM:HEAD_DIM + 1]

    acc_ref[...] = jnp.zeros_like(acc_ref)

    @pl.when(own == 0)
    def _():
        for h in range(HEADS):
            s_ref[h] = scores(h, 0)

    m0 = jnp.full((1, MOBA_BLOCK), MASK_VALUE, F32)
    l0 = jnp.zeros((1, MOBA_BLOCK), F32)

    def body(j, carry):
        ms, ls = list(carry[0]), list(carry[1])
        for h in range(HEADS):
            ms[h], ls[h] = item(h, j, s_ref[h], ms[h], ls[h], lambda h=h: scores(h, j + 1),
                                bias_ref[0, pl.ds(j * HEADS + h, 1), :])
        return tuple(ms), tuple(ls)

    def several(n):
        def run(t, carry):
            for r in range(n):
                carry = body(n * t + r, carry)
            return carry
        return run

    carry = ((m0,) * HEADS, (l0,) * HEADS)
    done = 0
    for n in (4, 2, 1):
        trips = (own - done) // n
        carry = lax.fori_loop(done // n, done // n + trips, several(n), carry)
        done = done + trips * n
    ms, ls = carry

    key_pos = lax.broadcasted_iota(jnp.int32, (MOBA_BLOCK, MOBA_BLOCK), 0)
    qry_pos = lax.broadcasted_iota(jnp.int32, (MOBA_BLOCK, MOBA_BLOCK), 1)
    causal = key_pos <= qry_pos
    def next_block_scores(h):
        return lax.dot_general(k_ref[0:MOBA_BLOCK, head(h)], qn_ref[:, head(h)], NT_DIMS,
                               preferred_element_type=F32)

    for h in range(HEADS):
        s = jnp.where(causal, s_ref[h], MASK_VALUE)
        _, l = item(h, own, s, ms[h], ls[h], lambda h=h: next_block_scores(h))
        o_ref[:, head(h)] = (acc_ref[h] * (1.0 / l)).T.astype(o_ref.dtype)


def _moba_attention(q_scaled, k_rot, v_t, bias):
    lanes = HEADS * MOBA_NBLK
    return pl.pallas_call(
        _moba_att_kernel,
        grid=(BATCH, MOBA_NBLK),
        in_specs=[pl.BlockSpec((MOBA_BLOCK, D_MODEL), lambda b, i: (b * MOBA_NBLK + i, 0)),
                  pl.BlockSpec((MOBA_BLOCK, D_MODEL),
                               lambda b, i: (b * MOBA_NBLK + jnp.minimum(i + 1, MOBA_NBLK - 1), 0)),
                  pl.BlockSpec((SEQ, D_MODEL), lambda b, i: (b, 0)),
                  pl.BlockSpec((1, MOBA_NBLK, HEADS * VT_ROWS, MOBA_BLOCK),
                               lambda b, i: (b, 0, 0, 0)),
                  pl.BlockSpec((1, lanes, MOBA_BLOCK), lambda b, i: (b, 0, i))],
        out_specs=pl.BlockSpec((MOBA_BLOCK, D_MODEL), lambda b, i: (b * MOBA_NBLK + i, 0)),
        out_shape=jax.ShapeDtypeStruct((M_TOKENS, D_MODEL), BF16),
        scratch_shapes=[pltpu.VMEM((HEADS, HEAD_DIM, MOBA_BLOCK), F32),
                        pltpu.VMEM((HEADS, MOBA_BLOCK, MOBA_BLOCK), F32)],
        compiler_params=_params(("arbitrary", "arbitrary")),
        name="moba_att",
    )(q_scaled, q_scaled, k_rot, v_t, bias)


HGRN_TILE = MOBA_BLOCK
PM_LEVELS = (1, 2)


def _hgrn_constants():
    c = HGRN_CHUNK
    idx = np.arange(c)
    tri = (idx[None, :] <= idx[:, None]).astype(np.float32)
    rows = [tri]
    masks = []
    for lvl in range(HGRN_LEVELS):
        half = 1 << lvl
        group = idx // (2 * half)
        upper = (idx & half) != 0
        if lvl in PM_LEVELS:
            boundary = tri[group * (2 * half) + half - 1]
            rows.append(np.where(upper[:, None], tri - boundary, boundary - tri))
        masks.append(((group[:, None] == group[None, :]) & upper[:, None] & ~upper[None, :])
                     .astype(np.float32))
    masks.append(np.eye(c, dtype=np.float32))
    return np.concatenate(rows, axis=0), np.stack(masks)


def _hgrn_tile(q_ref, lf_ref, k_ref, v_ref, og_ref, nw_ref, pm_ref, mask_ref, o_ref, state_ref,
               fill):
    c = HGRN_CHUNK
    n_ch = HGRN_TILE // c
    units = [(ch, h) for ch in range(n_ch) for h in range(HEADS)]
    rows = lambda ch: slice(ch * c, (ch + 1) * c)
    lanes = lambda h: slice(h * HEAD_DIM, (h + 1) * HEAD_DIM)
    chunks_on_lanes = lambda t: jnp.concatenate([t[rows(ch), :] for ch in range(n_ch)], axis=1)

    lf = lf_ref[...]
    hi = lf.astype(BF16)
    rem = lf - hi.astype(F32)
    mid = rem.astype(BF16)
    lo = (rem - mid.astype(F32)).astype(BF16)
    split = jnp.concatenate([chunks_on_lanes(hi), chunks_on_lanes(mid), chunks_on_lanes(lo)], axis=0)
    sums = jnp.dot(pm_ref[...], split, preferred_element_type=F32)

    row = lax.broadcasted_iota(jnp.int32, (c, HEAD_DIM), 0)
    upper = [(row & (1 << lvl)) != 0 for lvl in range(HGRN_LEVELS)]

    def decayed_operands(u, ch, h):
        col = slice(u * HEAD_DIM, (u + 1) * HEAD_DIM)
        g = sums[0:c, col]
        q_bf, k_bf = q_ref[rows(ch), lanes(h)], k_ref[rows(ch), lanes(h)]
        qf, kk = q_bf.astype(F32), k_bf.astype(F32)
        zs = []
        for lvl in range(HGRN_LEVELS):
            half = 1 << lvl
            if lvl == 0:
                e = jnp.where(upper[0], lf_ref[rows(ch), lanes(h)], 0.0)
            elif lvl in PM_LEVELS:
                i = PM_LEVELS.index(lvl)
                e = sums[(i + 1) * c:(i + 2) * c, col]
            else:
                boundary = jnp.concatenate(
                    [jnp.broadcast_to(g[b + half - 1:b + half, :], (2 * half, HEAD_DIM))
                     for b in range(0, c, 2 * half)], axis=0)
                e = -jnp.abs(g - boundary)
            zs.append((jnp.where(upper[lvl], qf, kk) * jnp.exp2(e)).astype(BF16))
        g_last = g[c - 1:c, :]
        return dict(q=q_bf, k=k_bf, zs=zs,
                    q_in=(qf * jnp.exp2(g)).astype(BF16),
                    k_out=(kk * jnp.exp2(g_last - g)).astype(BF16),
                    decay=jnp.exp2(g_last),
                    v=v_ref[rows(ch), lanes(h)])

    def state_free_matmuls(op):
        score = lambda lhs, rhs: lax.dot_general(lhs, rhs, NT_DIMS,
                                                 preferred_element_type=F32).astype(BF16)
        a = score(op["q"], op["k"]) * mask_ref[HGRN_LEVELS]
        for lvl, z in enumerate(op["zs"]):
            a = a + score(z, z) * mask_ref[lvl]
        op["o"] = jnp.dot(a, op["v"], preferred_element_type=F32)
        op["update"] = lax.dot_general(op["v"], op["k_out"], TN_DIMS, preferred_element_type=F32)

    ops = {}
    for u, (ch, h) in enumerate(units):
        ops[ch, h] = decayed_operands(u, ch, h)
        if u >= 1:
            state_free_matmuls(ops[units[u - 1]])
        fill()
    state_free_matmuls(ops[units[-1]])

    for h in range(HEADS):
        state_t = state_ref[h]
        for ch in range(n_ch):
            op = ops[ch, h]
            op["state"] = state_t.astype(BF16)
            state_t = state_t * op["decay"] + op["update"]
        state_ref[h] = state_t
    for ch, h in units:
        op = ops[ch, h]
        o = op["o"] + lax.dot_general(op["q_in"], op["state"], NT_DIMS, preferred_element_type=F32)
        ms = jnp.mean(o * o, axis=-1, keepdims=True)
        y = o * lax.rsqrt(ms + RMS_EPS) * nw_ref[:, lanes(h)] * og_ref[rows(ch), lanes(h)]
        o_ref[rows(ch), lanes(h)] = y.astype(o_ref.dtype)


MIX_TM = 1024


def _mix_kernel(ya_ref, yb_ref, gate_ref, x_ref, wa_ref, wb_ref, wo_ref, lnw_ref, lnb_ref,
                o_ref, ob_ref):
    half = MIX_TM // 2
    first, second = slice(0, half), slice(half, MIX_TM)

    def merged(rows, za):
        zb = jnp.dot(yb_ref[rows, :], wb_ref[...], preferred_element_type=F32)
        return (gate_ref[rows, :D_MODEL] * za + gate_ref[rows, D_MODEL:] * zb).astype(BF16)

    def pre_norm(rows, za):
        mixed = jnp.dot(merged(rows, za), wo_ref[...], preferred_element_type=F32)
        return DN_ALPHA * x_ref[rows, :] + mixed

    def finish(rows, r):
        y = _layer_norm(r, lnw_ref[...], lnb_ref[...])
        o_ref[rows, :] = y
        ob_ref[rows, :] = y.astype(BF16)

    branch_a = lambda rows: jnp.dot(ya_ref[rows, :], wa_ref[...], preferred_element_type=F32)
    r_first = pre_norm(first, branch_a(first))
    za_second = branch_a(second)
    finish(first, r_first)
    finish(second, pre_norm(second, za_second))


def _mix(ya, yb, gates, x, wa, wb, wo, lnw, lnb):
    d = D_MODEL
    row = lambda n: pl.BlockSpec((MIX_TM, n), lambda i: (i, 0))
    return pl.pallas_call(
        _mix_kernel,
        grid=(M_TOKENS // MIX_TM,),
        in_specs=[row(d), row(d), row(2 * d), row(d),
                  _resident((d, d)), _resident((d, d)), _resident((d, d)),
                  _resident((1, d)), _resident((1, d))],
        out_specs=[row(d), row(d)],
        out_shape=[jax.ShapeDtypeStruct((M_TOKENS, d), F32),
                   jax.ShapeDtypeStruct((M_TOKENS, d), BF16)],
        compiler_params=_params(("parallel",)),
        name="mix",
    )(ya, yb, gates, x, wa, wb, wo, lnw, lnb)


FFN_TM = 1024
FFN_CHUNK = 1408


def _ffn_kernel(xb_ref, x_ref, win_ref, wd_ref, lnw_ref, lnb_ref, o_ref):
    half = FFN_TM // 2
    n_chunks = D_FF // FFN_CHUNK

    def chunk(rows, c):
        cs = slice(c * FFN_CHUNK, (c + 1) * FFN_CHUNK)
        xb = xb_ref[rows, :]
        up = slice(D_FF + c * FFN_CHUNK, D_FF + (c + 1) * FFN_CHUNK)
        hg = jnp.dot(xb, win_ref[:, cs], preferred_element_type=F32)
        hu = jnp.dot(xb, win_ref[:, up], preferred_element_type=F32)
        act = (hg * _sigmoid(hg) * hu).astype(BF16)
        return jnp.dot(act, wd_ref[cs, :], preferred_element_type=F32)

    def finish(rows, y):
        o_ref[rows, :] = _layer_norm(DN_ALPHA * x_ref[rows, :] + y, lnw_ref[...], lnb_ref[...])

    first, second = slice(0, half), slice(half, FFN_TM)
    y_first = sum(chunk(first, c) for c in range(n_chunks))
    y_second = chunk(second, 0)
    finish(first, y_first)
    for c in range(1, n_chunks):
        y_second = y_second + chunk(second, c)
    finish(second, y_second)


def _ffn(xb, x, w_in, wd, lnw, lnb):
    d = D_MODEL
    row = lambda n: pl.BlockSpec((FFN_TM, n), lambda i: (i, 0))
    return pl.pallas_call(
        _ffn_kernel,
        grid=(M_TOKENS // FFN_TM,),
        in_specs=[row(d), row(d), _resident((d, 2 * D_FF)),
                  _resident((D_FF, d)), _resident((1, d)), _resident((1, d))],
        out_specs=row(d),
        out_shape=jax.ShapeDtypeStruct((M_TOKENS, d), F32),
        compiler_params=_params(("parallel",)),
        name="ffn",
    )(xb, x, w_in, wd, lnw, lnb)


def _rope_frequencies():
    half = HEAD_DIM // 2
    inv_freq = ROPE_THETA ** (-jnp.arange(half, dtype=F32) / half)
    return jnp.concatenate([inv_freq, inv_freq]).reshape(1, HEAD_DIM)


def _layer(x2, w_in, lb_logits, hgrn_norm_w, w_branch_a, w_branch_b, b_gate, w_out,
           ln1_w, ln1_b, w_ffn_in, w_ffn_down, ln2_w, ln2_b):
    d = D_MODEL
    row2 = lambda v: v.reshape(1, -1).astype(F32)

    q_scaled, bias, k_rot, v_t, gates, y_a = _proj(
        x2, w_in.astype(BF16), _rope_frequencies(), row2(b_gate), lb_logits.astype(F32),
        row2(hgrn_norm_w))
    y_b = _moba_attention(q_scaled, k_rot, v_t, bias)

    x1, x1b = _mix(y_a, y_b, gates, x2, w_branch_a.astype(BF16), w_branch_b.astype(BF16),
                   w_out.astype(BF16), row2(ln1_w), row2(ln1_b))
    return _ffn(x1b, x1, w_ffn_in.astype(BF16), w_ffn_down.astype(BF16), row2(ln2_w), row2(ln2_b))


def kernel(x, w_in, lb_logits, hgrn_norm_w, w_branch_a, w_branch_b, b_gate, w_out, ln1_w, ln1_b,
           w_ffn_in, w_ffn_down, ln2_w, ln2_b):
    assert DEPTH == 1
    h = x.reshape(M_TOKENS, D_MODEL)
    h = _layer(h, w_in[0], lb_logits, hgrn_norm_w[0], w_branch_a[0], w_branch_b[0],
               b_gate[0], w_out[0], ln1_w[0], ln1_b[0], w_ffn_in[0], w_ffn_down[0],
               ln2_w[0], ln2_b[0])
    return h.reshape(BATCH, SEQ, D_MODEL)
```

```python
import numpy as np
import jax
import jax.numpy as jnp
from jax import lax
from jax.experimental import pallas as pl
from jax.experimental.pallas import tpu as pltpu

D_MODEL = 1024
BATCH = 4
SEQ = 4096
DEPTH = 1
HEADS = 8
HEAD_DIM = 128
HGRN_CHUNK = 64
HGRN_LEVELS = 6
MOBA_BLOCK = 256
MOBA_NBLK = SEQ // MOBA_BLOCK
MOBA_TOPK = 3
ROPE_THETA = 10000.0
D_FF = 2816
DN_ALPHA = (2.0 * DEPTH) ** 0.25
LN_EPS = 1e-5
RMS_EPS = 1e-6
M_TOKENS = BATCH * SEQ

MASK_VALUE = -1e30
VT_ROWS = HEAD_DIM + 16
LOG2_E = 1.4426950408889634
VMEM_LIMIT = 56 * 1024 * 1024

F32 = jnp.float32
BF16 = jnp.bfloat16
NT_DIMS = (((1,), (1,)), ((), ()))
TN_DIMS = (((0,), (0,)), ((), ()))


def _params(semantics):
    return pltpu.CompilerParams(dimension_semantics=semantics, vmem_limit_bytes=VMEM_LIMIT)


def _resident(shape):
    return pl.BlockSpec(shape, lambda *_: (0,) * len(shape), pipeline_mode=pl.Buffered(1))


def _sigmoid(z):
    return 1.0 / (1.0 + jnp.exp(-z))


def _layer_norm(r, w, b):
    mu = jnp.mean(r, axis=-1, keepdims=True)
    d = r - mu
    var = jnp.mean(d * d, axis=-1, keepdims=True)
    return d * lax.rsqrt(var + LN_EPS) * w + b


PROJ_SLAB = 256
N_PROJ = 9 * D_MODEL
COL_HGRN_Q, COL_HGRN_F, COL_HGRN_V, COL_HGRN_GATE = 0, D_MODEL, 2 * D_MODEL, 3 * D_MODEL
COL_MOBA_Q, COL_MOBA_K, COL_MOBA_V, COL_MERGE_GATE = 4 * D_MODEL, 5 * D_MODEL, 6 * D_MODEL, 7 * D_MODEL


def _proj_kernel(x_ref, w_ref, freq_ref,
                 bg_ref, lbl_ref, nw_ref, pm_ref, mask_ref,
                 q_ref, bias_ref, k_ref, vt_ref, gate_ref, ya_ref,
                 km_ref, state_ref, hq_ref, hog_ref, hlf_ref, hk_ref, hv_ref, xb_ref, qf_ref,
                 cos_all_ref, sin_all_ref):
    own = pl.program_id(1)
    lanes = HEADS * MOBA_NBLK

    @pl.when(own == 0)
    def _():
        km_ref[...] = jnp.zeros_like(km_ref)
        state_ref[...] = jnp.zeros_like(state_ref)

    xb_ref[...] = x_ref[...].astype(BF16)
    block_rows = pl.ds(pl.multiple_of(own * MOBA_BLOCK, MOBA_BLOCK), MOBA_BLOCK)

    @pl.when(pl.program_id(0) == 0)
    def _():
        position = own * MOBA_BLOCK + lax.broadcasted_iota(jnp.int32, (MOBA_BLOCK, HEAD_DIM), 0)
        angle = position.astype(F32) * freq_ref[...]
        first_half = lax.broadcasted_iota(jnp.int32, (MOBA_BLOCK, HEAD_DIM), 1) < HEAD_DIM // 2
        cos_all_ref[block_rows, :] = jnp.cos(angle)
        sin_all_ref[block_rows, :] = jnp.where(first_half, -jnp.sin(angle), jnp.sin(angle))

    cos_ref = cos_all_ref.at[block_rows, :]
    sin_ref = sin_all_ref.at[block_rows, :]
    head = lambda h: slice(h * HEAD_DIM, (h + 1) * HEAD_DIM)
    slab = lambda i: slice(i * PROJ_SLAB, (i + 1) * PROJ_SLAB)
    n_slabs = D_MODEL // PROJ_SLAB

    def project(first_col, i):
        cols = slice(first_col + i * PROJ_SLAB, first_col + (i + 1) * PROJ_SLAB)
        return jnp.dot(xb_ref[...], w_ref[:, cols], preferred_element_type=F32)

    rope = lambda t: t * cos_ref[...] + pltpu.roll(t, HEAD_DIM // 2, 1) * sin_ref[...]


    def hgrn_silu(i):
        def run():
            acc = project(COL_HGRN_Q if i < n_slabs else COL_HGRN_GATE, i % n_slabs)
            dst = hq_ref if i < n_slabs else hog_ref
            dst[:, slab(i % n_slabs)] = (acc * _sigmoid(acc)).astype(dst.dtype)
        return run

    def hgrn_forget(i):
        def run():
            l0, l1 = lbl_ref[0:1, slab(i)], lbl_ref[1:2, slab(i)]
            top = jnp.maximum(l0, l1)
            e0, e1 = jnp.exp(l0 - top), jnp.exp(l1 - top)
            lb = e0 / (e0 + e1)
            z = project(COL_HGRN_F, i)
            hlf_ref[:, slab(i)] = jnp.log(lb + (1.0 - lb) * _sigmoid(z)) * LOG2_E
            hk_ref[:, slab(i)] = ((1.0 - lb) * _sigmoid(-z)).astype(hk_ref.dtype)
        return run

    def hgrn_value(i):
        def run():
            hv_ref[:, slab(i)] = project(COL_HGRN_V, i).astype(hv_ref.dtype)
        return run

    heads_of = lambda i: range(i * PROJ_SLAB // HEAD_DIM, (i + 1) * PROJ_SLAB // HEAD_DIM)
    in_slab = lambda i, h: slice((h - heads_of(i)[0]) * HEAD_DIM, (h - heads_of(i)[0] + 1) * HEAD_DIM)

    def moba_query(i):
        def run():
            acc = project(COL_MOBA_Q, i)
            for h in heads_of(i):
                qh = rope(acc[:, in_slab(i, h)])
                qf_ref[:, head(h)] = qh
                q_ref[:, head(h)] = (qh * (HEAD_DIM ** -0.5 * LOG2_E)).astype(q_ref.dtype)
        return run

    def moba_choice():
        split = lambda t: (t.astype(BF16), (t - t.astype(BF16).astype(F32)).astype(BF16))
        q_hi, q_lo = split(qf_ref[...])
        km_hi, km_lo = split(km_ref[...])
        gate = lax.dot_general(jnp.concatenate([q_hi, q_lo, q_hi], axis=1),
                               jnp.concatenate([km_hi, km_hi, km_lo], axis=1), NT_DIMS,
                               preferred_element_type=F32)
        lane = lax.broadcasted_iota(jnp.int32, (MOBA_BLOCK, lanes), 1)
        eligible = lane < own * HEADS
        gate = jnp.where(eligible, gate, -jnp.inf)
        rank = jnp.zeros((MOBA_BLOCK, lanes), F32)
        for r in range(1, MOBA_NBLK):
            partner = pltpu.roll(gate, lanes - r * HEADS, 1)
            wrapped = lane >= lanes - r * HEADS
            beats = (partner > gate) | (wrapped & (partner == gate))
            rank = rank + beats.astype(F32)
        selected = eligible & (rank < float(MOBA_TOPK))
        bias_ref[0] = jnp.where(selected, 0.0, MASK_VALUE).T

    def moba_key(i):
        def run():
            acc = project(COL_MOBA_K, i)
            for h in heads_of(i):
                rot = rope(acc[:, in_slab(i, h)])
                k_ref[:, head(h)] = rot.astype(k_ref.dtype)
                mean = jnp.mean(rot, axis=0, keepdims=True)
                zeros = lambda n: [jnp.zeros((1, n * HEAD_DIM), F32)] if n else []
                km_ref[pl.ds(own * HEADS + h, 1), :] = jnp.concatenate(
                    zeros(h) + [mean] + zeros(HEADS - 1 - h), axis=1)
        return run

    def moba_value(i):
        def run():
            vt = project(COL_MOBA_V, i).T.astype(vt_ref.dtype)
            pad_row = lax.broadcasted_iota(jnp.int32, (VT_ROWS - HEAD_DIM, MOBA_BLOCK), 0)
            ones_row = jnp.where(pad_row == 0, 1.0, 0.0).astype(vt_ref.dtype)
            for h in heads_of(i):
                vt_ref[0, 0, h * VT_ROWS:h * VT_ROWS + HEAD_DIM, :] = vt[in_slab(i, h), :]
                vt_ref[0, 0, h * VT_ROWS + HEAD_DIM:(h + 1) * VT_ROWS, :] = ones_row
        return run

    def merge_gate(i):
        def run():
            acc = project(COL_MERGE_GATE, i) + bg_ref[:, slab(i)]
            gate_ref[:, slab(i)] = _sigmoid(acc).astype(gate_ref.dtype)
        return run

    hgrn_inputs = ([hgrn_silu(i) for i in range(2 * n_slabs)]
                   + [hgrn_forget(i) for i in range(n_slabs)]
                   + [hgrn_value(i) for i in range(n_slabs)])
    queries = [moba_query(i) for i in range(n_slabs)] + [moba_choice]
    for i, piece in enumerate(hgrn_inputs):
        piece()
        if i < len(queries):
            queries[i]()

    rest = ([moba_key(i) for i in range(n_slabs)] + [moba_value(i) for i in range(n_slabs)]
            + [merge_gate(i) for i in range(2 * n_slabs)])
    calls = []

    def fill():
        calls.append(None)
        if rest and len(calls) % 2 == 0:
            rest.pop(0)()

    _hgrn_tile(hq_ref, hlf_ref, hk_ref, hv_ref, hog_ref, nw_ref, pm_ref, mask_ref, ya_ref,
               state_ref, fill)
    while rest:
        fill()


def _proj(x, w_in, rope_freq, b_gate, lb_logits, norm_w):
    d = D_MODEL
    lanes = HEADS * MOBA_NBLK
    pm, masks = _hgrn_constants()
    pm = np.tile(pm, (1, 3))
    row = lambda n: pl.BlockSpec((MOBA_BLOCK, n), lambda b, i: (b * MOBA_NBLK + i, 0))
    shape = lambda n, dt: jax.ShapeDtypeStruct((M_TOKENS, n), dt)
    tile = lambda dt: pltpu.VMEM((MOBA_BLOCK, d), dt)
    return pl.pallas_call(
        _proj_kernel,
        grid=(BATCH, MOBA_NBLK),
        in_specs=[row(d), _resident((d, N_PROJ)), _resident((1, HEAD_DIM)), _resident((1, 2 * d)), _resident((DEPTH + 1, d)),
                  _resident((1, d)), _resident(pm.shape), _resident(masks.shape)],
        out_specs=[row(d),
                   pl.BlockSpec((1, lanes, MOBA_BLOCK), lambda b, i: (b, 0, i)),
                   row(d),
                   pl.BlockSpec((1, 1, HEADS * VT_ROWS, MOBA_BLOCK), lambda b, i: (b, i, 0, 0)),
                   row(2 * d), row(d)],
        out_shape=[shape(d, BF16),
                   jax.ShapeDtypeStruct((BATCH, lanes, SEQ), F32),
                   shape(d, BF16),
                   jax.ShapeDtypeStruct((BATCH, MOBA_NBLK, HEADS * VT_ROWS, MOBA_BLOCK), BF16),
                   shape(2 * d, BF16), shape(d, BF16)],
        scratch_shapes=[pltpu.VMEM((lanes, d), F32),
                        pltpu.VMEM((HEADS, HEAD_DIM, HEAD_DIM), F32),
                        tile(BF16), tile(BF16), tile(F32), tile(BF16), tile(BF16),
                        tile(BF16), tile(F32),
                        pltpu.VMEM((SEQ, HEAD_DIM), F32), pltpu.VMEM((SEQ, HEAD_DIM), F32)],
        compiler_params=_params(("arbitrary", "arbitrary")),
        name="proj",
    )(x, w_in, rope_freq, b_gate, lb_logits, norm_w,
      jnp.asarray(pm, BF16), jnp.asarray(masks, BF16))


def _moba_att_kernel(q_ref, qn_ref, k_ref, vt_ref, bias_ref, o_ref, acc_ref, s_ref):
    own = pl.program_id(1)

    def head(h):
        return slice(h * HEAD_DIM, (h + 1) * HEAD_DIM)

    def scores(h, j):
        kj = k_ref[pl.ds(pl.multiple_of(j * MOBA_BLOCK, MOBA_BLOCK), MOBA_BLOCK), head(h)]
        return lax.dot_general(kj, q_ref[:, head(h)], NT_DIMS, preferred_element_type=F32)

    def item(h, j, s, m, l, next_scores, bias=None):
        top = jnp.max(s, axis=0, keepdims=True)
        m_new = jnp.maximum(m, top if bias is None else top + bias)
        a = jnp.exp2(m - m_new)
        p = jnp.exp2(s - (m_new if bias is None else m_new - 2.0 * bias))
        if next_scores is not None:
            s_ref[h] = next_scores()
        pv = jnp.dot(vt_ref[0, j, h * VT_ROWS:(h + 1) * VT_ROWS, :], p.astype(BF16),
                     preferred_element_type=F32)
        acc_ref[h] = a * acc_ref[h] + pv[:HEAD_DIM]
        return m_new, a * l + pv[HEAD_DIM:HEAD_DIM + 1]

    acc_ref[...] = jnp.zeros_like(acc_ref)

    @pl.when(own == 0)
    def _():
        for h in range(HEADS):
            s_ref[h] = scores(h, 0)

    m0 = jnp.full((1, MOBA_BLOCK), MASK_VALUE, F32)
    l0 = jnp.zeros((1, MOBA_BLOCK), F32)

    def body(j, carry):
        ms, ls = list(carry[0]), list(carry[1])
        for h in range(HEADS):
            ms[h], ls[h] = item(h, j, s_ref[h], ms[h], ls[h], lambda h=h: scores(h, j + 1),
                                bias_ref[0, pl.ds(j * HEADS + h, 1), :])
        return tuple(ms), tuple(ls)

    def several(n):
        def run(t, carry):
            for r in range(n):
                carry = body(n * t + r, carry)
            return carry
        return run

    carry = ((m0,) * HEADS, (l0,) * HEADS)
    done = 0
    for n in (4, 2, 1):
        trips = (own - done) // n
        carry = lax.fori_loop(done // n, done // n + trips, several(n), carry)
        done = done + trips * n
    ms, ls = carry

    key_pos = lax.broadcasted_iota(jnp.int32, (MOBA_BLOCK, MOBA_BLOCK), 0)
    qry_pos = lax.broadcasted_iota(jnp.int32, (MOBA_BLOCK, MOBA_BLOCK), 1)
    causal = key_pos <= qry_pos
    def next_block_scores(h):
        return lax.dot_general(k_ref[0:MOBA_BLOCK, head(h)], qn_ref[:, head(h)], NT_DIMS,
                               preferred_element_type=F32)

    for h in range(HEADS):
        s = jnp.where(causal, s_ref[h], MASK_VALUE)
        _, l = item(h, own, s, ms[h], ls[h], lambda h=h: next_block_scores(h))
        o_ref[:, head(h)] = (acc_ref[h] * (1.0 / l)).T.astype(o_ref.dtype)


def _moba_attention(q_scaled, k_rot, v_t, bias):
    lanes = HEADS * MOBA_NBLK
    return pl.pallas_call(
        _moba_att_kernel,
        grid=(BATCH, MOBA_NBLK),
        in_specs=[pl.BlockSpec((MOBA_BLOCK, D_MODEL), lambda b, i: (b * MOBA_NBLK + i, 0)),
                  pl.BlockSpec((MOBA_BLOCK, D_MODEL),
                               lambda b, i: (b * MOBA_NBLK + jnp.minimum(i + 1, MOBA_NBLK - 1), 0)),
                  pl.BlockSpec((SEQ, D_MODEL), lambda b, i: (b, 0)),
                  pl.BlockSpec((1, MOBA_NBLK, HEADS * VT_ROWS, MOBA_BLOCK),
                               lambda b, i: (b, 0, 0, 0)),
                  pl.BlockSpec((1, lanes, MOBA_BLOCK), lambda b, i: (b, 0, i))],
        out_specs=pl.BlockSpec((MOBA_BLOCK, D_MODEL), lambda b, i: (b * MOBA_NBLK + i, 0)),
        out_shape=jax.ShapeDtypeStruct((M_TOKENS, D_MODEL), BF16),
        scratch_shapes=[pltpu.VMEM((HEADS, HEAD_DIM, MOBA_BLOCK), F32),
                        pltpu.VMEM((HEADS, MOBA_BLOCK, MOBA_BLOCK), F32)],
        compiler_params=_params(("arbitrary", "arbitrary")),
        name="moba_att",
    )(q_scaled, q_scaled, k_rot, v_t, bias)


HGRN_TILE = MOBA_BLOCK
PM_LEVELS = (1, 2)


def _hgrn_constants():
    c = HGRN_CHUNK
    idx = np.arange(c)
    tri = (idx[None, :] <= idx[:, None]).astype(np.float32)
    rows = [tri]
    masks = []
    for lvl in range(HGRN_LEVELS):
        half = 1 << lvl
        group = idx // (2 * half)
        upper = (idx & half) != 0
        if lvl in PM_LEVELS:
            boundary = tri[group * (2 * half) + half - 1]
            rows.append(np.where(upper[:, None], tri - boundary, boundary - tri))
        masks.append(((group[:, None] == group[None, :]) & upper[:, None] & ~upper[None, :])
                     .astype(np.float32))
    masks.append(np.eye(c, dtype=np.float32))
    return np.concatenate(rows, axis=0), np.stack(masks)


def _hgrn_tile(q_ref, lf_ref, k_ref, v_ref, og_ref, nw_ref, pm_ref, mask_ref, o_ref, state_ref,
               fill):
    c = HGRN_CHUNK
    n_ch = HGRN_TILE // c
    units = [(ch, h) for ch in range(n_ch) for h in range(HEADS)]
    rows = lambda ch: slice(ch * c, (ch + 1) * c)
    lanes = lambda h: slice(h * HEAD_DIM, (h + 1) * HEAD_DIM)
    chunks_on_lanes = lambda t: jnp.concatenate([t[rows(ch), :] for ch in range(n_ch)], axis=1)

    lf = lf_ref[...]
    hi = lf.astype(BF16)
    rem = lf - hi.astype(F32)
    mid = rem.astype(BF16)
    lo = (rem - mid.astype(F32)).astype(BF16)
    split = jnp.concatenate([chunks_on_lanes(hi), chunks_on_lanes(mid), chunks_on_lanes(lo)], axis=0)
    sums = jnp.dot(pm_ref[...], split, preferred_element_type=F32)

    row = lax.broadcasted_iota(jnp.int32, (c, HEAD_DIM), 0)
    upper = [(row & (1 << lvl)) != 0 for lvl in range(HGRN_LEVELS)]
    t_idx = lax.broadcasted_iota(jnp.int32, (c, c), 0)
    s_idx = lax.broadcasted_iota(jnp.int32, (c, c), 1)
    on_diagonal = (t_idx == s_idx).astype(F32)
    below_diagonal = ((t_idx == s_idx + 1) & ((t_idx & 1) != 0)).astype(F32)

    def decayed_operands(u, ch, h):
        col = slice(u * HEAD_DIM, (u + 1) * HEAD_DIM)
        g = sums[0:c, col]
        q_bf, k_bf = q_ref[rows(ch), lanes(h)], k_ref[rows(ch), lanes(h)]
        qf, kk = q_bf.astype(F32), k_bf.astype(F32)
        same = jnp.sum(qf * kk, axis=-1, keepdims=True)
        q_dec = jnp.where(upper[0], qf * jnp.exp2(lf_ref[rows(ch), lanes(h)]), 0.0)
        prev = jnp.sum(q_dec * pltpu.roll(kk, 1, 0), axis=-1, keepdims=True)
        near = (same * on_diagonal + prev * below_diagonal).astype(BF16)
        zs = []
        for lvl in range(1, HGRN_LEVELS):
            half = 1 << lvl
            if lvl in PM_LEVELS:
                i = PM_LEVELS.index(lvl)
                e = sums[(i + 1) * c:(i + 2) * c, col]
            else:
                boundary = jnp.concatenate(
                    [jnp.broadcast_to(g[b + half - 1:b + half, :], (2 * half, HEAD_DIM))
                     for b in range(0, c, 2 * half)], axis=0)
                e = -jnp.abs(g - boundary)
            zs.append((jnp.where(upper[lvl], qf, kk) * jnp.exp2(e)).astype(BF16))
        g_last = g[c - 1:c, :]
        return dict(near=near, zs=zs,
                    q_in=(qf * jnp.exp2(g)).astype(BF16),
                    k_out=(kk * jnp.exp2(g_last - g)).astype(BF16),
                    decay=jnp.exp2(g_last),
                    v=v_ref[rows(ch), lanes(h)])

    def state_free_matmuls(op):
        score = lambda lhs, rhs: lax.dot_general(lhs, rhs, NT_DIMS,
                                                 preferred_element_type=F32).astype(BF16)
        a = op["near"]
        for lvl, z in enumerate(op["zs"], start=1):
            a = a + score(z, z) * mask_ref[lvl]
        op["o"] = jnp.dot(a, op["v"], preferred_element_type=F32)
        op["update"] = lax.dot_general(op["v"], op["k_out"], TN_DIMS, preferred_element_type=F32)

    ops = {}
    for u, (ch, h) in enumerate(units):
        ops[ch, h] = decayed_operands(u, ch, h)
        if u >= 1:
            state_free_matmuls(ops[units[u - 1]])
        fill()
    state_free_matmuls(ops[units[-1]])

    for h in range(HEADS):
        state_t = state_ref[h]
        for ch in range(n_ch):
            op = ops[ch, h]
            op["state"] = state_t.astype(BF16)
            state_t = state_t * op["decay"] + op["update"]
        state_ref[h] = state_t
    for ch, h in units:
        op = ops[ch, h]
        o = op["o"] + lax.dot_general(op["q_in"], op["state"], NT_DIMS, preferred_element_type=F32)
        ms = jnp.mean(o * o, axis=-1, keepdims=True)
        y = o * lax.rsqrt(ms + RMS_EPS) * nw_ref[:, lanes(h)] * og_ref[rows(ch), lanes(h)]
        o_ref[rows(ch), lanes(h)] = y.astype(o_ref.dtype)


MIX_TM = 512


def _mix_kernel(ya_ref, yb_ref, gate_ref, x_ref, wa_ref, wb_ref, wo_ref, lnw_ref, lnb_ref,
                o_ref, ob_ref):
    half = MIX_TM // 2
    first, second = slice(0, half), slice(half, MIX_TM)

    def merged(rows, za):
        zb = jnp.dot(yb_ref[rows, :], wb_ref[...], preferred_element_type=F32)
        return (gate_ref[rows, :D_MODEL] * za + gate_ref[rows, D_MODEL:] * zb).astype(BF16)

    def pre_norm(rows, za):
        mixed = jnp.dot(merged(rows, za), wo_ref[...], preferred_element_type=F32)
        return DN_ALPHA * x_ref[rows, :] + mixed

    def finish(rows, r):
        y = _layer_norm(r, lnw_ref[...], lnb_ref[...])
        o_ref[rows, :] = y
        ob_ref[rows, :] = y.astype(BF16)

    branch_a = lambda rows: jnp.dot(ya_ref[rows, :], wa_ref[...], preferred_element_type=F32)
    r_first = pre_norm(first, branch_a(first))
    za_second = branch_a(second)
    finish(first, r_first)
    finish(second, pre_norm(second, za_second))


def _mix(ya, yb, gates, x, wa, wb, wo, lnw, lnb):
    d = D_MODEL
    row = lambda n: pl.BlockSpec((MIX_TM, n), lambda i: (i, 0))
    return pl.pallas_call(
        _mix_kernel,
        grid=(M_TOKENS // MIX_TM,),
        in_specs=[row(d), row(d), row(2 * d), row(d),
                  _resident((d, d)), _resident((d, d)), _resident((d, d)),
                  _resident((1, d)), _resident((1, d))],
        out_specs=[row(d), row(d)],
        out_shape=[jax.ShapeDtypeStruct((M_TOKENS, d), F32),
                   jax.ShapeDtypeStruct((M_TOKENS, d), BF16)],
        compiler_params=_params(("parallel",)),
        name="mix",
    )(ya, yb, gates, x, wa, wb, wo, lnw, lnb)


FFN_TM = 512
FFN_CHUNK = 1408


def _ffn_kernel(xb_ref, x_ref, win_ref, wd_ref, lnw_ref, lnb_ref, o_ref):
    half = FFN_TM // 2
    n_chunks = D_FF // FFN_CHUNK

    def chunk(rows, c):
        cs = slice(c * FFN_CHUNK, (c + 1) * FFN_CHUNK)
        xb = xb_ref[rows, :]
        up = slice(D_FF + c * FFN_CHUNK, D_FF + (c + 1) * FFN_CHUNK)
        hg = jnp.dot(xb, win_ref[:, cs], preferred_element_type=F32)
        hu = jnp.dot(xb, win_ref[:, up], preferred_element_type=F32)
        act = (hg * _sigmoid(hg) * hu).astype(BF16)
        return jnp.dot(act, wd_ref[cs, :], preferred_element_type=F32)

    def finish(rows, y):
        o_ref[rows, :] = _layer_norm(DN_ALPHA * x_ref[rows, :] + y, lnw_ref[...], lnb_ref[...])

    first, second = slice(0, half), slice(half, FFN_TM)
    y_first = sum(chunk(first, c) for c in range(n_chunks))
    y_second = chunk(second, 0)
    finish(first, y_first)
    for c in range(1, n_chunks):
        y_second = y_second + chunk(second, c)
    finish(second, y_second)


def _ffn(xb, x, w_in, wd, lnw, lnb):
    d = D_MODEL
    row = lambda n: pl.BlockSpec((FFN_TM, n), lambda i: (i, 0))
    return pl.pallas_call(
        _ffn_kernel,
        grid=(M_TOKENS // FFN_TM,),
        in_specs=[row(d), row(d), _resident((d, 2 * D_FF)),
                  _resident((D_FF, d)), _resident((1, d)), _resident((1, d))],
        out_specs=row(d),
        out_shape=jax.ShapeDtypeStruct((M_TOKENS, d), F32),
        compiler_params=_params(("parallel",)),
        name="ffn",
    )(xb, x, w_in, wd, lnw, lnb)


def _rope_frequencies():
    half = HEAD_DIM // 2
    inv_freq = ROPE_THETA ** (-jnp.arange(half, dtype=F32) / half)
    return jnp.concatenate([inv_freq, inv_freq]).reshape(1, HEAD_DIM)


def _layer(x2, w_in, lb_logits, hgrn_norm_w, w_branch_a, w_branch_b, b_gate, w_out,
           ln1_w, ln1_b, w_ffn_in, w_ffn_down, ln2_w, ln2_b):
    d = D_MODEL
    row2 = lambda v: v.reshape(1, -1).astype(F32)

    q_scaled, bias, k_rot, v_t, gates, y_a = _proj(
        x2, w_in.astype(BF16), _rope_frequencies(), row2(b_gate), lb_logits.astype(F32),
        row2(hgrn_norm_w))
    y_b = _moba_attention(q_scaled, k_rot, v_t, bias)

    x1, x1b = _mix(y_a, y_b, gates, x2, w_branch_a.astype(BF16), w_branch_b.astype(BF16),
                   w_out.astype(BF16), row2(ln1_w), row2(ln1_b))
    return _ffn(x1b, x1, w_ffn_in.astype(BF16), w_ffn_down.astype(BF16), row2(ln2_w), row2(ln2_b))


def kernel(x, w_in, lb_logits, hgrn_norm_w, w_branch_a, w_branch_b, b_gate, w_out, ln1_w, ln1_b,
           w_ffn_in, w_ffn_down, ln2_w, ln2_b):
    assert DEPTH == 1
    h = x.reshape(M_TOKENS, D_MODEL)
    h = _layer(h, w_in[0], lb_logits, hgrn_norm_w[0], w_branch_a[0], w_branch_b[0],
               b_gate[0], w_out[0], ln1_w[0], ln1_b[0], w_ffn_in[0], w_ffn_down[0],
               ln2_w[0], ln2_b[0])
    return h.reshape(BATCH, SEQ, D_MODEL)
```

```python
import numpy as np
import jax
import jax.numpy as jnp
from jax import lax
from jax.experimental import pallas as pl
from jax.experimental.pallas import tpu as pltpu

D_MODEL = 1024
BATCH = 4
SEQ = 4096
DEPTH = 1
HEADS = 8
HEAD_DIM = 128
HGRN_CHUNK = 64
HGRN_LEVELS = 6
MOBA_BLOCK = 256
MOBA_NBLK = SEQ // MOBA_BLOCK
MOBA_TOPK = 3
ROPE_THETA = 10000.0
D_FF = 2816
DN_ALPHA = (2.0 * DEPTH) ** 0.25
LN_EPS = 1e-5
RMS_EPS = 1e-6
M_TOKENS = BATCH * SEQ

MASK_VALUE = -1e30
VT_ROWS = HEAD_DIM + 16
LOG2_E = 1.4426950408889634
VMEM_LIMIT = 56 * 1024 * 1024

F32 = jnp.float32
BF16 = jnp.bfloat16
NT_DIMS = (((1,), (1,)), ((), ()))
TN_DIMS = (((0,), (0,)), ((), ()))


def _params(semantics):
    return pltpu.CompilerParams(dimension_semantics=semantics, vmem_limit_bytes=VMEM_LIMIT)


def _resident(shape):
    return pl.BlockSpec(shape, lambda *_: (0,) * len(shape), pipeline_mode=pl.Buffered(1))


def _sigmoid(z):
    return 1.0 / (1.0 + jnp.exp(-z))


def _layer_norm(r, w, b):
    mu = jnp.mean(r, axis=-1, keepdims=True)
    d = r - mu
    var = jnp.mean(d * d, axis=-1, keepdims=True)
    return d * lax.rsqrt(var + LN_EPS) * w + b


PROJ_SLAB = 256
FILL_EVERY = 4
N_PROJ = 9 * D_MODEL
COL_HGRN_Q, COL_HGRN_F, COL_HGRN_V, COL_HGRN_GATE = 0, D_MODEL, 2 * D_MODEL, 3 * D_MODEL
COL_MOBA_Q, COL_MOBA_K, COL_MOBA_V, COL_MERGE_GATE = 4 * D_MODEL, 5 * D_MODEL, 6 * D_MODEL, 7 * D_MODEL


def _proj_kernel(x_ref, w_ref, freq_ref,
                 bg_ref, lbl_ref, nw_ref, pm_ref, mask_ref,
                 q_ref, bias_ref, k_ref, vt_ref, gate_ref, ya_ref,
                 km_ref, state_ref, hq_ref, hog_ref, hlf_ref, hk_ref, hv_ref, xb_ref, qf_ref,
                 cos_all_ref, sin_all_ref):
    own = pl.program_id(1)
    lanes = HEADS * MOBA_NBLK

    @pl.when(own == 0)
    def _():
        km_ref[...] = jnp.zeros_like(km_ref)
        state_ref[...] = jnp.zeros_like(state_ref)

    xb_ref[...] = x_ref[...].astype(BF16)
    block_rows = pl.ds(pl.multiple_of(own * MOBA_BLOCK, MOBA_BLOCK), MOBA_BLOCK)

    @pl.when(pl.program_id(0) == 0)
    def _():
        position = own * MOBA_BLOCK + lax.broadcasted_iota(jnp.int32, (MOBA_BLOCK, HEAD_DIM), 0)
        angle = position.astype(F32) * freq_ref[...]
        first_half = lax.broadcasted_iota(jnp.int32, (MOBA_BLOCK, HEAD_DIM), 1) < HEAD_DIM // 2
        cos_all_ref[block_rows, :] = jnp.cos(angle)
        sin_all_ref[block_rows, :] = jnp.where(first_half, -jnp.sin(angle), jnp.sin(angle))

    cos_ref = cos_all_ref.at[block_rows, :]
    sin_ref = sin_all_ref.at[block_rows, :]
    head = lambda h: slice(h * HEAD_DIM, (h + 1) * HEAD_DIM)
    slab = lambda i: slice(i * PROJ_SLAB, (i + 1) * PROJ_SLAB)
    n_slabs = D_MODEL // PROJ_SLAB

    def project(first_col, i):
        cols = slice(first_col + i * PROJ_SLAB, first_col + (i + 1) * PROJ_SLAB)
        return jnp.dot(xb_ref[...], w_ref[:, cols], preferred_element_type=F32)

    rope = lambda t: t * cos_ref[...] + pltpu.roll(t, HEAD_DIM // 2, 1) * sin_ref[...]


    def hgrn_silu(i):
        def run():
            acc = project(COL_HGRN_Q if i < n_slabs else COL_HGRN_GATE, i % n_slabs)
            dst = hq_ref if i < n_slabs else hog_ref
            dst[:, slab(i % n_slabs)] = (acc * _sigmoid(acc)).astype(dst.dtype)
        return run

    def hgrn_forget(i):
        def run():
            l0, l1 = lbl_ref[0:1, slab(i)], lbl_ref[1:2, slab(i)]
            top = jnp.maximum(l0, l1)
            e0, e1 = jnp.exp(l0 - top), jnp.exp(l1 - top)
            lb = e0 / (e0 + e1)
            z = project(COL_HGRN_F, i)
            hlf_ref[:, slab(i)] = jnp.log(lb + (1.0 - lb) * _sigmoid(z)) * LOG2_E
            hk_ref[:, slab(i)] = ((1.0 - lb) * _sigmoid(-z)).astype(hk_ref.dtype)
        return run

    def hgrn_value(i):
        def run():
            hv_ref[:, slab(i)] = project(COL_HGRN_V, i).astype(hv_ref.dtype)
        return run

    heads_of = lambda i: range(i * PROJ_SLAB // HEAD_DIM, (i + 1) * PROJ_SLAB // HEAD_DIM)
    in_slab = lambda i, h: slice((h - heads_of(i)[0]) * HEAD_DIM, (h - heads_of(i)[0] + 1) * HEAD_DIM)

    def moba_query(i):
        def run():
            acc = project(COL_MOBA_Q, i)
            for h in heads_of(i):
                qh = rope(acc[:, in_slab(i, h)])
                qf_ref[:, head(h)] = qh
                q_ref[:, head(h)] = (qh * (HEAD_DIM ** -0.5 * LOG2_E)).astype(q_ref.dtype)
        return run

    def moba_choice():
        split = lambda t: (t.astype(BF16), (t - t.astype(BF16).astype(F32)).astype(BF16))
        q_hi, q_lo = split(qf_ref[...])
        km_hi, km_lo = split(km_ref[...])
        gate = lax.dot_general(jnp.concatenate([q_hi, q_lo, q_hi], axis=1),
                               jnp.concatenate([km_hi, km_hi, km_lo], axis=1), NT_DIMS,
                               preferred_element_type=F32)
        lane = lax.broadcasted_iota(jnp.int32, (MOBA_BLOCK, lanes), 1)
        eligible = lane < own * HEADS
        gate = jnp.where(eligible, gate, -jnp.inf)
        rank = jnp.zeros((MOBA_BLOCK, lanes), F32)
        for r in range(1, MOBA_NBLK):
            partner = pltpu.roll(gate, lanes - r * HEADS, 1)
            wrapped = lane >= lanes - r * HEADS
            beats = (partner > gate) | (wrapped & (partner == gate))
            rank = rank + beats.astype(F32)
        selected = eligible & (rank < float(MOBA_TOPK))
        bias_ref[0] = jnp.where(selected, 0.0, MASK_VALUE).T

    def moba_key(i):
        def run():
            acc = project(COL_MOBA_K, i)
            for h in heads_of(i):
                rot = rope(acc[:, in_slab(i, h)])
                k_ref[:, head(h)] = rot.astype(k_ref.dtype)
                mean = jnp.mean(rot, axis=0, keepdims=True)
                zeros = lambda n: [jnp.zeros((1, n * HEAD_DIM), F32)] if n else []
                km_ref[pl.ds(own * HEADS + h, 1), :] = jnp.concatenate(
                    zeros(h) + [mean] + zeros(HEADS - 1 - h), axis=1)
        return run

    def moba_value(i):
        def run():
            vt = project(COL_MOBA_V, i).T.astype(vt_ref.dtype)
            pad_row = lax.broadcasted_iota(jnp.int32, (VT_ROWS - HEAD_DIM, MOBA_BLOCK), 0)
            ones_row = jnp.where(pad_row == 0, 1.0, 0.0).astype(vt_ref.dtype)
            for h in heads_of(i):
                vt_ref[0, 0, h * VT_ROWS:h * VT_ROWS + HEAD_DIM, :] = vt[in_slab(i, h), :]
                vt_ref[0, 0, h * VT_ROWS + HEAD_DIM:(h + 1) * VT_ROWS, :] = ones_row
        return run

    def merge_gate(i):
        def run():
            acc = project(COL_MERGE_GATE, i) + bg_ref[:, slab(i)]
            gate_ref[:, slab(i)] = _sigmoid(acc).astype(gate_ref.dtype)
        return run

    hgrn_inputs = ([hgrn_silu(i) for i in range(2 * n_slabs)]
                   + [hgrn_forget(i) for i in range(n_slabs)]
                   + [hgrn_value(i) for i in range(n_slabs)])
    queries = [moba_query(i) for i in range(n_slabs)] + [moba_choice]
    for i, piece in enumerate(hgrn_inputs):
        piece()
        if i < len(queries):
            queries[i]()

    rest = ([moba_key(i) for i in range(n_slabs)] + [moba_value(i) for i in range(n_slabs)]
            + [merge_gate(i) for i in range(2 * n_slabs)])
    calls = []

    def fill():
        calls.append(None)
        if rest and len(calls) % FILL_EVERY == 0:
            rest.pop(0)()

    _hgrn_tile(hq_ref, hlf_ref, hk_ref, hv_ref, hog_ref, nw_ref, pm_ref, mask_ref, ya_ref,
               state_ref, fill)
    while rest:
        fill()


def _proj(x, w_in, rope_freq, b_gate, lb_logits, norm_w):
    d = D_MODEL
    lanes = HEADS * MOBA_NBLK
    pm, masks = _hgrn_constants()
    pm = np.tile(pm, (1, 3))
    row = lambda n: pl.BlockSpec((MOBA_BLOCK, n), lambda b, i: (b * MOBA_NBLK + i, 0))
    shape = lambda n, dt: jax.ShapeDtypeStruct((M_TOKENS, n), dt)
    tile = lambda dt: pltpu.VMEM((MOBA_BLOCK, d), dt)
    return pl.pallas_call(
        _proj_kernel,
        grid=(BATCH, MOBA_NBLK),
        in_specs=[row(d), _resident((d, N_PROJ)), _resident((1, HEAD_DIM)), _resident((1, 2 * d)), _resident((DEPTH + 1, d)),
                  _resident((1, d)), _resident(pm.shape), _resident(masks.shape)],
        out_specs=[row(d),
                   pl.BlockSpec((1, lanes, MOBA_BLOCK), lambda b, i: (b, 0, i)),
                   row(d),
                   pl.BlockSpec((1, 1, HEADS * VT_ROWS, MOBA_BLOCK), lambda b, i: (b, i, 0, 0)),
                   row(2 * d), row(d)],
        out_shape=[shape(d, BF16),
                   jax.ShapeDtypeStruct((BATCH, lanes, SEQ), F32),
                   shape(d, BF16),
                   jax.ShapeDtypeStruct((BATCH, MOBA_NBLK, HEADS * VT_ROWS, MOBA_BLOCK), BF16),
                   shape(2 * d, BF16), shape(d, BF16)],
        scratch_shapes=[pltpu.VMEM((lanes, d), F32),
                        pltpu.VMEM((HEADS, HEAD_DIM, HEAD_DIM), F32),
                        tile(BF16), tile(BF16), tile(F32), tile(BF16), tile(BF16),
                        tile(BF16), tile(F32),
                        pltpu.VMEM((SEQ, HEAD_DIM), F32), pltpu.VMEM((SEQ, HEAD_DIM), F32)],
        compiler_params=_params(("arbitrary", "arbitrary")),
        name="proj",
    )(x, w_in, rope_freq, b_gate, lb_logits, norm_w,
      jnp.asarray(pm, BF16), jnp.asarray(masks, BF16))


def _moba_att_kernel(q_ref, qn_ref, k_ref, vt_ref, bias_ref, o_ref, acc_ref, s_ref):
    own = pl.program_id(1)

    def head(h):
        return slice(h * HEAD_DIM, (h + 1) * HEAD_DIM)

    def scores(h, j):
        kj = k_ref[pl.ds(pl.multiple_of(j * MOBA_BLOCK, MOBA_BLOCK), MOBA_BLOCK), head(h)]
        return lax.dot_general(kj, q_ref[:, head(h)], NT_DIMS, preferred_element_type=F32)

    def item(h, j, s, m, l, next_scores, bias=None):
        top = jnp.max(s, axis=0, keepdims=True)
        m_new = jnp.maximum(m, top if bias is None else top + bias)
        a = jnp.exp2(m - m_new)
        p = jnp.exp2(s - (m_new if bias is None else m_new - 2.0 * bias))
        if next_scores is not None:
            s_ref[h] = next_scores()
        pv = jnp.dot(vt_ref[0, j, h * VT_ROWS:(h + 1) * VT_ROWS, :], p.astype(BF16),
                     preferred_element_type=F32)
        acc_ref[h] = a * acc_ref[h] + pv[:HEAD_DIM]
        return m_new, a * l + pv[HEAD_DIM:HEAD_DIM + 1]

    acc_ref[...] = jnp.zeros_like(acc_ref)

    @pl.when(own == 0)
    def _():
        for h in range(HEADS):
            s_ref[h] = scores(h, 0)

    m0 = jnp.full((1, MOBA_BLOCK), MASK_VALUE, F32)
    l0 = jnp.zeros((1, MOBA_BLOCK), F32)

    def body(j, carry):
        ms, ls = list(carry[0]), list(carry[1])
        for h in range(HEADS):
            ms[h], ls[h] = item(h, j, s_ref[h], ms[h], ls[h], lambda h=h: scores(h, j + 1),
                                bias_ref[0, pl.ds(j * HEADS + h, 1), :])
        return tuple(ms), tuple(ls)

    def several(n):
        def run(t, carry):
            for r in range(n):
                carry = body(n * t + r, carry)
            return carry
        return run

    carry = ((m0,) * HEADS, (l0,) * HEADS)
    done = 0
    for n in (8, 4, 2, 1):
        trips = (own - done) // n
        carry = lax.fori_loop(done // n, done // n + trips, several(n), carry)
        done = done + trips * n
    ms, ls = carry

    key_pos = lax.broadcasted_iota(jnp.int32, (MOBA_BLOCK, MOBA_BLOCK), 0)
    qry_pos = lax.broadcasted_iota(jnp.int32, (MOBA_BLOCK, MOBA_BLOCK), 1)
    causal = key_pos <= qry_pos
    def next_block_scores(h):
        return lax.dot_general(k_ref[0:MOBA_BLOCK, head(h)], qn_ref[:, head(h)], NT_DIMS,
                               preferred_element_type=F32)

    for h in range(HEADS):
        s = jnp.where(causal, s_ref[h], MASK_VALUE)
        _, l = item(h, own, s, ms[h], ls[h], lambda h=h: next_block_scores(h))
        o_ref[:, head(h)] = (acc_ref[h] * (1.0 / l)).T.astype(o_ref.dtype)


def _moba_attention(q_scaled, k_rot, v_t, bias):
    lanes = HEADS * MOBA_NBLK
    return pl.pallas_call(
        _moba_att_kernel,
        grid=(BATCH, MOBA_NBLK),
        in_specs=[pl.BlockSpec((MOBA_BLOCK, D_MODEL), lambda b, i: (b * MOBA_NBLK + i, 0)),
                  pl.BlockSpec((MOBA_BLOCK, D_MODEL),
                               lambda b, i: (b * MOBA_NBLK + jnp.minimum(i + 1, MOBA_NBLK - 1), 0)),
                  pl.BlockSpec((SEQ, D_MODEL), lambda b, i: (b, 0)),
                  pl.BlockSpec((1, MOBA_NBLK, HEADS * VT_ROWS, MOBA_BLOCK),
                               lambda b, i: (b, 0, 0, 0)),
                  pl.BlockSpec((1, lanes, MOBA_BLOCK), lambda b, i: (b, 0, i))],
        out_specs=pl.BlockSpec((MOBA_BLOCK, D_MODEL), lambda b, i: (b * MOBA_NBLK + i, 0)),
        out_shape=jax.ShapeDtypeStruct((M_TOKENS, D_MODEL), BF16),
        scratch_shapes=[pltpu.VMEM((HEADS, HEAD_DIM, MOBA_BLOCK), F32),
                        pltpu.VMEM((HEADS, MOBA_BLOCK, MOBA_BLOCK), F32)],
        compiler_params=_params(("arbitrary", "arbitrary")),
        name="moba_att",
    )(q_scaled, q_scaled, k_rot, v_t, bias)


HGRN_TILE = MOBA_BLOCK
PM_LEVELS = (1, 2)
HGRN_GROUP = 4


def _hgrn_constants():
    c = HGRN_CHUNK
    idx = np.arange(c)
    tri = (idx[None, :] <= idx[:, None]).astype(np.float32)
    rows = [tri]
    masks = []
    for lvl in range(HGRN_LEVELS):
        half = 1 << lvl
        group = idx // (2 * half)
        upper = (idx & half) != 0
        if lvl in PM_LEVELS:
            boundary = tri[group * (2 * half) + half - 1]
            rows.append(np.where(upper[:, None], tri - boundary, boundary - tri))
        masks.append(((group[:, None] == group[None, :]) & upper[:, None] & ~upper[None, :])
                     .astype(np.float32))
    masks.append(np.eye(c, dtype=np.float32))
    return np.concatenate(rows, axis=0), np.stack(masks)


def _hgrn_tile(q_ref, lf_ref, k_ref, v_ref, og_ref, nw_ref, pm_ref, mask_ref, o_ref, state_ref,
               fill):
    c = HGRN_CHUNK
    n_ch = HGRN_TILE // c
    units = [(ch, h) for ch in range(n_ch) for h in range(HEADS)]
    rows = lambda ch: slice(ch * c, (ch + 1) * c)
    lanes = lambda h: slice(h * HEAD_DIM, (h + 1) * HEAD_DIM)
    chunks_on_lanes = lambda t: jnp.concatenate([t[rows(ch), :] for ch in range(n_ch)], axis=1)

    lf = lf_ref[...]
    hi = lf.astype(BF16)
    rem = lf - hi.astype(F32)
    mid = rem.astype(BF16)
    lo = (rem - mid.astype(F32)).astype(BF16)
    split = jnp.concatenate([chunks_on_lanes(hi), chunks_on_lanes(mid), chunks_on_lanes(lo)], axis=0)
    sums = jnp.dot(pm_ref[...], split, preferred_element_type=F32)

    row = lax.broadcasted_iota(jnp.int32, (c, HEAD_DIM), 0)
    upper = [(row & (1 << lvl)) != 0 for lvl in range(HGRN_LEVELS)]
    t_idx = lax.broadcasted_iota(jnp.int32, (c, c), 0)
    s_idx = lax.broadcasted_iota(jnp.int32, (c, c), 1)
    on_diagonal = (t_idx == s_idx).astype(F32)
    below_diagonal = ((t_idx == s_idx + 1) & ((t_idx & 1) != 0)).astype(F32)

    def decayed_operands(u, ch, h):
        col = slice(u * HEAD_DIM, (u + 1) * HEAD_DIM)
        g = sums[0:c, col]
        q_bf, k_bf = q_ref[rows(ch), lanes(h)], k_ref[rows(ch), lanes(h)]
        qf, kk = q_bf.astype(F32), k_bf.astype(F32)
        same = jnp.sum(qf * kk, axis=-1, keepdims=True)
        q_dec = jnp.where(upper[0], qf * jnp.exp2(lf_ref[rows(ch), lanes(h)]), 0.0)
        prev = jnp.sum(q_dec * pltpu.roll(kk, 1, 0), axis=-1, keepdims=True)
        near = (same * on_diagonal + prev * below_diagonal).astype(BF16)
        zs = []
        for lvl in range(1, HGRN_LEVELS):
            half = 1 << lvl
            if lvl in PM_LEVELS:
                i = PM_LEVELS.index(lvl)
                e = sums[(i + 1) * c:(i + 2) * c, col]
            else:
                boundary = jnp.concatenate(
                    [jnp.broadcast_to(g[b + half - 1:b + half, :], (2 * half, HEAD_DIM))
                     for b in range(0, c, 2 * half)], axis=0)
                e = -jnp.abs(g - boundary)
            zs.append((jnp.where(upper[lvl], qf, kk) * jnp.exp2(e)).astype(BF16))
        g_last = g[c - 1:c, :]
        return dict(near=near, zs=zs,
                    q_in=(qf * jnp.exp2(g)).astype(BF16),
                    k_out=(kk * jnp.exp2(g_last - g)).astype(BF16),
                    decay=jnp.exp2(g_last),
                    v=v_ref[rows(ch), lanes(h)])

    def state_free_matmuls(op):
        score = lambda lhs, rhs: lax.dot_general(lhs, rhs, NT_DIMS,
                                                 preferred_element_type=F32).astype(BF16)
        a = op["near"]
        for lvl, z in enumerate(op["zs"], start=1):
            a = a + score(z, z) * mask_ref[lvl]
        op["o"] = jnp.dot(a, op["v"], preferred_element_type=F32)
        op["update"] = lax.dot_general(op["v"], op["k_out"], TN_DIMS, preferred_element_type=F32)

    for first_head in range(0, HEADS, HGRN_GROUP):
        heads = range(first_head, first_head + HGRN_GROUP)
        group = [(ch, h) for ch in range(n_ch) for h in heads]
        ops = {}
        for i, (ch, h) in enumerate(group):
            ops[ch, h] = decayed_operands(ch * HEADS + h, ch, h)
            if i >= 1:
                state_free_matmuls(ops[group[i - 1]])
            fill()
        state_free_matmuls(ops[group[-1]])

        for h in heads:
            state_t = state_ref[h]
            for ch in range(n_ch):
                op = ops[ch, h]
                op["state"] = state_t.astype(BF16)
                state_t = state_t * op["decay"] + op["update"]
            state_ref[h] = state_t
        for ch, h in group:
            op = ops[ch, h]
            o = op["o"] + lax.dot_general(op["q_in"], op["state"], NT_DIMS,
                                          preferred_element_type=F32)
            ms = jnp.mean(o * o, axis=-1, keepdims=True)
            y = o * lax.rsqrt(ms + RMS_EPS) * nw_ref[:, lanes(h)] * og_ref[rows(ch), lanes(h)]
            o_ref[rows(ch), lanes(h)] = y.astype(o_ref.dtype)
            fill()


MIX_TM = 512


def _mix_kernel(ya_ref, yb_ref, gate_ref, x_ref, wa_ref, wb_ref, wo_ref, lnw_ref, lnb_ref,
                o_ref, ob_ref):
    half = MIX_TM // 2
    first, second = slice(0, half), slice(half, MIX_TM)

    def merged(rows, za):
        zb = jnp.dot(yb_ref[rows, :], wb_ref[...], preferred_element_type=F32)
        return (gate_ref[rows, :D_MODEL] * za + gate_ref[rows, D_MODEL:] * zb).astype(BF16)

    def pre_norm(rows, za):
        mixed = jnp.dot(merged(rows, za), wo_ref[...], preferred_element_type=F32)
        return DN_ALPHA * x_ref[rows, :] + mixed

    def finish(rows, r):
        y = _layer_norm(r, lnw_ref[...], lnb_ref[...])
        o_ref[rows, :] = y
        ob_ref[rows, :] = y.astype(BF16)

    branch_a = lambda rows: jnp.dot(ya_ref[rows, :], wa_ref[...], preferred_element_type=F32)
    r_first = pre_norm(first, branch_a(first))
    za_second = branch_a(second)
    finish(first, r_first)
    finish(second, pre_norm(second, za_second))


def _mix(ya, yb, gates, x, wa, wb, wo, lnw, lnb):
    d = D_MODEL
    row = lambda n: pl.BlockSpec((MIX_TM, n), lambda i: (i, 0))
    return pl.pallas_call(
        _mix_kernel,
        grid=(M_TOKENS // MIX_TM,),
        in_specs=[row(d), row(d), row(2 * d), row(d),
                  _resident((d, d)), _resident((d, d)), _resident((d, d)),
                  _resident((1, d)), _resident((1, d))],
        out_specs=[row(d), row(d)],
        out_shape=[jax.ShapeDtypeStruct((M_TOKENS, d), F32),
                   jax.ShapeDtypeStruct((M_TOKENS, d), BF16)],
        compiler_params=_params(("parallel",)),
        name="mix",
    )(ya, yb, gates, x, wa, wb, wo, lnw, lnb)


FFN_TM = 512
FFN_CHUNK = 1408


def _ffn_kernel(xb_ref, x_ref, win_ref, wd_ref, lnw_ref, lnb_ref, o_ref):
    half = FFN_TM // 2
    n_chunks = D_FF // FFN_CHUNK

    def chunk(rows, c):
        cs = slice(c * FFN_CHUNK, (c + 1) * FFN_CHUNK)
        xb = xb_ref[rows, :]
        up = slice(D_FF + c * FFN_CHUNK, D_FF + (c + 1) * FFN_CHUNK)
        hg = jnp.dot(xb, win_ref[:, cs], preferred_element_type=F32)
        hu = jnp.dot(xb, win_ref[:, up], preferred_element_type=F32)
        act = (hg * _sigmoid(hg) * hu).astype(BF16)
        return jnp.dot(act, wd_ref[cs, :], preferred_element_type=F32)

    def finish(rows, y):
        o_ref[rows, :] = _layer_norm(DN_ALPHA * x_ref[rows, :] + y, lnw_ref[...], lnb_ref[...])

    first, second = slice(0, half), slice(half, FFN_TM)
    y_first = sum(chunk(first, c) for c in range(n_chunks))
    y_second = chunk(second, 0)
    finish(first, y_first)
    for c in range(1, n_chunks):
        y_second = y_second + chunk(second, c)
    finish(second, y_second)


def _ffn(xb, x, w_in, wd, lnw, lnb):
    d = D_MODEL
    row = lambda n: pl.BlockSpec((FFN_TM, n), lambda i: (i, 0))
    return pl.pallas_call(
        _ffn_kernel,
        grid=(M_TOKENS // FFN_TM,),
        in_specs=[row(d), row(d), _resident((d, 2 * D_FF)),
                  _resident((D_FF, d)), _resident((1, d)), _resident((1, d))],
        out_specs=row(d),
        out_shape=jax.ShapeDtypeStruct((M_TOKENS, d), F32),
        compiler_params=_params(("parallel",)),
        name="ffn",
    )(xb, x, w_in, wd, lnw, lnb)


def _rope_frequencies():
    half = HEAD_DIM // 2
    inv_freq = ROPE_THETA ** (-jnp.arange(half, dtype=F32) / half)
    return jnp.concatenate([inv_freq, inv_freq]).reshape(1, HEAD_DIM)


def _layer(x2, w_in, lb_logits, hgrn_norm_w, w_branch_a, w_branch_b, b_gate, w_out,
           ln1_w, ln1_b, w_ffn_in, w_ffn_down, ln2_w, ln2_b):
    d = D_MODEL
    row2 = lambda v: v.reshape(1, -1).astype(F32)

    q_scaled, bias, k_rot, v_t, gates, y_a = _proj(
        x2, w_in.astype(BF16), _rope_frequencies(), row2(b_gate), lb_logits.astype(F32),
        row2(hgrn_norm_w))
    y_b = _moba_attention(q_scaled, k_rot, v_t, bias)

    x1, x1b = _mix(y_a, y_b, gates, x2, w_branch_a.astype(BF16), w_branch_b.astype(BF16),
                   w_out.astype(BF16), row2(ln1_w), row2(ln1_b))
    return _ffn(x1b, x1, w_ffn_in.astype(BF16), w_ffn_down.astype(BF16), row2(ln2_w), row2(ln2_b))


def kernel(x, w_in, lb_logits, hgrn_norm_w, w_branch_a, w_branch_b, b_gate, w_out, ln1_w, ln1_b,
           w_ffn_in, w_ffn_down, ln2_w, ln2_b):
    assert DEPTH == 1
    h = x.reshape(M_TOKENS, D_MODEL)
    h = _layer(h, w_in[0], lb_logits, hgrn_norm_w[0], w_branch_a[0], w_branch_b[0],
               b_gate[0], w_out[0], ln1_w[0], ln1_b[0], w_ffn_in[0], w_ffn_down[0],
               ln2_w[0], ln2_b[0])
    return h.reshape(BATCH, SEQ, D_MODEL)
```

```python
import numpy as np
import jax
import jax.numpy as jnp
from jax import lax
from jax.experimental import pallas as pl
from jax.experimental.pallas import tpu as pltpu

D_MODEL = 1024
BATCH = 4
SEQ = 4096
DEPTH = 1
HEADS = 8
HEAD_DIM = 128
HGRN_CHUNK = 64
HGRN_LEVELS = 6
MOBA_BLOCK = 256
MOBA_NBLK = SEQ // MOBA_BLOCK
MOBA_TOPK = 3
ROPE_THETA = 10000.0
D_FF = 2816
DN_ALPHA = (2.0 * DEPTH) ** 0.25
LN_EPS = 1e-5
RMS_EPS = 1e-6
M_TOKENS = BATCH * SEQ

MASK_VALUE = -1e30
VT_ROWS = HEAD_DIM + 16
LOG2_E = 1.4426950408889634
VMEM_LIMIT = 56 * 1024 * 1024

F32 = jnp.float32
BF16 = jnp.bfloat16
NT_DIMS = (((1,), (1,)), ((), ()))
TN_DIMS = (((0,), (0,)), ((), ()))


def _params(semantics):
    return pltpu.CompilerParams(dimension_semantics=semantics, vmem_limit_bytes=VMEM_LIMIT)


def _resident(shape):
    return pl.BlockSpec(shape, lambda *_: (0,) * len(shape), pipeline_mode=pl.Buffered(1))


def _sigmoid(z):
    return 1.0 / (1.0 + jnp.exp(-z))


def _layer_norm(r, w, b):
    mu = jnp.mean(r, axis=-1, keepdims=True)
    d = r - mu
    var = jnp.mean(d * d, axis=-1, keepdims=True)
    return d * lax.rsqrt(var + LN_EPS) * w + b


PROJ_SLAB = 256
FILL_EVERY = 4
N_PROJ = 9 * D_MODEL
COL_HGRN_Q, COL_HGRN_F, COL_HGRN_V, COL_HGRN_GATE = 0, D_MODEL, 2 * D_MODEL, 3 * D_MODEL
COL_MOBA_Q, COL_MOBA_K, COL_MOBA_V, COL_MERGE_GATE = 4 * D_MODEL, 5 * D_MODEL, 6 * D_MODEL, 7 * D_MODEL


def _proj_kernel(x_ref, w_ref, freq_ref,
                 bg_ref, lbl_ref, nw_ref, pm_ref, mask_ref,
                 q_ref, bias_ref, k_ref, vt_ref, gate_ref, ya_ref,
                 km_ref, state_ref, hq_ref, hog_ref, hlf_ref, hk_ref, hv_ref, xb_ref, qf_ref,
                 cos_all_ref, sin_all_ref):
    own = pl.program_id(1)
    lanes = HEADS * MOBA_NBLK

    @pl.when(own == 0)
    def _():
        km_ref[...] = jnp.zeros_like(km_ref)
        state_ref[...] = jnp.zeros_like(state_ref)

    xb_ref[...] = x_ref[...].astype(BF16)
    block_rows = pl.ds(pl.multiple_of(own * MOBA_BLOCK, MOBA_BLOCK), MOBA_BLOCK)

    @pl.when(pl.program_id(0) == 0)
    def _():
        position = own * MOBA_BLOCK + lax.broadcasted_iota(jnp.int32, (MOBA_BLOCK, HEAD_DIM), 0)
        angle = position.astype(F32) * freq_ref[...]
        first_half = lax.broadcasted_iota(jnp.int32, (MOBA_BLOCK, HEAD_DIM), 1) < HEAD_DIM // 2
        cos_all_ref[block_rows, :] = jnp.cos(angle)
        sin_all_ref[block_rows, :] = jnp.where(first_half, -jnp.sin(angle), jnp.sin(angle))

    cos_ref = cos_all_ref.at[block_rows, :]
    sin_ref = sin_all_ref.at[block_rows, :]
    head = lambda h: slice(h * HEAD_DIM, (h + 1) * HEAD_DIM)
    slab = lambda i: slice(i * PROJ_SLAB, (i + 1) * PROJ_SLAB)
    n_slabs = D_MODEL // PROJ_SLAB

    def project(first_col, i):
        cols = slice(first_col + i * PROJ_SLAB, first_col + (i + 1) * PROJ_SLAB)
        return jnp.dot(xb_ref[...], w_ref[:, cols], preferred_element_type=F32)

    rope = lambda t: t * cos_ref[...] + pltpu.roll(t, HEAD_DIM // 2, 1) * sin_ref[...]


    def hgrn_silu(i):
        def run():
            acc = project(COL_HGRN_Q if i < n_slabs else COL_HGRN_GATE, i % n_slabs)
            dst = hq_ref if i < n_slabs else hog_ref
            dst[:, slab(i % n_slabs)] = (acc * _sigmoid(acc)).astype(dst.dtype)
        return run

    def hgrn_forget(i):
        def run():
            l0, l1 = lbl_ref[0:1, slab(i)], lbl_ref[1:2, slab(i)]
            top = jnp.maximum(l0, l1)
            e0, e1 = jnp.exp(l0 - top), jnp.exp(l1 - top)
            lb = e0 / (e0 + e1)
            z = project(COL_HGRN_F, i)
            hlf_ref[:, slab(i)] = jnp.log(lb + (1.0 - lb) * _sigmoid(z)) * LOG2_E
            hk_ref[:, slab(i)] = ((1.0 - lb) * _sigmoid(-z)).astype(hk_ref.dtype)
        return run

    def hgrn_value(i):
        def run():
            hv_ref[:, slab(i)] = project(COL_HGRN_V, i).astype(hv_ref.dtype)
        return run

    heads_of = lambda i: range(i * PROJ_SLAB // HEAD_DIM, (i + 1) * PROJ_SLAB // HEAD_DIM)
    in_slab = lambda i, h: slice((h - heads_of(i)[0]) * HEAD_DIM, (h - heads_of(i)[0] + 1) * HEAD_DIM)

    def moba_query(i):
        def run():
            acc = project(COL_MOBA_Q, i)
            for h in heads_of(i):
                qh = rope(acc[:, in_slab(i, h)])
                qf_ref[:, head(h)] = qh
                q_ref[:, head(h)] = (qh * (HEAD_DIM ** -0.5 * LOG2_E)).astype(q_ref.dtype)
        return run

    def moba_choice():
        split = lambda t: (t.astype(BF16), (t - t.astype(BF16).astype(F32)).astype(BF16))
        q_hi, q_lo = split(qf_ref[...])
        km_hi, km_lo = split(km_ref[...])
        gate = lax.dot_general(jnp.concatenate([q_hi, q_lo, q_hi], axis=1),
                               jnp.concatenate([km_hi, km_hi, km_lo], axis=1), NT_DIMS,
                               preferred_element_type=F32)
        lane = lax.broadcasted_iota(jnp.int32, (MOBA_BLOCK, lanes), 1)
        eligible = lane < own * HEADS
        gate = jnp.where(eligible, gate, -jnp.inf)
        rank = jnp.zeros((MOBA_BLOCK, lanes), F32)
        for r in range(1, MOBA_NBLK):
            partner = pltpu.roll(gate, lanes - r * HEADS, 1)
            wrapped = lane >= lanes - r * HEADS
            beats = (partner > gate) | (wrapped & (partner == gate))
            rank = rank + beats.astype(F32)
        selected = eligible & (rank < float(MOBA_TOPK))
        bias_ref[0] = jnp.where(selected, 0.0, MASK_VALUE).T

    def moba_key(i):
        def run():
            acc = project(COL_MOBA_K, i)
            for h in heads_of(i):
                rot = rope(acc[:, in_slab(i, h)])
                k_ref[:, head(h)] = rot.astype(k_ref.dtype)
                mean = jnp.mean(rot, axis=0, keepdims=True)
                zeros = lambda n: [jnp.zeros((1, n * HEAD_DIM), F32)] if n else []
                km_ref[pl.ds(own * HEADS + h, 1), :] = jnp.concatenate(
                    zeros(h) + [mean] + zeros(HEADS - 1 - h), axis=1)
        return run

    def moba_value(i):
        def run():
            vt = project(COL_MOBA_V, i).T.astype(vt_ref.dtype)
            pad_row = lax.broadcasted_iota(jnp.int32, (VT_ROWS - HEAD_DIM, MOBA_BLOCK), 0)
            ones_row = jnp.where(pad_row == 0, 1.0, 0.0).astype(vt_ref.dtype)
            for h in heads_of(i):
                vt_ref[0, 0, h * VT_ROWS:h * VT_ROWS + HEAD_DIM, :] = vt[in_slab(i, h), :]
                vt_ref[0, 0, h * VT_ROWS + HEAD_DIM:(h + 1) * VT_ROWS, :] = ones_row
        return run

    def merge_gate(i):
        def run():
            acc = project(COL_MERGE_GATE, i) + bg_ref[:, slab(i)]
            gate_ref[:, slab(i)] = _sigmoid(acc).astype(gate_ref.dtype)
        return run

    hgrn_inputs = ([hgrn_silu(i) for i in range(2 * n_slabs)]
                   + [hgrn_forget(i) for i in range(n_slabs)]
                   + [hgrn_value(i) for i in range(n_slabs)])
    queries = [moba_query(i) for i in range(n_slabs)] + [moba_choice]
    for i, piece in enumerate(hgrn_inputs):
        piece()
        if i < len(queries):
            queries[i]()

    rest = ([moba_key(i) for i in range(n_slabs)] + [moba_value(i) for i in range(n_slabs)]
            + [merge_gate(i) for i in range(2 * n_slabs)])
    calls = []

    def fill():
        calls.append(None)
        if rest and len(calls) % FILL_EVERY == 0:
            rest.pop(0)()

    _hgrn_tile(hq_ref, hlf_ref, hk_ref, hv_ref, hog_ref, nw_ref, pm_ref, mask_ref, ya_ref,
               state_ref, fill)
    while rest:
        fill()


def _proj(x, w_in, rope_freq, b_gate, lb_logits, norm_w):
    d = D_MODEL
    lanes = HEADS * MOBA_NBLK
    pm, masks = _hgrn_constants()
    pm = np.tile(pm, (1, 3))
    row = lambda n: pl.BlockSpec((MOBA_BLOCK, n), lambda b, i: (b * MOBA_NBLK + i, 0))
    shape = lambda n, dt: jax.ShapeDtypeStruct((M_TOKENS, n), dt)
    tile = lambda dt: pltpu.VMEM((MOBA_BLOCK, d), dt)
    return pl.pallas_call(
        _proj_kernel,
        grid=(BATCH, MOBA_NBLK),
        in_specs=[row(d), _resident((d, N_PROJ)), _resident((1, HEAD_DIM)), _resident((1, 2 * d)), _resident((DEPTH + 1, d)),
                  _resident((1, d)), _resident(pm.shape), _resident(masks.shape)],
        out_specs=[row(d),
                   pl.BlockSpec((1, lanes, MOBA_BLOCK), lambda b, i: (b, 0, i)),
                   row(d),
                   pl.BlockSpec((1, 1, HEADS * VT_ROWS, MOBA_BLOCK), lambda b, i: (b, i, 0, 0)),
                   row(2 * d), row(d)],
        out_shape=[shape(d, BF16),
                   jax.ShapeDtypeStruct((BATCH, lanes, SEQ), F32),
                   shape(d, BF16),
                   jax.ShapeDtypeStruct((BATCH, MOBA_NBLK, HEADS * VT_ROWS, MOBA_BLOCK), BF16),
                   shape(2 * d, BF16), shape(d, BF16)],
        scratch_shapes=[pltpu.VMEM((lanes, d), F32),
                        pltpu.VMEM((HEADS, HEAD_DIM, HEAD_DIM), F32),
                        tile(BF16), tile(BF16), tile(F32), tile(BF16), tile(BF16),
                        tile(BF16), tile(F32),
                        pltpu.VMEM((SEQ, HEAD_DIM), F32), pltpu.VMEM((SEQ, HEAD_DIM), F32)],
        compiler_params=_params(("arbitrary", "arbitrary")),
        name="proj",
    )(x, w_in, rope_freq, b_gate, lb_logits, norm_w,
      jnp.asarray(pm, BF16), jnp.asarray(masks, BF16))


def _moba_att_kernel(q_ref, qn_ref, k_ref, vt_ref, bias_ref, o_ref, acc_ref, s_ref):
    own = pl.program_id(1)

    def head(h):
        return slice(h * HEAD_DIM, (h + 1) * HEAD_DIM)

    def scores(h, j):
        kj = k_ref[pl.ds(pl.multiple_of(j * MOBA_BLOCK, MOBA_BLOCK), MOBA_BLOCK), head(h)]
        return lax.dot_general(kj, q_ref[:, head(h)], NT_DIMS, preferred_element_type=F32)

    def item(h, j, s, m, l, next_scores, bias=None):
        top = jnp.max(s, axis=0, keepdims=True)
        m_new = jnp.maximum(m, top if bias is None else top + bias)
        a = jnp.exp2(m - m_new)
        p = jnp.exp2(s - (m_new if bias is None else m_new - 2.0 * bias))
        if next_scores is not None:
            s_ref[h] = next_scores()
        pv = jnp.dot(vt_ref[0, j, h * VT_ROWS:(h + 1) * VT_ROWS, :], p.astype(BF16),
                     preferred_element_type=F32)
        acc_ref[h] = a * acc_ref[h] + pv[:HEAD_DIM]
        return m_new, a * l + pv[HEAD_DIM:HEAD_DIM + 1]

    acc_ref[...] = jnp.zeros_like(acc_ref)

    @pl.when(own == 0)
    def _():
        for h in range(HEADS):
            s_ref[h] = scores(h, 0)

    m0 = jnp.full((1, MOBA_BLOCK), MASK_VALUE, F32)
    l0 = jnp.zeros((1, MOBA_BLOCK), F32)

    def body(j, carry):
        ms, ls = list(carry[0]), list(carry[1])
        for h in range(HEADS):
            ms[h], ls[h] = item(h, j, s_ref[h], ms[h], ls[h], lambda h=h: scores(h, j + 1),
                                bias_ref[0, pl.ds(j * HEADS + h, 1), :])
        return tuple(ms), tuple(ls)

    def several(n):
        def run(t, carry):
            for r in range(n):
                carry = body(n * t + r, carry)
            return carry
        return run

    carry = ((m0,) * HEADS, (l0,) * HEADS)
    done = 0
    for n in (8, 4, 2, 1):
        trips = (own - done) // n
        carry = lax.fori_loop(done // n, done // n + trips, several(n), carry)
        done = done + trips * n
    ms, ls = carry

    key_pos = lax.broadcasted_iota(jnp.int32, (MOBA_BLOCK, MOBA_BLOCK), 0)
    qry_pos = lax.broadcasted_iota(jnp.int32, (MOBA_BLOCK, MOBA_BLOCK), 1)
    causal = key_pos <= qry_pos
    def next_block_scores(h):
        return lax.dot_general(k_ref[0:MOBA_BLOCK, head(h)], qn_ref[:, head(h)], NT_DIMS,
                               preferred_element_type=F32)

    for h in range(HEADS):
        s = jnp.where(causal, s_ref[h], MASK_VALUE)
        _, l = item(h, own, s, ms[h], ls[h], lambda h=h: next_block_scores(h))
        o_ref[:, head(h)] = (acc_ref[h] * (1.0 / l)).T.astype(o_ref.dtype)


def _moba_attention(q_scaled, k_rot, v_t, bias):
    lanes = HEADS * MOBA_NBLK
    return pl.pallas_call(
        _moba_att_kernel,
        grid=(BATCH, MOBA_NBLK),
        in_specs=[pl.BlockSpec((MOBA_BLOCK, D_MODEL), lambda b, i: (b * MOBA_NBLK + i, 0)),
                  pl.BlockSpec((MOBA_BLOCK, D_MODEL),
                               lambda b, i: (b * MOBA_NBLK + jnp.minimum(i + 1, MOBA_NBLK - 1), 0)),
                  pl.BlockSpec((SEQ, D_MODEL), lambda b, i: (b, 0)),
                  pl.BlockSpec((1, MOBA_NBLK, HEADS * VT_ROWS, MOBA_BLOCK),
                               lambda b, i: (b, 0, 0, 0)),
                  pl.BlockSpec((1, lanes, MOBA_BLOCK), lambda b, i: (b, 0, i))],
        out_specs=pl.BlockSpec((MOBA_BLOCK, D_MODEL), lambda b, i: (b * MOBA_NBLK + i, 0)),
        out_shape=jax.ShapeDtypeStruct((M_TOKENS, D_MODEL), BF16),
        scratch_shapes=[pltpu.VMEM((HEADS, HEAD_DIM, MOBA_BLOCK), F32),
                        pltpu.VMEM((HEADS, MOBA_BLOCK, MOBA_BLOCK), F32)],
        compiler_params=_params(("arbitrary", "arbitrary")),
        name="moba_att",
    )(q_scaled, q_scaled, k_rot, v_t, bias)


HGRN_TILE = MOBA_BLOCK
PM_LEVELS = (1, 2)
HGRN_GROUP = 4


def _hgrn_constants():
    c = HGRN_CHUNK
    idx = np.arange(c)
    tri = (idx[None, :] <= idx[:, None]).astype(np.float32)
    rows = [tri]
    masks = []
    for lvl in range(HGRN_LEVELS):
        half = 1 << lvl
        group = idx // (2 * half)
        upper = (idx & half) != 0
        if lvl in PM_LEVELS:
            boundary = tri[group * (2 * half) + half - 1]
            rows.append(np.where(upper[:, None], tri - boundary, boundary - tri))
        masks.append(((group[:, None] == group[None, :]) & upper[:, None] & ~upper[None, :])
                     .astype(np.float32))
    masks.append(np.eye(c, dtype=np.float32))
    return np.concatenate(rows, axis=0), np.stack(masks)


def _hgrn_tile(q_ref, lf_ref, k_ref, v_ref, og_ref, nw_ref, pm_ref, mask_ref, o_ref, state_ref,
               fill):
    c = HGRN_CHUNK
    n_ch = HGRN_TILE // c
    units = [(ch, h) for ch in range(n_ch) for h in range(HEADS)]
    rows = lambda ch: slice(ch * c, (ch + 1) * c)
    lanes = lambda h: slice(h * HEAD_DIM, (h + 1) * HEAD_DIM)
    chunks_on_lanes = lambda t: jnp.concatenate([t[rows(ch), :] for ch in range(n_ch)], axis=1)

    lf = lf_ref[...]
    hi = lf.astype(BF16)
    rem = lf - hi.astype(F32)
    mid = rem.astype(BF16)
    lo = (rem - mid.astype(F32)).astype(BF16)
    split = jnp.concatenate([chunks_on_lanes(hi), chunks_on_lanes(mid), chunks_on_lanes(lo)], axis=0)
    sums = jnp.dot(pm_ref[...], split, preferred_element_type=F32)

    row = lax.broadcasted_iota(jnp.int32, (c, HEAD_DIM), 0)
    upper = [(row & (1 << lvl)) != 0 for lvl in range(HGRN_LEVELS)]
    t_idx = lax.broadcasted_iota(jnp.int32, (c, c), 0)
    s_idx = lax.broadcasted_iota(jnp.int32, (c, c), 1)
    on_diagonal = (t_idx == s_idx).astype(F32)
    below_diagonal = ((t_idx == s_idx + 1) & ((t_idx & 1) != 0)).astype(F32)

    def decayed_operands(u, ch, h):
        col = slice(u * HEAD_DIM, (u + 1) * HEAD_DIM)
        g = sums[0:c, col]
        q_bf, k_bf = q_ref[rows(ch), lanes(h)], k_ref[rows(ch), lanes(h)]
        qf, kk = q_bf.astype(F32), k_bf.astype(F32)
        same = jnp.sum(qf * kk, axis=-1, keepdims=True)
        q_dec = jnp.where(upper[0], qf * jnp.exp2(lf_ref[rows(ch), lanes(h)]), 0.0)
        prev = jnp.sum(q_dec * pltpu.roll(kk, 1, 0), axis=-1, keepdims=True)
        near = (same * on_diagonal + prev * below_diagonal).astype(BF16)
        zs = []
        for lvl in range(1, HGRN_LEVELS):
            half = 1 << lvl
            if lvl in PM_LEVELS:
                i = PM_LEVELS.index(lvl)
                e = sums[(i + 1) * c:(i + 2) * c, col]
            else:
                boundary = jnp.concatenate(
                    [jnp.broadcast_to(g[b + half - 1:b + half, :], (2 * half, HEAD_DIM))
                     for b in range(0, c, 2 * half)], axis=0)
                e = -jnp.abs(g - boundary)
            zs.append((jnp.where(upper[lvl], qf, kk) * jnp.exp2(e)).astype(BF16))
        g_last = g[c - 1:c, :]
        return dict(near=near, zs=zs,
                    q_in=(qf * jnp.exp2(g)).astype(BF16),
                    k_out=(kk * jnp.exp2(g_last - g)).astype(BF16),
                    decay=jnp.exp2(g_last),
                    v=v_ref[rows(ch), lanes(h)])

    def state_free_matmuls(op):
        score = lambda lhs, rhs: lax.dot_general(lhs, rhs, NT_DIMS,
                                                 preferred_element_type=F32).astype(BF16)
        a = op["near"]
        for lvl, z in enumerate(op["zs"], start=1):
            a = a + score(z, z) * mask_ref[lvl]
        op["o"] = jnp.dot(a, op["v"], preferred_element_type=F32)
        op["update"] = lax.dot_general(op["v"], op["k_out"], TN_DIMS, preferred_element_type=F32)

    for first_head in range(0, HEADS, HGRN_GROUP):
        heads = range(first_head, first_head + HGRN_GROUP)
        group = [(ch, h) for ch in range(n_ch) for h in heads]
        ops = {}
        for i, (ch, h) in enumerate(group):
            ops[ch, h] = decayed_operands(ch * HEADS + h, ch, h)
            if i >= 1:
                state_free_matmuls(ops[group[i - 1]])
            fill()
        state_free_matmuls(ops[group[-1]])

        for h in heads:
            state_t = state_ref[h]
            for ch in range(n_ch):
                op = ops[ch, h]
                op["state"] = state_t.T.astype(BF16)
                state_t = state_t * op["decay"] + op["update"]
            state_ref[h] = state_t
        for ch, h in group:
            op = ops[ch, h]
            o = op["o"] + jnp.dot(op["q_in"], op["state"], preferred_element_type=F32)
            ms = jnp.mean(o * o, axis=-1, keepdims=True)
            y = o * lax.rsqrt(ms + RMS_EPS) * nw_ref[:, lanes(h)] * og_ref[rows(ch), lanes(h)]
            o_ref[rows(ch), lanes(h)] = y.astype(o_ref.dtype)
            fill()


MIX_TM = 512


def _mix_kernel(ya_ref, yb_ref, gate_ref, x_ref, wa_ref, wb_ref, wo_ref, lnw_ref, lnb_ref,
                o_ref, ob_ref):
    half = MIX_TM // 2
    first, second = slice(0, half), slice(half, MIX_TM)

    def merged(rows, za):
        zb = jnp.dot(yb_ref[rows, :], wb_ref[...], preferred_element_type=F32)
        return (gate_ref[rows, :D_MODEL] * za + gate_ref[rows, D_MODEL:] * zb).astype(BF16)

    def pre_norm(rows, za):
        mixed = jnp.dot(merged(rows, za), wo_ref[...], preferred_element_type=F32)
        return DN_ALPHA * x_ref[rows, :] + mixed

    def finish(rows, r):
        y = _layer_norm(r, lnw_ref[...], lnb_ref[...])
        o_ref[rows, :] = y
        ob_ref[rows, :] = y.astype(BF16)

    branch_a = lambda rows: jnp.dot(ya_ref[rows, :], wa_ref[...], preferred_element_type=F32)
    r_first = pre_norm(first, branch_a(first))
    za_second = branch_a(second)
    finish(first, r_first)
    finish(second, pre_norm(second, za_second))


def _mix(ya, yb, gates, x, wa, wb, wo, lnw, lnb):
    d = D_MODEL
    row = lambda n: pl.BlockSpec((MIX_TM, n), lambda i: (i, 0))
    return pl.pallas_call(
        _mix_kernel,
        grid=(M_TOKENS // MIX_TM,),
        in_specs=[row(d), row(d), row(2 * d), row(d),
                  _resident((d, d)), _resident((d, d)), _resident((d, d)),
                  _resident((1, d)), _resident((1, d))],
        out_specs=[row(d), row(d)],
        out_shape=[jax.ShapeDtypeStruct((M_TOKENS, d), F32),
                   jax.ShapeDtypeStruct((M_TOKENS, d), BF16)],
        compiler_params=_params(("parallel",)),
        name="mix",
    )(ya, yb, gates, x, wa, wb, wo, lnw, lnb)


FFN_TM = 512
FFN_CHUNK = 1408


def _ffn_kernel(xb_ref, x_ref, win_ref, wd_ref, lnw_ref, lnb_ref, o_ref):
    half = FFN_TM // 2
    n_chunks = D_FF // FFN_CHUNK

    def chunk(rows, c):
        cs = slice(c * FFN_CHUNK, (c + 1) * FFN_CHUNK)
        xb = xb_ref[rows, :]
        up = slice(D_FF + c * FFN_CHUNK, D_FF + (c + 1) * FFN_CHUNK)
        hg = jnp.dot(xb, win_ref[:, cs], preferred_element_type=F32)
        hu = jnp.dot(xb, win_ref[:, up], preferred_element_type=F32)
        act = (hg * _sigmoid(hg) * hu).astype(BF16)
        return jnp.dot(act, wd_ref[cs, :], preferred_element_type=F32)

    def finish(rows, y):
        o_ref[rows, :] = _layer_norm(DN_ALPHA * x_ref[rows, :] + y, lnw_ref[...], lnb_ref[...])

    first, second = slice(0, half), slice(half, FFN_TM)
    y_first = sum(chunk(first, c) for c in range(n_chunks))
    y_second = chunk(second, 0)
    finish(first, y_first)
    for c in range(1, n_chunks):
        y_second = y_second + chunk(second, c)
    finish(second, y_second)


def _ffn(xb, x, w_in, wd, lnw, lnb):
    d = D_MODEL
    row = lambda n: pl.BlockSpec((FFN_TM, n), lambda i: (i, 0))
    return pl.pallas_call(
        _ffn_kernel,
        grid=(M_TOKENS // FFN_TM,),
        in_specs=[row(d), row(d), _resident((d, 2 * D_FF)),
                  _resident((D_FF, d)), _resident((1, d)), _resident((1, d))],
        out_specs=row(d),
        out_shape=jax.ShapeDtypeStruct((M_TOKENS, d), F32),
        compiler_params=_params(("parallel",)),
        name="ffn",
    )(xb, x, w_in, wd, lnw, lnb)


def _rope_frequencies():
    half = HEAD_DIM // 2
    inv_freq = ROPE_THETA ** (-jnp.arange(half, dtype=F32) / half)
    return jnp.concatenate([inv_freq, inv_freq]).reshape(1, HEAD_DIM)


def _layer(x2, w_in, lb_logits, hgrn_norm_w, w_branch_a, w_branch_b, b_gate, w_out,
           ln1_w, ln1_b, w_ffn_in, w_ffn_down, ln2_w, ln2_b):
    d = D_MODEL
    row2 = lambda v: v.reshape(1, -1).astype(F32)

    q_scaled, bias, k_rot, v_t, gates, y_a = _proj(
        x2, w_in.astype(BF16), _rope_frequencies(), row2(b_gate), lb_logits.astype(F32),
        row2(hgrn_norm_w))
    y_b = _moba_attention(q_scaled, k_rot, v_t, bias)

    x1, x1b = _mix(y_a, y_b, gates, x2, w_branch_a.astype(BF16), w_branch_b.astype(BF16),
                   w_out.astype(BF16), row2(ln1_w), row2(ln1_b))
    return _ffn(x1b, x1, w_ffn_in.astype(BF16), w_ffn_down.astype(BF16), row2(ln2_w), row2(ln2_b))


def kernel(x, w_in, lb_logits, hgrn_norm_w, w_branch_a, w_branch_b, b_gate, w_out, ln1_w, ln1_b,
           w_ffn_in, w_ffn_down, ln2_w, ln2_b):
    assert DEPTH == 1
    h = x.reshape(M_TOKENS, D_MODEL)
    h = _layer(h, w_in[0], lb_logits, hgrn_norm_w[0], w_branch_a[0], w_branch_b[0],
               b_gate[0], w_out[0], ln1_w[0], ln1_b[0], w_ffn_in[0], w_ffn_down[0],
               ln2_w[0], ln2_b[0])
    return h.reshape(BATCH, SEQ, D_MODEL)
```

```python
import numpy as np
import jax
import jax.numpy as jnp
from jax import lax
from jax.experimental import pallas as pl
from jax.experimental.pallas import tpu as pltpu

D_MODEL = 1024
BATCH = 4
SEQ = 4096
DEPTH = 1
HEADS = 8
HEAD_DIM = 128
HGRN_CHUNK = 64
HGRN_LEVELS = 6
MOBA_BLOCK = 256
MOBA_NBLK = SEQ // MOBA_BLOCK
MOBA_TOPK = 3
ROPE_THETA = 10000.0
D_FF = 2816
DN_ALPHA = (2.0 * DEPTH) ** 0.25
LN_EPS = 1e-5
RMS_EPS = 1e-6
M_TOKENS = BATCH * SEQ

MASK_VALUE = -1e30
VT_ROWS = HEAD_DIM + 16
LOG2_E = 1.4426950408889634
VMEM_LIMIT = 56 * 1024 * 1024

F32 = jnp.float32
BF16 = jnp.bfloat16
NT_DIMS = (((1,), (1,)), ((), ()))
TN_DIMS = (((0,), (0,)), ((), ()))


def _params(semantics):
    return pltpu.CompilerParams(dimension_semantics=semantics, vmem_limit_bytes=VMEM_LIMIT)


def _resident(shape):
    return pl.BlockSpec(shape, lambda *_: (0,) * len(shape), pipeline_mode=pl.Buffered(1))


def _sigmoid(z):
    return 1.0 / (1.0 + jnp.exp(-z))


def _layer_norm(r, w, b):
    mu = jnp.mean(r, axis=-1, keepdims=True)
    d = r - mu
    var = jnp.mean(d * d, axis=-1, keepdims=True)
    return d * lax.rsqrt(var + LN_EPS) * w + b


PROJ_SLAB = 256
FILL_EVERY = 4
N_PROJ = 9 * D_MODEL
COL_HGRN_Q, COL_HGRN_F, COL_HGRN_V, COL_HGRN_GATE = 0, D_MODEL, 2 * D_MODEL, 3 * D_MODEL
COL_MOBA_Q, COL_MOBA_K, COL_MOBA_V, COL_MERGE_GATE = 4 * D_MODEL, 5 * D_MODEL, 6 * D_MODEL, 7 * D_MODEL


def _proj_kernel(x_ref, w_ref, freq_ref,
                 bg_ref, lbl_ref, nw_ref, pm_ref, mask_ref,
                 q_ref, bias_ref, k_ref, vt_ref, gate_ref, ya_ref,
                 km_ref, state_ref, hq_ref, hog_ref, hlf_ref, hk_ref, hv_ref, xb_ref, qf_ref,
                 cos_all_ref, sin_all_ref):
    own = pl.program_id(1)
    lanes = HEADS * MOBA_NBLK

    @pl.when(own == 0)
    def _():
        km_ref[...] = jnp.zeros_like(km_ref)
        state_ref[...] = jnp.zeros_like(state_ref)

    xb_ref[...] = x_ref[...].astype(BF16)
    block_rows = pl.ds(pl.multiple_of(own * MOBA_BLOCK, MOBA_BLOCK), MOBA_BLOCK)

    @pl.when(pl.program_id(0) == 0)
    def _():
        position = own * MOBA_BLOCK + lax.broadcasted_iota(jnp.int32, (MOBA_BLOCK, HEAD_DIM), 0)
        angle = position.astype(F32) * freq_ref[...]
        first_half = lax.broadcasted_iota(jnp.int32, (MOBA_BLOCK, HEAD_DIM), 1) < HEAD_DIM // 2
        cos_all_ref[block_rows, :] = jnp.cos(angle)
        sin_all_ref[block_rows, :] = jnp.where(first_half, -jnp.sin(angle), jnp.sin(angle))

    cos_ref = cos_all_ref.at[block_rows, :]
    sin_ref = sin_all_ref.at[block_rows, :]
    head = lambda h: slice(h * HEAD_DIM, (h + 1) * HEAD_DIM)
    slab = lambda i: slice(i * PROJ_SLAB, (i + 1) * PROJ_SLAB)
    n_slabs = D_MODEL // PROJ_SLAB

    def project(first_col, i):
        cols = slice(first_col + i * PROJ_SLAB, first_col + (i + 1) * PROJ_SLAB)
        return jnp.dot(xb_ref[...], w_ref[:, cols], preferred_element_type=F32)

    rope = lambda t: t * cos_ref[...] + pltpu.roll(t, HEAD_DIM // 2, 1) * sin_ref[...]


    def hgrn_silu(i):
        def run():
            acc = project(COL_HGRN_Q if i < n_slabs else COL_HGRN_GATE, i % n_slabs)
            dst = hq_ref if i < n_slabs else hog_ref
            dst[:, slab(i % n_slabs)] = (acc * _sigmoid(acc)).astype(dst.dtype)
        return run

    def hgrn_forget(i):
        def run():
            l0, l1 = lbl_ref[0:1, slab(i)], lbl_ref[1:2, slab(i)]
            top = jnp.maximum(l0, l1)
            e0, e1 = jnp.exp(l0 - top), jnp.exp(l1 - top)
            lb = e0 / (e0 + e1)
            z = project(COL_HGRN_F, i)
            hlf_ref[:, slab(i)] = jnp.log(lb + (1.0 - lb) * _sigmoid(z)) * LOG2_E
            hk_ref[:, slab(i)] = ((1.0 - lb) * _sigmoid(-z)).astype(hk_ref.dtype)
        return run

    def hgrn_value(i):
        def run():
            hv_ref[:, slab(i)] = project(COL_HGRN_V, i).astype(hv_ref.dtype)
        return run

    heads_of = lambda i: range(i * PROJ_SLAB // HEAD_DIM, (i + 1) * PROJ_SLAB // HEAD_DIM)
    in_slab = lambda i, h: slice((h - heads_of(i)[0]) * HEAD_DIM, (h - heads_of(i)[0] + 1) * HEAD_DIM)

    def moba_query(i):
        def run():
            acc = project(COL_MOBA_Q, i)
            for h in heads_of(i):
                qh = rope(acc[:, in_slab(i, h)])
                qf_ref[:, head(h)] = qh
                q_ref[:, head(h)] = (qh * (HEAD_DIM ** -0.5 * LOG2_E)).astype(q_ref.dtype)
        return run

    def moba_choice():
        split = lambda t: (t.astype(BF16), (t - t.astype(BF16).astype(F32)).astype(BF16))
        q_hi, q_lo = split(qf_ref[...])
        km_hi, km_lo = split(km_ref[...])
        gate = lax.dot_general(jnp.concatenate([q_hi, q_lo, q_hi], axis=1),
                               jnp.concatenate([km_hi, km_hi, km_lo], axis=1), NT_DIMS,
                               preferred_element_type=F32)
        lane = lax.broadcasted_iota(jnp.int32, (MOBA_BLOCK, lanes), 1)
        eligible = lane < own * HEADS
        gate = jnp.where(eligible, gate, -jnp.inf)
        rank = jnp.zeros((MOBA_BLOCK, lanes), F32)
        for r in range(1, MOBA_NBLK):
            partner = pltpu.roll(gate, lanes - r * HEADS, 1)
            wrapped = lane >= lanes - r * HEADS
            beats = (partner > gate) | (wrapped & (partner == gate))
            rank = rank + beats.astype(F32)
        selected = eligible & (rank < float(MOBA_TOPK))
        bias_ref[0] = jnp.where(selected, 0.0, MASK_VALUE).T

    def moba_key(i):
        def run():
            acc = project(COL_MOBA_K, i)
            for h in heads_of(i):
                rot = rope(acc[:, in_slab(i, h)])
                k_ref[:, head(h)] = rot.astype(k_ref.dtype)
                mean = jnp.mean(rot, axis=0, keepdims=True)
                zeros = lambda n: [jnp.zeros((1, n * HEAD_DIM), F32)] if n else []
                km_ref[pl.ds(own * HEADS + h, 1), :] = jnp.concatenate(
                    zeros(h) + [mean] + zeros(HEADS - 1 - h), axis=1)
        return run

    def moba_value(i):
        def run():
            vt = project(COL_MOBA_V, i).T.astype(vt_ref.dtype)
            pad_row = lax.broadcasted_iota(jnp.int32, (VT_ROWS - HEAD_DIM, MOBA_BLOCK), 0)
            ones_row = jnp.where(pad_row == 0, 1.0, 0.0).astype(vt_ref.dtype)
            for h in heads_of(i):
                vt_ref[0, 0, h * VT_ROWS:h * VT_ROWS + HEAD_DIM, :] = vt[in_slab(i, h), :]
                vt_ref[0, 0, h * VT_ROWS + HEAD_DIM:(h + 1) * VT_ROWS, :] = ones_row
        return run

    def merge_gate(i):
        def run():
            acc = project(COL_MERGE_GATE, i) + bg_ref[:, slab(i)]
            gate_ref[:, slab(i)] = _sigmoid(acc).astype(gate_ref.dtype)
        return run

    hgrn_inputs = ([hgrn_silu(i) for i in range(2 * n_slabs)]
                   + [hgrn_forget(i) for i in range(n_slabs)]
                   + [hgrn_value(i) for i in range(n_slabs)])
    queries = [moba_query(i) for i in range(n_slabs)] + [moba_choice]
    for i, piece in enumerate(hgrn_inputs):
        piece()
        if i < len(queries):
            queries[i]()

    rest = ([moba_key(i) for i in range(n_slabs)] + [moba_value(i) for i in range(n_slabs)]
            + [merge_gate(i) for i in range(2 * n_slabs)])
    calls = []

    def fill():
        calls.append(None)
        if rest and len(calls) % FILL_EVERY == 0:
            rest.pop(0)()

    _hgrn_tile(hq_ref, hlf_ref, hk_ref, hv_ref, hog_ref, nw_ref, pm_ref, mask_ref, ya_ref,
               state_ref, fill)
    while rest:
        fill()


def _proj(x, w_in, rope_freq, b_gate, lb_logits, norm_w):
    d = D_MODEL
    lanes = HEADS * MOBA_NBLK
    pm, masks = _hgrn_constants()
    pm = np.tile(pm, (1, 3))
    row = lambda n: pl.BlockSpec((MOBA_BLOCK, n), lambda b, i: (b * MOBA_NBLK + i, 0))
    shape = lambda n, dt: jax.ShapeDtypeStruct((M_TOKENS, n), dt)
    tile = lambda dt: pltpu.VMEM((MOBA_BLOCK, d), dt)
    return pl.pallas_call(
        _proj_kernel,
        grid=(BATCH, MOBA_NBLK),
        in_specs=[row(d), _resident((d, N_PROJ)), _resident((1, HEAD_DIM)), _resident((1, 2 * d)), _resident((DEPTH + 1, d)),
                  _resident((1, d)), _resident(pm.shape), _resident(masks.shape)],
        out_specs=[row(d),
                   pl.BlockSpec((1, lanes, MOBA_BLOCK), lambda b, i: (b, 0, i)),
                   row(d),
                   pl.BlockSpec((1, 1, HEADS * VT_ROWS, MOBA_BLOCK), lambda b, i: (b, i, 0, 0)),
                   row(2 * d), row(d)],
        out_shape=[shape(d, BF16),
                   jax.ShapeDtypeStruct((BATCH, lanes, SEQ), F32),
                   shape(d, BF16),
                   jax.ShapeDtypeStruct((BATCH, MOBA_NBLK, HEADS * VT_ROWS, MOBA_BLOCK), BF16),
                   shape(2 * d, BF16), shape(d, BF16)],
        scratch_shapes=[pltpu.VMEM((lanes, d), F32),
                        pltpu.VMEM((HEADS, HEAD_DIM, HEAD_DIM), F32),
                        tile(BF16), tile(BF16), tile(F32), tile(BF16), tile(BF16),
                        tile(BF16), tile(F32),
                        pltpu.VMEM((SEQ, HEAD_DIM), F32), pltpu.VMEM((SEQ, HEAD_DIM), F32)],
        compiler_params=_params(("arbitrary", "arbitrary")),
        name="proj",
    )(x, w_in, rope_freq, b_gate, lb_logits, norm_w,
      jnp.asarray(pm, BF16), jnp.asarray(masks, BF16))


def _moba_att_kernel(q_ref, qn_ref, k_ref, vt_ref, bias_ref, o_ref, acc_ref, s_ref, qt_ref):
    own = pl.program_id(1)

    def head(h):
        return slice(h * HEAD_DIM, (h + 1) * HEAD_DIM)

    for h in range(HEADS):
        qt_ref[0, head(h), :] = q_ref[:, head(h)].astype(F32).T.astype(BF16)
        qt_ref[1, head(h), :] = qn_ref[:, head(h)].astype(F32).T.astype(BF16)

    def scores(h, j):
        kj = k_ref[pl.ds(pl.multiple_of(j * MOBA_BLOCK, MOBA_BLOCK), MOBA_BLOCK), head(h)]
        return jnp.dot(kj, qt_ref[0, head(h), :], preferred_element_type=F32)

    def item(h, j, s, m, l, next_scores, bias=None):
        top = jnp.max(s, axis=0, keepdims=True)
        m_new = jnp.maximum(m, top if bias is None else top + bias)
        a = jnp.exp2(m - m_new)
        p = jnp.exp2(s - (m_new if bias is None else m_new - 2.0 * bias))
        if next_scores is not None:
            s_ref[h] = next_scores()
        pv = jnp.dot(vt_ref[0, j, h * VT_ROWS:(h + 1) * VT_ROWS, :], p.astype(BF16),
                     preferred_element_type=F32)
        acc_ref[h] = a * acc_ref[h] + pv[:HEAD_DIM]
        return m_new, a * l + pv[HEAD_DIM:HEAD_DIM + 1]

    acc_ref[...] = jnp.zeros_like(acc_ref)

    @pl.when(own == 0)
    def _():
        for h in range(HEADS):
            s_ref[h] = scores(h, 0)

    m0 = jnp.full((1, MOBA_BLOCK), MASK_VALUE, F32)
    l0 = jnp.zeros((1, MOBA_BLOCK), F32)

    def body(j, carry):
        ms, ls = list(carry[0]), list(carry[1])
        for h in range(HEADS):
            ms[h], ls[h] = item(h, j, s_ref[h], ms[h], ls[h], lambda h=h: scores(h, j + 1),
                                bias_ref[0, pl.ds(j * HEADS + h, 1), :])
        return tuple(ms), tuple(ls)

    def several(n):
        def run(t, carry):
            for r in range(n):
                carry = body(n * t + r, carry)
            return carry
        return run

    carry = ((m0,) * HEADS, (l0,) * HEADS)
    done = 0
    for n in (8, 4, 2, 1):
        trips = (own - done) // n
        carry = lax.fori_loop(done // n, done // n + trips, several(n), carry)
        done = done + trips * n
    ms, ls = carry

    key_pos = lax.broadcasted_iota(jnp.int32, (MOBA_BLOCK, MOBA_BLOCK), 0)
    qry_pos = lax.broadcasted_iota(jnp.int32, (MOBA_BLOCK, MOBA_BLOCK), 1)
    causal = key_pos <= qry_pos
    def next_block_scores(h):
        return jnp.dot(k_ref[0:MOBA_BLOCK, head(h)], qt_ref[1, head(h), :],
                       preferred_element_type=F32)

    for h in range(HEADS):
        s = jnp.where(causal, s_ref[h], MASK_VALUE)
        _, l = item(h, own, s, ms[h], ls[h], lambda h=h: next_block_scores(h))
        o_ref[:, head(h)] = (acc_ref[h] * (1.0 / l)).T.astype(o_ref.dtype)


def _moba_attention(q_scaled, k_rot, v_t, bias):
    lanes = HEADS * MOBA_NBLK
    return pl.pallas_call(
        _moba_att_kernel,
        grid=(BATCH, MOBA_NBLK),
        in_specs=[pl.BlockSpec((MOBA_BLOCK, D_MODEL), lambda b, i: (b * MOBA_NBLK + i, 0)),
                  pl.BlockSpec((MOBA_BLOCK, D_MODEL),
                               lambda b, i: (b * MOBA_NBLK + jnp.minimum(i + 1, MOBA_NBLK - 1), 0)),
                  pl.BlockSpec((SEQ, D_MODEL), lambda b, i: (b, 0)),
                  pl.BlockSpec((1, MOBA_NBLK, HEADS * VT_ROWS, MOBA_BLOCK),
                               lambda b, i: (b, 0, 0, 0)),
                  pl.BlockSpec((1, lanes, MOBA_BLOCK), lambda b, i: (b, 0, i))],
        out_specs=pl.BlockSpec((MOBA_BLOCK, D_MODEL), lambda b, i: (b * MOBA_NBLK + i, 0)),
        out_shape=jax.ShapeDtypeStruct((M_TOKENS, D_MODEL), BF16),
        scratch_shapes=[pltpu.VMEM((HEADS, HEAD_DIM, MOBA_BLOCK), F32),
                        pltpu.VMEM((HEADS, MOBA_BLOCK, MOBA_BLOCK), F32),
                        pltpu.VMEM((2, D_MODEL, MOBA_BLOCK), BF16)],
        compiler_params=_params(("arbitrary", "arbitrary")),
        name="moba_att",
    )(q_scaled, q_scaled, k_rot, v_t, bias)


HGRN_TILE = MOBA_BLOCK
PM_LEVELS = (1, 2)
HGRN_GROUP = 4


def _hgrn_constants():
    c = HGRN_CHUNK
    idx = np.arange(c)
    tri = (idx[None, :] <= idx[:, None]).astype(np.float32)
    rows = [tri]
    masks = []
    for lvl in range(HGRN_LEVELS):
        half = 1 << lvl
        group = idx // (2 * half)
        upper = (idx & half) != 0
        if lvl in PM_LEVELS:
            boundary = tri[group * (2 * half) + half - 1]
            rows.append(np.where(upper[:, None], tri - boundary, boundary - tri))
        masks.append(((group[:, None] == group[None, :]) & upper[:, None] & ~upper[None, :])
                     .astype(np.float32))
    masks.append(np.eye(c, dtype=np.float32))
    return np.concatenate(rows, axis=0), np.stack(masks)


def _hgrn_tile(q_ref, lf_ref, k_ref, v_ref, og_ref, nw_ref, pm_ref, mask_ref, o_ref, state_ref,
               fill):
    c = HGRN_CHUNK
    n_ch = HGRN_TILE // c
    units = [(ch, h) for ch in range(n_ch) for h in range(HEADS)]
    rows = lambda ch: slice(ch * c, (ch + 1) * c)
    lanes = lambda h: slice(h * HEAD_DIM, (h + 1) * HEAD_DIM)
    chunks_on_lanes = lambda t: jnp.concatenate([t[rows(ch), :] for ch in range(n_ch)], axis=1)

    lf = lf_ref[...]
    hi = lf.astype(BF16)
    rem = lf - hi.astype(F32)
    mid = rem.astype(BF16)
    lo = (rem - mid.astype(F32)).astype(BF16)
    split = jnp.concatenate([chunks_on_lanes(hi), chunks_on_lanes(mid), chunks_on_lanes(lo)], axis=0)
    sums = jnp.dot(pm_ref[...], split, preferred_element_type=F32)

    row = lax.broadcasted_iota(jnp.int32, (c, HEAD_DIM), 0)
    upper = [(row & (1 << lvl)) != 0 for lvl in range(HGRN_LEVELS)]
    t_idx = lax.broadcasted_iota(jnp.int32, (c, c), 0)
    s_idx = lax.broadcasted_iota(jnp.int32, (c, c), 1)
    on_diagonal = (t_idx == s_idx).astype(F32)
    below_diagonal = ((t_idx == s_idx + 1) & ((t_idx & 1) != 0)).astype(F32)

    def decayed_operands(u, ch, h):
        col = slice(u * HEAD_DIM, (u + 1) * HEAD_DIM)
        g = sums[0:c, col]
        q_bf, k_bf = q_ref[rows(ch), lanes(h)], k_ref[rows(ch), lanes(h)]
        qf, kk = q_bf.astype(F32), k_bf.astype(F32)
        same = jnp.sum(qf * kk, axis=-1, keepdims=True)
        q_dec = jnp.where(upper[0], qf * jnp.exp2(lf_ref[rows(ch), lanes(h)]), 0.0)
        prev = jnp.sum(q_dec * pltpu.roll(kk, 1, 0), axis=-1, keepdims=True)
        near = (same * on_diagonal + prev * below_diagonal).astype(BF16)
        zs = []
        for lvl in range(1, HGRN_LEVELS):
            half = 1 << lvl
            if lvl in PM_LEVELS:
                i = PM_LEVELS.index(lvl)
                e = sums[(i + 1) * c:(i + 2) * c, col]
            else:
                boundary = jnp.concatenate(
                    [jnp.broadcast_to(g[b + half - 1:b + half, :], (2 * half, HEAD_DIM))
                     for b in range(0, c, 2 * half)], axis=0)
                e = -jnp.abs(g - boundary)
            zs.append((jnp.where(upper[lvl], qf, kk) * jnp.exp2(e)).astype(BF16))
        g_last = g[c - 1:c, :]
        return dict(near=near, zs=zs,
                    q_in=(qf * jnp.exp2(g)).astype(BF16),
                    k_out=(kk * jnp.exp2(g_last - g)).astype(BF16),
                    decay=jnp.exp2(g_last),
                    v=v_ref[rows(ch), lanes(h)])

    def state_free_matmuls(op):
        score = lambda lhs, rhs: lax.dot_general(lhs, rhs, NT_DIMS,
                                                 preferred_element_type=F32).astype(BF16)
        a = op["near"]
        for lvl, z in enumerate(op["zs"], start=1):
            a = a + score(z, z) * mask_ref[lvl]
        op["o"] = jnp.dot(a, op["v"], preferred_element_type=F32)
        op["update"] = lax.dot_general(op["v"], op["k_out"], TN_DIMS, preferred_element_type=F32)

    for first_head in range(0, HEADS, HGRN_GROUP):
        heads = range(first_head, first_head + HGRN_GROUP)
        group = [(ch, h) for ch in range(n_ch) for h in heads]
        ops = {}
        for i, (ch, h) in enumerate(group):
            ops[ch, h] = decayed_operands(ch * HEADS + h, ch, h)
            if i >= 1:
                state_free_matmuls(ops[group[i - 1]])
            fill()
        state_free_matmuls(ops[group[-1]])

        for h in heads:
            state_t = state_ref[h]
            for ch in range(n_ch):
                op = ops[ch, h]
                op["state"] = state_t.T.astype(BF16)
                state_t = state_t * op["decay"] + op["update"]
            state_ref[h] = state_t
        for ch, h in group:
            op = ops[ch, h]
            o = op["o"] + jnp.dot(op["q_in"], op["state"], preferred_element_type=F32)
            ms = jnp.mean(o * o, axis=-1, keepdims=True)
            y = o * lax.rsqrt(ms + RMS_EPS) * nw_ref[:, lanes(h)] * og_ref[rows(ch), lanes(h)]
            o_ref[rows(ch), lanes(h)] = y.astype(o_ref.dtype)
            fill()


MIX_TM = 512


def _mix_kernel(ya_ref, yb_ref, gate_ref, x_ref, wa_ref, wb_ref, wo_ref, lnw_ref, lnb_ref,
                o_ref, ob_ref):
    half = MIX_TM // 2
    first, second = slice(0, half), slice(half, MIX_TM)

    def merged(rows, za):
        zb = jnp.dot(yb_ref[rows, :], wb_ref[...], preferred_element_type=F32)
        return (gate_ref[rows, :D_MODEL] * za + gate_ref[rows, D_MODEL:] * zb).astype(BF16)

    def pre_norm(rows, za):
        mixed = jnp.dot(merged(rows, za), wo_ref[...], preferred_element_type=F32)
        return DN_ALPHA * x_ref[rows, :] + mixed

    def finish(rows, r):
        y = _layer_norm(r, lnw_ref[...], lnb_ref[...])
        o_ref[rows, :] = y
        ob_ref[rows, :] = y.astype(BF16)

    branch_a = lambda rows: jnp.dot(ya_ref[rows, :], wa_ref[...], preferred_element_type=F32)
    r_first = pre_norm(first, branch_a(first))
    za_second = branch_a(second)
    finish(first, r_first)
    finish(second, pre_norm(second, za_second))


def _mix(ya, yb, gates, x, wa, wb, wo, lnw, lnb):
    d = D_MODEL
    row = lambda n: pl.BlockSpec((MIX_TM, n), lambda i: (i, 0))
    return pl.pallas_call(
        _mix_kernel,
        grid=(M_TOKENS // MIX_TM,),
        in_specs=[row(d), row(d), row(2 * d), row(d),
                  _resident((d, d)), _resident((d, d)), _resident((d, d)),
                  _resident((1, d)), _resident((1, d))],
        out_specs=[row(d), row(d)],
        out_shape=[jax.ShapeDtypeStruct((M_TOKENS, d), F32),
                   jax.ShapeDtypeStruct((M_TOKENS, d), BF16)],
        compiler_params=_params(("parallel",)),
        name="mix",
    )(ya, yb, gates, x, wa, wb, wo, lnw, lnb)


FFN_TM = 512
FFN_CHUNK = 1408


def _ffn_kernel(xb_ref, x_ref, win_ref, wd_ref, lnw_ref, lnb_ref, o_ref):
    half = FFN_TM // 2
    n_chunks = D_FF // FFN_CHUNK

    def chunk(rows, c):
        cs = slice(c * FFN_CHUNK, (c + 1) * FFN_CHUNK)
        xb = xb_ref[rows, :]
        up = slice(D_FF + c * FFN_CHUNK, D_FF + (c + 1) * FFN_CHUNK)
        hg = jnp.dot(xb, win_ref[:, cs], preferred_element_type=F32)
        hu = jnp.dot(xb, win_ref[:, up], preferred_element_type=F32)
        act = (hg * _sigmoid(hg) * hu).astype(BF16)
        return jnp.dot(act, wd_ref[cs, :], preferred_element_type=F32)

    def finish(rows, y):
        o_ref[rows, :] = _layer_norm(DN_ALPHA * x_ref[rows, :] + y, lnw_ref[...], lnb_ref[...])

    first, second = slice(0, half), slice(half, FFN_TM)
    y_first = sum(chunk(first, c) for c in range(n_chunks))
    y_second = chunk(second, 0)
    finish(first, y_first)
    for c in range(1, n_chunks):
        y_second = y_second + chunk(second, c)
    finish(second, y_second)


def _ffn(xb, x, w_in, wd, lnw, lnb):
    d = D_MODEL
    row = lambda n: pl.BlockSpec((FFN_TM, n), lambda i: (i, 0))
    return pl.pallas_call(
        _ffn_kernel,
        grid=(M_TOKENS // FFN_TM,),
        in_specs=[row(d), row(d), _resident((d, 2 * D_FF)),
                  _resident((D_FF, d)), _resident((1, d)), _resident((1, d))],
        out_specs=row(d),
        out_shape=jax.ShapeDtypeStruct((M_TOKENS, d), F32),
        compiler_params=_params(("parallel",)),
        name="ffn",
    )(xb, x, w_in, wd, lnw, lnb)


def _rope_frequencies():
    half = HEAD_DIM // 2
    inv_freq = ROPE_THETA ** (-jnp.arange(half, dtype=F32) / half)
    return jnp.concatenate([inv_freq, inv_freq]).reshape(1, HEAD_DIM)


def _layer(x2, w_in, lb_logits, hgrn_norm_w, w_branch_a, w_branch_b, b_gate, w_out,
           ln1_w, ln1_b, w_ffn_in, w_ffn_down, ln2_w, ln2_b):
    d = D_MODEL
    row2 = lambda v: v.reshape(1, -1).astype(F32)

    q_scaled, bias, k_rot, v_t, gates, y_a = _proj(
        x2, w_in.astype(BF16), _rope_frequencies(), row2(b_gate), lb_logits.astype(F32),
        row2(hgrn_norm_w))
    y_b = _moba_attention(q_scaled, k_rot, v_t, bias)

    x1, x1b = _mix(y_a, y_b, gates, x2, w_branch_a.astype(BF16), w_branch_b.astype(BF16),
                   w_out.astype(BF16), row2(ln1_w), row2(ln1_b))
    return _ffn(x1b, x1, w_ffn_in.astype(BF16), w_ffn_down.astype(BF16), row2(ln2_w), row2(ln2_b))


def kernel(x, w_in, lb_logits, hgrn_norm_w, w_branch_a, w_branch_b, b_gate, w_out, ln1_w, ln1_b,
           w_ffn_in, w_ffn_down, ln2_w, ln2_b):
    assert DEPTH == 1
    h = x.reshape(M_TOKENS, D_MODEL)
    h = _layer(h, w_in[0], lb_logits, hgrn_norm_w[0], w_branch_a[0], w_branch_b[0],
               b_gate[0], w_out[0], ln1_w[0], ln1_b[0], w_ffn_in[0], w_ffn_down[0],
               ln2_w[0], ln2_b[0])
    return h.reshape(BATCH, SEQ, D_MODEL)
```

```python
import numpy as np
import jax
import jax.numpy as jnp
from jax import lax
from jax.experimental import pallas as pl
from jax.experimental.pallas import tpu as pltpu

D_MODEL = 1024
BATCH = 4
SEQ = 4096
DEPTH = 1
HEADS = 8
HEAD_DIM = 128
HGRN_CHUNK = 64
HGRN_LEVELS = 6
MOBA_BLOCK = 256
MOBA_NBLK = SEQ // MOBA_BLOCK
MOBA_TOPK = 3
ROPE_THETA = 10000.0
D_FF = 2816
DN_ALPHA = (2.0 * DEPTH) ** 0.25
LN_EPS = 1e-5
RMS_EPS = 1e-6
M_TOKENS = BATCH * SEQ

MASK_VALUE = -1e30
VT_ROWS = HEAD_DIM + 16
LOG2_E = 1.4426950408889634
VMEM_LIMIT = 56 * 1024 * 1024

F32 = jnp.float32
BF16 = jnp.bfloat16
NT_DIMS = (((1,), (1,)), ((), ()))
TN_DIMS = (((0,), (0,)), ((), ()))


def _params(semantics):
    return pltpu.CompilerParams(dimension_semantics=semantics, vmem_limit_bytes=VMEM_LIMIT)


def _resident(shape):
    return pl.BlockSpec(shape, lambda *_: (0,) * len(shape), pipeline_mode=pl.Buffered(1))


def _sigmoid(z):
    return 1.0 / (1.0 + jnp.exp(-z))


def _layer_norm(r, w, b):
    mu = jnp.mean(r, axis=-1, keepdims=True)
    d = r - mu
    var = jnp.mean(d * d, axis=-1, keepdims=True)
    return d * lax.rsqrt(var + LN_EPS) * w + b


PROJ_SLAB = 256
FILL_EVERY = 4
N_PROJ = 9 * D_MODEL
COL_HGRN_Q, COL_HGRN_F, COL_HGRN_V, COL_HGRN_GATE = 0, D_MODEL, 2 * D_MODEL, 3 * D_MODEL
COL_MOBA_Q, COL_MOBA_K, COL_MOBA_V, COL_MERGE_GATE = 4 * D_MODEL, 5 * D_MODEL, 6 * D_MODEL, 7 * D_MODEL


def _proj_kernel(x_ref, w_ref, freq_ref,
                 bg_ref, lbl_ref, nw_ref, pm_ref, mask_ref,
                 q_ref, bias_ref, k_ref, vt_ref, gate_ref, ya_ref,
                 km_ref, state_ref, hq_ref, hog_ref, hlf_ref, hk_ref, hv_ref, xb_ref, qf_ref,
                 cos_all_ref, sin_all_ref):
    own = pl.program_id(1)
    lanes = HEADS * MOBA_NBLK

    @pl.when(own == 0)
    def _():
        km_ref[...] = jnp.zeros_like(km_ref)
        state_ref[...] = jnp.zeros_like(state_ref)

    xb_ref[...] = x_ref[...].astype(BF16)
    block_rows = pl.ds(pl.multiple_of(own * MOBA_BLOCK, MOBA_BLOCK), MOBA_BLOCK)

    @pl.when(pl.program_id(0) == 0)
    def _():
        position = own * MOBA_BLOCK + lax.broadcasted_iota(jnp.int32, (MOBA_BLOCK, HEAD_DIM), 0)
        angle = position.astype(F32) * freq_ref[...]
        first_half = lax.broadcasted_iota(jnp.int32, (MOBA_BLOCK, HEAD_DIM), 1) < HEAD_DIM // 2
        cos_all_ref[block_rows, :] = jnp.cos(angle)
        sin_all_ref[block_rows, :] = jnp.where(first_half, -jnp.sin(angle), jnp.sin(angle))

    cos_ref = cos_all_ref.at[block_rows, :]
    sin_ref = sin_all_ref.at[block_rows, :]
    head = lambda h: slice(h * HEAD_DIM, (h + 1) * HEAD_DIM)
    slab = lambda i: slice(i * PROJ_SLAB, (i + 1) * PROJ_SLAB)
    n_slabs = D_MODEL // PROJ_SLAB

    def project(first_col, i):
        cols = slice(first_col + i * PROJ_SLAB, first_col + (i + 1) * PROJ_SLAB)
        return jnp.dot(xb_ref[...], w_ref[:, cols], preferred_element_type=F32)

    rope = lambda t: t * cos_ref[...] + pltpu.roll(t, HEAD_DIM // 2, 1) * sin_ref[...]


    def hgrn_silu(i):
        def run():
            acc = project(COL_HGRN_Q if i < n_slabs else COL_HGRN_GATE, i % n_slabs)
            dst = hq_ref if i < n_slabs else hog_ref
            dst[:, slab(i % n_slabs)] = (acc * _sigmoid(acc)).astype(dst.dtype)
        return run

    def hgrn_forget(i):
        def run():
            l0, l1 = lbl_ref[0:1, slab(i)], lbl_ref[1:2, slab(i)]
            top = jnp.maximum(l0, l1)
            e0, e1 = jnp.exp(l0 - top), jnp.exp(l1 - top)
            lb = e0 / (e0 + e1)
            z = project(COL_HGRN_F, i)
            hlf_ref[:, slab(i)] = jnp.log(lb + (1.0 - lb) * _sigmoid(z)) * LOG2_E
            hk_ref[:, slab(i)] = ((1.0 - lb) * _sigmoid(-z)).astype(hk_ref.dtype)
        return run

    def hgrn_value(i):
        def run():
            hv_ref[:, slab(i)] = project(COL_HGRN_V, i).astype(hv_ref.dtype)
        return run

    heads_of = lambda i: range(i * PROJ_SLAB // HEAD_DIM, (i + 1) * PROJ_SLAB // HEAD_DIM)
    in_slab = lambda i, h: slice((h - heads_of(i)[0]) * HEAD_DIM, (h - heads_of(i)[0] + 1) * HEAD_DIM)

    def moba_query(i):
        def run():
            acc = project(COL_MOBA_Q, i)
            for h in heads_of(i):
                qh = rope(acc[:, in_slab(i, h)])
                qf_ref[:, head(h)] = qh
                q_ref[:, head(h)] = (qh * (HEAD_DIM ** -0.5 * LOG2_E)).astype(q_ref.dtype)
        return run

    def moba_choice():
        split = lambda t: (t.astype(BF16), (t - t.astype(BF16).astype(F32)).astype(BF16))
        q_hi, q_lo = split(qf_ref[...])
        km_hi, km_lo = split(km_ref[...])
        gate = lax.dot_general(jnp.concatenate([q_hi, q_lo, q_hi], axis=1),
                               jnp.concatenate([km_hi, km_hi, km_lo], axis=1), NT_DIMS,
                               preferred_element_type=F32)
        lane = lax.broadcasted_iota(jnp.int32, (MOBA_BLOCK, lanes), 1)
        eligible = lane < own * HEADS
        gate = jnp.where(eligible, gate, -jnp.inf)
        rank = jnp.zeros((MOBA_BLOCK, lanes), F32)
        for r in range(1, MOBA_NBLK):
            partner = pltpu.roll(gate, lanes - r * HEADS, 1)
            wrapped = lane >= lanes - r * HEADS
            beats = (partner > gate) | (wrapped & (partner == gate))
            rank = rank + beats.astype(F32)
        selected = eligible & (rank < float(MOBA_TOPK))
        bias_ref[0] = jnp.where(selected, 0.0, MASK_VALUE).T

    def moba_key(i):
        def run():
            acc = project(COL_MOBA_K, i)
            for h in heads_of(i):
                rot = rope(acc[:, in_slab(i, h)])
                k_ref[:, head(h)] = rot.astype(k_ref.dtype)
                mean = jnp.mean(rot, axis=0, keepdims=True)
                zeros = lambda n: [jnp.zeros((1, n * HEAD_DIM), F32)] if n else []
                km_ref[pl.ds(own * HEADS + h, 1), :] = jnp.concatenate(
                    zeros(h) + [mean] + zeros(HEADS - 1 - h), axis=1)
        return run

    def moba_value(i):
        def run():
            vt = project(COL_MOBA_V, i).T.astype(vt_ref.dtype)
            pad_row = lax.broadcasted_iota(jnp.int32, (VT_ROWS - HEAD_DIM, MOBA_BLOCK), 0)
            ones_row = jnp.where(pad_row == 0, 1.0, 0.0).astype(vt_ref.dtype)
            for h in heads_of(i):
                vt_ref[0, 0, h * VT_ROWS:h * VT_ROWS + HEAD_DIM, :] = vt[in_slab(i, h), :]
                vt_ref[0, 0, h * VT_ROWS + HEAD_DIM:(h + 1) * VT_ROWS, :] = ones_row
        return run

    def merge_gate(i):
        def run():
            acc = project(COL_MERGE_GATE, i) + bg_ref[:, slab(i)]
            gate_ref[:, slab(i)] = _sigmoid(acc).astype(gate_ref.dtype)
        return run

    hgrn_inputs = ([hgrn_silu(i) for i in range(2 * n_slabs)]
                   + [hgrn_forget(i) for i in range(n_slabs)]
                   + [hgrn_value(i) for i in range(n_slabs)])
    queries = [moba_query(i) for i in range(n_slabs)] + [moba_choice]
    for i, piece in enumerate(hgrn_inputs):
        piece()
        if i < len(queries):
            queries[i]()

    rest = ([moba_key(i) for i in range(n_slabs)] + [moba_value(i) for i in range(n_slabs)]
            + [merge_gate(i) for i in range(2 * n_slabs)])
    calls = []

    def fill():
        calls.append(None)
        if rest and len(calls) % FILL_EVERY == 0:
            rest.pop(0)()

    _hgrn_tile(hq_ref, hlf_ref, hk_ref, hv_ref, hog_ref, nw_ref, pm_ref, mask_ref, ya_ref,
               state_ref, fill)
    while rest:
        fill()


def _proj(x, w_in, rope_freq, b_gate, lb_logits, norm_w):
    d = D_MODEL
    lanes = HEADS * MOBA_NBLK
    pm, masks = _hgrn_constants()
    pm = np.tile(pm, (1, 3))
    row = lambda n: pl.BlockSpec((MOBA_BLOCK, n), lambda b, i: (b * MOBA_NBLK + i, 0))
    shape = lambda n, dt: jax.ShapeDtypeStruct((M_TOKENS, n), dt)
    tile = lambda dt: pltpu.VMEM((MOBA_BLOCK, d), dt)
    return pl.pallas_call(
        _proj_kernel,
        grid=(BATCH, MOBA_NBLK),
        in_specs=[row(d), _resident((d, N_PROJ)), _resident((1, HEAD_DIM)), _resident((1, 2 * d)), _resident((DEPTH + 1, d)),
                  _resident((1, d)), _resident(pm.shape), _resident(masks.shape)],
        out_specs=[row(d),
                   pl.BlockSpec((1, lanes, MOBA_BLOCK), lambda b, i: (b, 0, i)),
                   row(d),
                   pl.BlockSpec((1, 1, HEADS * VT_ROWS, MOBA_BLOCK), lambda b, i: (b, i, 0, 0)),
                   row(2 * d), row(d)],
        out_shape=[shape(d, BF16),
                   jax.ShapeDtypeStruct((BATCH, lanes, SEQ), F32),
                   shape(d, BF16),
                   jax.ShapeDtypeStruct((BATCH, MOBA_NBLK, HEADS * VT_ROWS, MOBA_BLOCK), BF16),
                   shape(2 * d, BF16), shape(d, BF16)],
        scratch_shapes=[pltpu.VMEM((lanes, d), F32),
                        pltpu.VMEM((HEADS, HEAD_DIM, HEAD_DIM), F32),
                        tile(BF16), tile(BF16), tile(F32), tile(BF16), tile(BF16),
                        tile(BF16), tile(F32),
                        pltpu.VMEM((SEQ, HEAD_DIM), F32), pltpu.VMEM((SEQ, HEAD_DIM), F32)],
        compiler_params=_params(("arbitrary", "arbitrary")),
        name="proj",
    )(x, w_in, rope_freq, b_gate, lb_logits, norm_w,
      jnp.asarray(pm, BF16), jnp.asarray(masks, BF16))


def _moba_att_kernel(q_ref, qn_ref, k_ref, vt_ref, bias_ref, o_ref, acc_ref, s_ref):
    own = pl.program_id(1)

    def head(h):
        return slice(h * HEAD_DIM, (h + 1) * HEAD_DIM)

    def scores(h, j):
        kj = k_ref[pl.ds(pl.multiple_of(j * MOBA_BLOCK, MOBA_BLOCK), MOBA_BLOCK), head(h)]
        return lax.dot_general(kj, q_ref[:, head(h)], NT_DIMS, preferred_element_type=F32)

    def item(h, j, s, m, l, next_scores, bias=None):
        top = jnp.max(s, axis=0, keepdims=True)
        m_new = jnp.maximum(m, top if bias is None else top + bias)
        a = jnp.exp2(m - m_new)
        p = jnp.exp2(s - (m_new if bias is None else m_new - 2.0 * bias))
        if next_scores is not None:
            s_ref[h] = next_scores()
        pv = jnp.dot(vt_ref[0, j, h * VT_ROWS:(h + 1) * VT_ROWS, :], p.astype(BF16),
                     preferred_element_type=F32)
        acc_ref[h] = a * acc_ref[h] + pv[:HEAD_DIM]
        return m_new, a * l + pv[HEAD_DIM:HEAD_DIM + 1]

    acc_ref[...] = jnp.zeros_like(acc_ref)

    @pl.when(own == 0)
    def _():
        for h in range(HEADS):
            s_ref[h] = scores(h, 0)

    m0 = jnp.full((1, MOBA_BLOCK), MASK_VALUE, F32)
    l0 = jnp.zeros((1, MOBA_BLOCK), F32)

    def body(j, carry):
        ms, ls = list(carry[0]), list(carry[1])
        for h in range(HEADS):
            ms[h], ls[h] = item(h, j, s_ref[h], ms[h], ls[h], lambda h=h: scores(h, j + 1),
                                bias_ref[0, pl.ds(j * HEADS + h, 1), :])
        return tuple(ms), tuple(ls)

    def several(n):
        def run(t, carry):
            for r in range(n):
                carry = body(n * t + r, carry)
            return carry
        return run

    carry = ((m0,) * HEADS, (l0,) * HEADS)
    done = 0
    for n in (8, 4, 2, 1):
        trips = (own - done) // n
        carry = lax.fori_loop(done // n, done // n + trips, several(n), carry)
        done = done + trips * n
    ms, ls = carry

    key_pos = lax.broadcasted_iota(jnp.int32, (MOBA_BLOCK, MOBA_BLOCK), 0)
    qry_pos = lax.broadcasted_iota(jnp.int32, (MOBA_BLOCK, MOBA_BLOCK), 1)
    causal = key_pos <= qry_pos
    def next_block_scores(h):
        return lax.dot_general(k_ref[0:MOBA_BLOCK, head(h)], qn_ref[:, head(h)], NT_DIMS,
                               preferred_element_type=F32)

    for h in range(HEADS):
        s = jnp.where(causal, s_ref[h], MASK_VALUE)
        _, l = item(h, own, s, ms[h], ls[h], lambda h=h: next_block_scores(h))
        o_ref[:, head(h)] = (acc_ref[h] * (1.0 / l)).T.astype(o_ref.dtype)


def _moba_attention(q_scaled, k_rot, v_t, bias):
    lanes = HEADS * MOBA_NBLK
    return pl.pallas_call(
        _moba_att_kernel,
        grid=(BATCH, MOBA_NBLK),
        in_specs=[pl.BlockSpec((MOBA_BLOCK, D_MODEL), lambda b, i: (b * MOBA_NBLK + i, 0)),
                  pl.BlockSpec((MOBA_BLOCK, D_MODEL),
                               lambda b, i: (b * MOBA_NBLK + jnp.minimum(i + 1, MOBA_NBLK - 1), 0)),
                  pl.BlockSpec((SEQ, D_MODEL), lambda b, i: (b, 0)),
                  pl.BlockSpec((1, MOBA_NBLK, HEADS * VT_ROWS, MOBA_BLOCK),
                               lambda b, i: (b, 0, 0, 0)),
                  pl.BlockSpec((1, lanes, MOBA_BLOCK), lambda b, i: (b, 0, i))],
        out_specs=pl.BlockSpec((MOBA_BLOCK, D_MODEL), lambda b, i: (b * MOBA_NBLK + i, 0)),
        out_shape=jax.ShapeDtypeStruct((M_TOKENS, D_MODEL), BF16),
        scratch_shapes=[pltpu.VMEM((HEADS, HEAD_DIM, MOBA_BLOCK), F32),
                        pltpu.VMEM((HEADS, MOBA_BLOCK, MOBA_BLOCK), F32)],
        compiler_params=_params(("arbitrary", "arbitrary")),
        name="moba_att",
    )(q_scaled, q_scaled, k_rot, v_t, bias)


HGRN_TILE = MOBA_BLOCK
PM_LEVELS = (1, 2)
HGRN_GROUP = 4


def _hgrn_constants():
    c = HGRN_CHUNK
    idx = np.arange(c)
    tri = (idx[None, :] <= idx[:, None]).astype(np.float32)
    rows = [tri]
    masks = []
    for lvl in range(HGRN_LEVELS):
        half = 1 << lvl
        group = idx // (2 * half)
        upper = (idx & half) != 0
        if lvl in PM_LEVELS:
            boundary = tri[group * (2 * half) + half - 1]
            rows.append(np.where(upper[:, None], tri - boundary, boundary - tri))
        masks.append(((group[:, None] == group[None, :]) & upper[:, None] & ~upper[None, :])
                     .astype(np.float32))
    masks.append(np.eye(c, dtype=np.float32))
    return np.concatenate(rows, axis=0), np.stack(masks)


def _hgrn_tile(q_ref, lf_ref, k_ref, v_ref, og_ref, nw_ref, pm_ref, mask_ref, o_ref, state_ref,
               fill):
    c = HGRN_CHUNK
    n_ch = HGRN_TILE // c
    rows = lambda ch: slice(ch * c, (ch + 1) * c)
    lanes = lambda h: slice(h * HEAD_DIM, (h + 1) * HEAD_DIM)
    chunks_on_lanes = lambda t: jnp.concatenate([t[rows(ch), :] for ch in range(n_ch)], axis=1)

    lf = lf_ref[...]
    hi = lf.astype(BF16)
    rem = lf - hi.astype(F32)
    mid = rem.astype(BF16)
    lo = (rem - mid.astype(F32)).astype(BF16)
    split = jnp.concatenate([chunks_on_lanes(hi), chunks_on_lanes(mid), chunks_on_lanes(lo)], axis=0)
    sums = jnp.dot(pm_ref[...], split, preferred_element_type=F32)

    row = lax.broadcasted_iota(jnp.int32, (c, HEAD_DIM), 0)
    upper = [(row & (1 << lvl)) != 0 for lvl in range(HGRN_LEVELS)]
    t_idx = lax.broadcasted_iota(jnp.int32, (c, c), 0)
    s_idx = lax.broadcasted_iota(jnp.int32, (c, c), 1)
    on_diagonal = (t_idx == s_idx).astype(F32)
    below_diagonal = ((t_idx == s_idx + 1) & ((t_idx & 1) != 0)).astype(F32)

    def decayed_operands(u, ch, h):
        col = slice(u * HEAD_DIM, (u + 1) * HEAD_DIM)
        g = sums[0:c, col]
        q_bf, k_bf = q_ref[rows(ch), lanes(h)], k_ref[rows(ch), lanes(h)]
        qf, kk = q_bf.astype(F32), k_bf.astype(F32)
        same = jnp.sum(qf * kk, axis=-1, keepdims=True)
        q_dec = jnp.where(upper[0], qf * jnp.exp2(lf_ref[rows(ch), lanes(h)]), 0.0)
        prev = jnp.sum(q_dec * pltpu.roll(kk, 1, 0), axis=-1, keepdims=True)
        near = (same * on_diagonal + prev * below_diagonal).astype(BF16)
        zs = []
        for lvl in range(1, HGRN_LEVELS):
            half = 1 << lvl
            if lvl in PM_LEVELS:
                i = PM_LEVELS.index(lvl)
                e = sums[(i + 1) * c:(i + 2) * c, col]
            else:
                boundary = jnp.concatenate(
                    [jnp.broadcast_to(g[b + half - 1:b + half, :], (2 * half, HEAD_DIM))
                     for b in range(0, c, 2 * half)], axis=0)
                e = -jnp.abs(g - boundary)
            zs.append((jnp.where(upper[lvl], qf, kk) * jnp.exp2(e)).astype(BF16))
        g_last = g[c - 1:c, :]
        return dict(near=near, zs=zs,
                    q_in=(qf * jnp.exp2(g)).astype(BF16),
                    k_out=(kk * jnp.exp2(g_last - g)).astype(BF16),
                    decay=jnp.exp2(g_last),
                    v=v_ref[rows(ch), lanes(h)])

    def state_free_matmuls(op):
        score = lambda lhs, rhs: lax.dot_general(lhs, rhs, NT_DIMS,
                                                 preferred_element_type=F32).astype(BF16)
        a = op["near"]
        for lvl, z in enumerate(op["zs"], start=1):
            a = a + score(z, z) * mask_ref[lvl]
        op["o"] = jnp.dot(a, op["v"], preferred_element_type=F32)
        op["update"] = lax.dot_general(op["v"], op["k_out"], TN_DIMS, preferred_element_type=F32)

    for first_head in range(0, HEADS, HGRN_GROUP):
        heads = range(first_head, first_head + HGRN_GROUP)
        group = [(ch, h) for ch in range(n_ch) for h in heads]
        ops = {}
        for i, (ch, h) in enumerate(group):
            ops[ch, h] = decayed_operands(ch * HEADS + h, ch, h)
            if i >= 1:
                state_free_matmuls(ops[group[i - 1]])
            fill()
        state_free_matmuls(ops[group[-1]])

        for h in heads:
            state_t = state_ref[h]
            for ch in range(n_ch):
                op = ops[ch, h]
                op["state"] = state_t.T.astype(BF16)
                state_t = state_t * op["decay"] + op["update"]
            state_ref[h] = state_t
        for ch, h in group:
            op = ops[ch, h]
            o = op["o"] + jnp.dot(op["q_in"], op["state"], preferred_element_type=F32)
            ms = jnp.mean(o * o, axis=-1, keepdims=True)
            y = o * lax.rsqrt(ms + RMS_EPS) * nw_ref[:, lanes(h)] * og_ref[rows(ch), lanes(h)]
            o_ref[rows(ch), lanes(h)] = y.astype(o_ref.dtype)
            fill()


MIX_TM = 512


def _mix_kernel(ya_ref, yb_ref, gate_ref, x_ref, wa_ref, wb_ref, wo_ref, lnw_ref, lnb_ref,
                o_ref, ob_ref):
    half = MIX_TM // 2
    first, second = slice(0, half), slice(half, MIX_TM)

    def merged(rows, za):
        zb = jnp.dot(yb_ref[rows, :], wb_ref[...], preferred_element_type=F32)
        return (gate_ref[rows, :D_MODEL] * za + gate_ref[rows, D_MODEL:] * zb).astype(BF16)

    def pre_norm(rows, za):
        mixed = jnp.dot(merged(rows, za), wo_ref[...], preferred_element_type=F32)
        return DN_ALPHA * x_ref[rows, :] + mixed

    def finish(rows, r):
        y = _layer_norm(r, lnw_ref[...], lnb_ref[...])
        o_ref[rows, :] = y
        ob_ref[rows, :] = y.astype(BF16)

    branch_a = lambda rows: jnp.dot(ya_ref[rows, :], wa_ref[...], preferred_element_type=F32)
    r_first = pre_norm(first, branch_a(first))
    za_second = branch_a(second)
    finish(first, r_first)
    finish(second, pre_norm(second, za_second))


def _mix(ya, yb, gates, x, wa, wb, wo, lnw, lnb):
    d = D_MODEL
    row = lambda n: pl.BlockSpec((MIX_TM, n), lambda i: (i, 0))
    return pl.pallas_call(
        _mix_kernel,
        grid=(M_TOKENS // MIX_TM,),
        in_specs=[row(d), row(d), row(2 * d), row(d),
                  _resident((d, d)), _resident((d, d)), _resident((d, d)),
                  _resident((1, d)), _resident((1, d))],
        out_specs=[row(d), row(d)],
        out_shape=[jax.ShapeDtypeStruct((M_TOKENS, d), F32),
                   jax.ShapeDtypeStruct((M_TOKENS, d), BF16)],
        compiler_params=_params(("parallel",)),
        name="mix",
    )(ya, yb, gates, x, wa, wb, wo, lnw, lnb)


FFN_TM = 512
FFN_CHUNK = 1408


def _ffn_kernel(xb_ref, x_ref, win_ref, wd_ref, lnw_ref, lnb_ref, o_ref):
    half = FFN_TM // 2
    n_chunks = D_FF // FFN_CHUNK

    def chunk(rows, c):
        cs = slice(c * FFN_CHUNK, (c + 1) * FFN_CHUNK)
        xb = xb_ref[rows, :]
        up = slice(D_FF + c * FFN_CHUNK, D_FF + (c + 1) * FFN_CHUNK)
        hg = jnp.dot(xb, win_ref[:, cs], preferred_element_type=F32)
        hu = jnp.dot(xb, win_ref[:, up], preferred_element_type=F32)
        act = (hg * _sigmoid(hg) * hu).astype(BF16)
        return jnp.dot(act, wd_ref[cs, :], preferred_element_type=F32)

    def finish(rows, y):
        o_ref[rows, :] = _layer_norm(DN_ALPHA * x_ref[rows, :] + y, lnw_ref[...], lnb_ref[...])

    first, second = slice(0, half), slice(half, FFN_TM)
    y_first = sum(chunk(first, c) for c in range(n_chunks))
    y_second = chunk(second, 0)
    finish(first, y_first)
    for c in range(1, n_chunks):
        y_second = y_second + chunk(second, c)
    finish(second, y_second)


def _ffn(xb, x, w_in, wd, lnw, lnb):
    d = D_MODEL
    row = lambda n: pl.BlockSpec((FFN_TM, n), lambda i: (i, 0))
    return pl.pallas_call(
        _ffn_kernel,
        grid=(M_TOKENS // FFN_TM,),
        in_specs=[row(d), row(d), _resident((d, 2 * D_FF)),
                  _resident((D_FF, d)), _resident((1, d)), _resident((1, d))],
        out_specs=row(d),
        out_shape=jax.ShapeDtypeStruct((M_TOKENS, d), F32),
        compiler_params=_params(("parallel",)),
        name="ffn",
    )(xb, x, w_in, wd, lnw, lnb)


def _rope_frequencies():
    half = HEAD_DIM // 2
    inv_freq = ROPE_THETA ** (-jnp.arange(half, dtype=F32) / half)
    return jnp.concatenate([inv_freq, inv_freq]).reshape(1, HEAD_DIM)


def _layer(x2, w_in, lb_logits, hgrn_norm_w, w_branch_a, w_branch_b, b_gate, w_out,
           ln1_w, ln1_b, w_ffn_in, w_ffn_down, ln2_w, ln2_b):
    d = D_MODEL
    row2 = lambda v: v.reshape(1, -1).astype(F32)

    q_scaled, bias, k_rot, v_t, gates, y_a = _proj(
        x2, w_in.astype(BF16), _rope_frequencies(), row2(b_gate), lb_logits.astype(F32),
        row2(hgrn_norm_w))
    y_b = _moba_attention(q_scaled, k_rot, v_t, bias)

    x1, x1b = _mix(y_a, y_b, gates, x2, w_branch_a.astype(BF16), w_branch_b.astype(BF16),
                   w_out.astype(BF16), row2(ln1_w), row2(ln1_b))
    return _ffn(x1b, x1, w_ffn_in.astype(BF16), w_ffn_down.astype(BF16), row2(ln2_w), row2(ln2_b))


def kernel(x, w_in, lb_logits, hgrn_norm_w, w_branch_a, w_branch_b, b_gate, w_out, ln1_w, ln1_b,
           w_ffn_in, w_ffn_down, ln2_w, ln2_b):
    assert DEPTH == 1
    h = x.reshape(M_TOKENS, D_MODEL)
    h = _layer(h, w_in[0], lb_logits, hgrn_norm_w[0], w_branch_a[0], w_branch_b[0],
               b_gate[0], w_out[0], ln1_w[0], ln1_b[0], w_ffn_in[0], w_ffn_down[0],
               ln2_w[0], ln2_b[0])
    return h.reshape(BATCH, SEQ, D_MODEL)
```

```python
import numpy as np
import jax
import jax.numpy as jnp
from jax import lax
from jax.experimental import pallas as pl
from jax.experimental.pallas import tpu as pltpu

D_MODEL = 1024
BATCH = 4
SEQ = 4096
DEPTH = 1
HEADS = 8
HEAD_DIM = 128
HGRN_CHUNK = 64
HGRN_LEVELS = 6
MOBA_BLOCK = 256
MOBA_NBLK = SEQ // MOBA_BLOCK
MOBA_TOPK = 3
ROPE_THETA = 10000.0
D_FF = 2816
DN_ALPHA = (2.0 * DEPTH) ** 0.25
LN_EPS = 1e-5
RMS_EPS = 1e-6
M_TOKENS = BATCH * SEQ

MASK_VALUE = -1e30
VT_ROWS = HEAD_DIM + 16
LOG2_E = 1.4426950408889634
VMEM_LIMIT = 56 * 1024 * 1024

F32 = jnp.float32
BF16 = jnp.bfloat16
NT_DIMS = (((1,), (1,)), ((), ()))
TN_DIMS = (((0,), (0,)), ((), ()))


def _params(semantics, flags=None):
    return pltpu.CompilerParams(dimension_semantics=semantics, vmem_limit_bytes=VMEM_LIMIT,
                                flags=flags)


def _resident(shape):
    return pl.BlockSpec(shape, lambda *_: (0,) * len(shape), pipeline_mode=pl.Buffered(1))


def _sigmoid(z):
    return 1.0 / (1.0 + jnp.exp(-z))


def _layer_norm(r, w, b):
    mu = jnp.mean(r, axis=-1, keepdims=True)
    d = r - mu
    var = jnp.mean(d * d, axis=-1, keepdims=True)
    return d * lax.rsqrt(var + LN_EPS) * w + b


PROJ_SLAB = 256
FILL_EVERY = 4
N_PROJ = 9 * D_MODEL
COL_HGRN_Q, COL_HGRN_F, COL_HGRN_V, COL_HGRN_GATE = 0, D_MODEL, 2 * D_MODEL, 3 * D_MODEL
COL_MOBA_Q, COL_MOBA_K, COL_MOBA_V, COL_MERGE_GATE = 4 * D_MODEL, 5 * D_MODEL, 6 * D_MODEL, 7 * D_MODEL


def _proj_kernel(x_ref, w_ref, freq_ref,
                 bg_ref, lbl_ref, nw_ref, pm_ref, mask_ref,
                 q_ref, bias_ref, k_ref, vt_ref, gate_ref, ya_ref,
                 km_ref, state_ref, hq_ref, hog_ref, hlf_ref, hk_ref, hv_ref, xb_ref, qf_ref,
                 cos_all_ref, sin_all_ref):
    own = pl.program_id(1)
    lanes = HEADS * MOBA_NBLK

    @pl.when(own == 0)
    def _():
        km_ref[...] = jnp.zeros_like(km_ref)
        state_ref[...] = jnp.zeros_like(state_ref)

    xb_ref[...] = x_ref[...].astype(BF16)
    block_rows = pl.ds(pl.multiple_of(own * MOBA_BLOCK, MOBA_BLOCK), MOBA_BLOCK)

    @pl.when(pl.program_id(0) == 0)
    def _():
        position = own * MOBA_BLOCK + lax.broadcasted_iota(jnp.int32, (MOBA_BLOCK, HEAD_DIM), 0)
        angle = position.astype(F32) * freq_ref[...]
        first_half = lax.broadcasted_iota(jnp.int32, (MOBA_BLOCK, HEAD_DIM), 1) < HEAD_DIM // 2
        cos_all_ref[block_rows, :] = jnp.cos(angle)
        sin_all_ref[block_rows, :] = jnp.where(first_half, -jnp.sin(angle), jnp.sin(angle))

    cos_ref = cos_all_ref.at[block_rows, :]
    sin_ref = sin_all_ref.at[block_rows, :]
    head = lambda h: slice(h * HEAD_DIM, (h + 1) * HEAD_DIM)
    slab = lambda i: slice(i * PROJ_SLAB, (i + 1) * PROJ_SLAB)
    n_slabs = D_MODEL // PROJ_SLAB

    def project(first_col, i):
        cols = slice(first_col + i * PROJ_SLAB, first_col + (i + 1) * PROJ_SLAB)
        return jnp.dot(xb_ref[...], w_ref[:, cols], preferred_element_type=F32)

    rope = lambda t: t * cos_ref[...] + pltpu.roll(t, HEAD_DIM // 2, 1) * sin_ref[...]


    def hgrn_silu(i):
        def run():
            acc = project(COL_HGRN_Q if i < n_slabs else COL_HGRN_GATE, i % n_slabs)
            dst = hq_ref if i < n_slabs else hog_ref
            dst[:, slab(i % n_slabs)] = (acc * _sigmoid(acc)).astype(dst.dtype)
        return run

    def hgrn_forget(i):
        def run():
            l0, l1 = lbl_ref[0:1, slab(i)], lbl_ref[1:2, slab(i)]
            top = jnp.maximum(l0, l1)
            e0, e1 = jnp.exp(l0 - top), jnp.exp(l1 - top)
            lb = e0 / (e0 + e1)
            z = project(COL_HGRN_F, i)
            hlf_ref[:, slab(i)] = jnp.log(lb + (1.0 - lb) * _sigmoid(z)) * LOG2_E
            hk_ref[:, slab(i)] = ((1.0 - lb) * _sigmoid(-z)).astype(hk_ref.dtype)
        return run

    def hgrn_value(i):
        def run():
            hv_ref[:, slab(i)] = project(COL_HGRN_V, i).astype(hv_ref.dtype)
        return run

    heads_of = lambda i: range(i * PROJ_SLAB // HEAD_DIM, (i + 1) * PROJ_SLAB // HEAD_DIM)
    in_slab = lambda i, h: slice((h - heads_of(i)[0]) * HEAD_DIM, (h - heads_of(i)[0] + 1) * HEAD_DIM)

    def moba_query(i):
        def run():
            acc = project(COL_MOBA_Q, i)
            for h in heads_of(i):
                qh = rope(acc[:, in_slab(i, h)])
                qf_ref[:, head(h)] = qh
                q_ref[:, head(h)] = (qh * (HEAD_DIM ** -0.5 * LOG2_E)).astype(q_ref.dtype)
        return run

    def moba_choice():
        split = lambda t: (t.astype(BF16), (t - t.astype(BF16).astype(F32)).astype(BF16))
        q_hi, q_lo = split(qf_ref[...])
        km_hi, km_lo = split(km_ref[...])
        gate = lax.dot_general(jnp.concatenate([q_hi, q_lo, q_hi], axis=1),
                               jnp.concatenate([km_hi, km_hi, km_lo], axis=1), NT_DIMS,
                               preferred_element_type=F32)
        lane = lax.broadcasted_iota(jnp.int32, (MOBA_BLOCK, lanes), 1)
        eligible = lane < own * HEADS
        gate = jnp.where(eligible, gate, -jnp.inf)
        rank = jnp.zeros((MOBA_BLOCK, lanes), F32)
        for r in range(1, MOBA_NBLK):
            partner = pltpu.roll(gate, lanes - r * HEADS, 1)
            wrapped = lane >= lanes - r * HEADS
            beats = (partner > gate) | (wrapped & (partner == gate))
            rank = rank + beats.astype(F32)
        selected = eligible & (rank < float(MOBA_TOPK))
        bias_ref[0] = jnp.where(selected, 0.0, MASK_VALUE).T

    def moba_key(i):
        def run():
            acc = project(COL_MOBA_K, i)
            for h in heads_of(i):
                rot = rope(acc[:, in_slab(i, h)])
                k_ref[:, head(h)] = rot.astype(k_ref.dtype)
                mean = jnp.mean(rot, axis=0, keepdims=True)
                zeros = lambda n: [jnp.zeros((1, n * HEAD_DIM), F32)] if n else []
                km_ref[pl.ds(own * HEADS + h, 1), :] = jnp.concatenate(
                    zeros(h) + [mean] + zeros(HEADS - 1 - h), axis=1)
        return run

    def moba_value(i):
        def run():
            vt = project(COL_MOBA_V, i).T.astype(vt_ref.dtype)
            pad_row = lax.broadcasted_iota(jnp.int32, (VT_ROWS - HEAD_DIM, MOBA_BLOCK), 0)
            ones_row = jnp.where(pad_row == 0, 1.0, 0.0).astype(vt_ref.dtype)
            for h in heads_of(i):
                vt_ref[0, 0, h * VT_ROWS:h * VT_ROWS + HEAD_DIM, :] = vt[in_slab(i, h), :]
                vt_ref[0, 0, h * VT_ROWS + HEAD_DIM:(h + 1) * VT_ROWS, :] = ones_row
        return run

    def merge_gate(i):
        def run():
            acc = project(COL_MERGE_GATE, i) + bg_ref[:, slab(i)]
            gate_ref[:, slab(i)] = _sigmoid(acc).astype(gate_ref.dtype)
        return run

    hgrn_inputs = ([hgrn_silu(i) for i in range(2 * n_slabs)]
                   + [hgrn_forget(i) for i in range(n_slabs)]
                   + [hgrn_value(i) for i in range(n_slabs)])
    queries = [moba_query(i) for i in range(n_slabs)] + [moba_choice]
    for i, piece in enumerate(hgrn_inputs):
        piece()
        if i < len(queries):
            queries[i]()

    rest = ([moba_key(i) for i in range(n_slabs)] + [moba_value(i) for i in range(n_slabs)]
            + [merge_gate(i) for i in range(2 * n_slabs)])
    calls = []

    def fill():
        calls.append(None)
        if rest and len(calls) % FILL_EVERY == 0:
            rest.pop(0)()

    _hgrn_tile(hq_ref, hlf_ref, hk_ref, hv_ref, hog_ref, nw_ref, pm_ref, mask_ref, ya_ref,
               state_ref, fill)
    while rest:
        fill()


def _proj(x, w_in, rope_freq, b_gate, lb_logits, norm_w):
    d = D_MODEL
    lanes = HEADS * MOBA_NBLK
    pm, masks = _hgrn_constants()
    pm = np.tile(pm, (1, 3))
    row = lambda n: pl.BlockSpec((MOBA_BLOCK, n), lambda b, i: (b * MOBA_NBLK + i, 0))
    shape = lambda n, dt: jax.ShapeDtypeStruct((M_TOKENS, n), dt)
    tile = lambda dt: pltpu.VMEM((MOBA_BLOCK, d), dt)
    return pl.pallas_call(
        _proj_kernel,
        grid=(BATCH, MOBA_NBLK),
        in_specs=[row(d), _resident((d, N_PROJ)), _resident((1, HEAD_DIM)), _resident((1, 2 * d)), _resident((DEPTH + 1, d)),
                  _resident((1, d)), _resident(pm.shape), _resident(masks.shape)],
        out_specs=[row(d),
                   pl.BlockSpec((1, lanes, MOBA_BLOCK), lambda b, i: (b, 0, i)),
                   row(d),
                   pl.BlockSpec((1, 1, HEADS * VT_ROWS, MOBA_BLOCK), lambda b, i: (b, i, 0, 0)),
                   row(2 * d), row(d)],
        out_shape=[shape(d, BF16),
                   jax.ShapeDtypeStruct((BATCH, lanes, SEQ), F32),
                   shape(d, BF16),
                   jax.ShapeDtypeStruct((BATCH, MOBA_NBLK, HEADS * VT_ROWS, MOBA_BLOCK), BF16),
                   shape(2 * d, BF16), shape(d, BF16)],
        scratch_shapes=[pltpu.VMEM((lanes, d), F32),
                        pltpu.VMEM((HEADS, HEAD_DIM, HEAD_DIM), F32),
                        tile(BF16), tile(BF16), tile(F32), tile(BF16), tile(BF16),
                        tile(BF16), tile(F32),
                        pltpu.VMEM((SEQ, HEAD_DIM), F32), pltpu.VMEM((SEQ, HEAD_DIM), F32)],
        compiler_params=_params(("arbitrary", "arbitrary")),
        name="proj",
    )(x, w_in, rope_freq, b_gate, lb_logits, norm_w,
      jnp.asarray(pm, BF16), jnp.asarray(masks, BF16))


def _moba_att_kernel(q_ref, qn_ref, k_ref, vt_ref, bias_ref, o_ref, acc_ref, s_ref):
    own = pl.program_id(1)

    def head(h):
        return slice(h * HEAD_DIM, (h + 1) * HEAD_DIM)

    def scores(h, j):
        kj = k_ref[pl.ds(pl.multiple_of(j * MOBA_BLOCK, MOBA_BLOCK), MOBA_BLOCK), head(h)]
        return lax.dot_general(kj, q_ref[:, head(h)], NT_DIMS, preferred_element_type=F32)

    def item(h, j, s, m, l, next_scores, bias=None):
        top = jnp.max(s, axis=0, keepdims=True)
        m_new = jnp.maximum(m, top if bias is None else top + bias)
        a = jnp.exp2(m - m_new)
        p = jnp.exp2(s - (m_new if bias is None else m_new - 2.0 * bias))
        if next_scores is not None:
            s_ref[h] = next_scores()
        pv = jnp.dot(vt_ref[0, j, h * VT_ROWS:(h + 1) * VT_ROWS, :], p.astype(BF16),
                     preferred_element_type=F32)
        acc_ref[h] = a * acc_ref[h] + pv[:HEAD_DIM]
        return m_new, a * l + pv[HEAD_DIM:HEAD_DIM + 1]

    acc_ref[...] = jnp.zeros_like(acc_ref)

    @pl.when(own == 0)
    def _():
        for h in range(HEADS):
            s_ref[h] = scores(h, 0)

    m0 = jnp.full((1, MOBA_BLOCK), MASK_VALUE, F32)
    l0 = jnp.zeros((1, MOBA_BLOCK), F32)

    def body(j, carry):
        ms, ls = list(carry[0]), list(carry[1])
        for h in range(HEADS):
            ms[h], ls[h] = item(h, j, s_ref[h], ms[h], ls[h], lambda h=h: scores(h, j + 1),
                                bias_ref[0, pl.ds(j * HEADS + h, 1), :])
        return tuple(ms), tuple(ls)

    def several(n):
        def run(t, carry):
            for r in range(n):
                carry = body(n * t + r, carry)
            return carry
        return run

    carry = ((m0,) * HEADS, (l0,) * HEADS)
    done = 0
    for n in (8, 4, 2, 1):
        trips = (own - done) // n
        carry = lax.fori_loop(done // n, done // n + trips, several(n), carry)
        done = done + trips * n
    ms, ls = carry

    key_pos = lax.broadcasted_iota(jnp.int32, (MOBA_BLOCK, MOBA_BLOCK), 0)
    qry_pos = lax.broadcasted_iota(jnp.int32, (MOBA_BLOCK, MOBA_BLOCK), 1)
    causal = key_pos <= qry_pos
    def next_block_scores(h):
        return lax.dot_general(k_ref[0:MOBA_BLOCK, head(h)], qn_ref[:, head(h)], NT_DIMS,
                               preferred_element_type=F32)

    for h in range(HEADS):
        s = jnp.where(causal, s_ref[h], MASK_VALUE)
        _, l = item(h, own, s, ms[h], ls[h], lambda h=h: next_block_scores(h))
        o_ref[0, head(h), :] = (acc_ref[h] * (1.0 / l)).astype(o_ref.dtype)


def _moba_attention(q_scaled, k_rot, v_t, bias):
    lanes = HEADS * MOBA_NBLK
    return pl.pallas_call(
        _moba_att_kernel,
        grid=(BATCH, MOBA_NBLK),
        in_specs=[pl.BlockSpec((MOBA_BLOCK, D_MODEL), lambda b, i: (b * MOBA_NBLK + i, 0)),
                  pl.BlockSpec((MOBA_BLOCK, D_MODEL),
                               lambda b, i: (b * MOBA_NBLK + jnp.minimum(i + 1, MOBA_NBLK - 1), 0)),
                  pl.BlockSpec((SEQ, D_MODEL), lambda b, i: (b, 0)),
                  pl.BlockSpec((1, MOBA_NBLK, HEADS * VT_ROWS, MOBA_BLOCK),
                               lambda b, i: (b, 0, 0, 0)),
                  pl.BlockSpec((1, lanes, MOBA_BLOCK), lambda b, i: (b, 0, i))],
        out_specs=pl.BlockSpec((1, D_MODEL, MOBA_BLOCK), lambda b, i: (b * MOBA_NBLK + i, 0, 0)),
        out_shape=jax.ShapeDtypeStruct((BATCH * MOBA_NBLK, D_MODEL, MOBA_BLOCK), BF16),
        scratch_shapes=[pltpu.VMEM((HEADS, HEAD_DIM, MOBA_BLOCK), F32),
                        pltpu.VMEM((HEADS, MOBA_BLOCK, MOBA_BLOCK), F32)],
        compiler_params=_params(("arbitrary", "arbitrary")),
        name="moba_att",
    )(q_scaled, q_scaled, k_rot, v_t, bias)


HGRN_TILE = MOBA_BLOCK
PM_LEVELS = (1, 2)
HGRN_GROUP = 4


def _hgrn_constants():
    c = HGRN_CHUNK
    idx = np.arange(c)
    tri = (idx[None, :] <= idx[:, None]).astype(np.float32)
    rows = [tri]
    masks = []
    for lvl in range(HGRN_LEVELS):
        half = 1 << lvl
        group = idx // (2 * half)
        upper = (idx & half) != 0
        if lvl in PM_LEVELS:
            boundary = tri[group * (2 * half) + half - 1]
            rows.append(np.where(upper[:, None], tri - boundary, boundary - tri))
        masks.append(((group[:, None] == group[None, :]) & upper[:, None] & ~upper[None, :])
                     .astype(np.float32))
    masks.append(np.eye(c, dtype=np.float32))
    return np.concatenate(rows, axis=0), np.stack(masks)


def _hgrn_tile(q_ref, lf_ref, k_ref, v_ref, og_ref, nw_ref, pm_ref, mask_ref, o_ref, state_ref,
               fill):
    c = HGRN_CHUNK
    n_ch = HGRN_TILE // c
    rows = lambda ch: slice(ch * c, (ch + 1) * c)
    lanes = lambda h: slice(h * HEAD_DIM, (h + 1) * HEAD_DIM)
    chunks_on_lanes = lambda t: jnp.concatenate([t[rows(ch), :] for ch in range(n_ch)], axis=1)

    lf = lf_ref[...]
    hi = lf.astype(BF16)
    rem = lf - hi.astype(F32)
    mid = rem.astype(BF16)
    lo = (rem - mid.astype(F32)).astype(BF16)
    split = jnp.concatenate([chunks_on_lanes(hi), chunks_on_lanes(mid), chunks_on_lanes(lo)], axis=0)
    sums = jnp.dot(pm_ref[...], split, preferred_element_type=F32)

    row = lax.broadcasted_iota(jnp.int32, (c, HEAD_DIM), 0)
    upper = [(row & (1 << lvl)) != 0 for lvl in range(HGRN_LEVELS)]
    t_idx = lax.broadcasted_iota(jnp.int32, (c, c), 0)
    s_idx = lax.broadcasted_iota(jnp.int32, (c, c), 1)
    on_diagonal = (t_idx == s_idx).astype(F32)
    below_diagonal = ((t_idx == s_idx + 1) & ((t_idx & 1) != 0)).astype(F32)

    def decayed_operands(u, ch, h):
        col = slice(u * HEAD_DIM, (u + 1) * HEAD_DIM)
        g = sums[0:c, col]
        q_bf, k_bf = q_ref[rows(ch), lanes(h)], k_ref[rows(ch), lanes(h)]
        qf, kk = q_bf.astype(F32), k_bf.astype(F32)
        same = jnp.sum(qf * kk, axis=-1, keepdims=True)
        q_dec = jnp.where(upper[0], qf * jnp.exp2(lf_ref[rows(ch), lanes(h)]), 0.0)
        prev = jnp.sum(q_dec * pltpu.roll(kk, 1, 0), axis=-1, keepdims=True)
        near = (same * on_diagonal + prev * below_diagonal).astype(BF16)
        zs = []
        for lvl in range(1, HGRN_LEVELS):
            half = 1 << lvl
            if lvl in PM_LEVELS:
                i = PM_LEVELS.index(lvl)
                e = sums[(i + 1) * c:(i + 2) * c, col]
            else:
                boundary = jnp.concatenate(
                    [jnp.broadcast_to(g[b + half - 1:b + half, :], (2 * half, HEAD_DIM))
                     for b in range(0, c, 2 * half)], axis=0)
                e = -jnp.abs(g - boundary)
            zs.append((jnp.where(upper[lvl], qf, kk) * jnp.exp2(e)).astype(BF16))
        g_last = g[c - 1:c, :]
        return dict(near=near, zs=zs,
                    q_in=(qf * jnp.exp2(g)).astype(BF16),
                    k_out=(kk * jnp.exp2(g_last - g)).astype(BF16),
                    decay=jnp.exp2(g_last),
                    v=v_ref[rows(ch), lanes(h)])

    def state_free_matmuls(op):
        score = lambda lhs, rhs: lax.dot_general(lhs, rhs, NT_DIMS,
                                                 preferred_element_type=F32).astype(BF16)
        a = op["near"]
        for lvl, z in enumerate(op["zs"], start=1):
            a = a + score(z, z) * mask_ref[lvl]
        op["o"] = jnp.dot(a, op["v"], preferred_element_type=F32)
        op["update"] = lax.dot_general(op["v"], op["k_out"], TN_DIMS, preferred_element_type=F32)

    for first_head in range(0, HEADS, HGRN_GROUP):
        heads = range(first_head, first_head + HGRN_GROUP)
        group = [(ch, h) for ch in range(n_ch) for h in heads]
        ops = {}
        for i, (ch, h) in enumerate(group):
            ops[ch, h] = decayed_operands(ch * HEADS + h, ch, h)
            if i >= 1:
                state_free_matmuls(ops[group[i - 1]])
            fill()
        state_free_matmuls(ops[group[-1]])

        for h in heads:
            state_t = state_ref[h]
            for ch in range(n_ch):
                op = ops[ch, h]
                op["state"] = state_t.T.astype(BF16)
                state_t = state_t * op["decay"] + op["update"]
            state_ref[h] = state_t
        for ch, h in group:
            op = ops[ch, h]
            o = op["o"] + jnp.dot(op["q_in"], op["state"], preferred_element_type=F32)
            ms = jnp.mean(o * o, axis=-1, keepdims=True)
            y = o * lax.rsqrt(ms + RMS_EPS) * nw_ref[:, lanes(h)] * og_ref[rows(ch), lanes(h)]
            o_ref[rows(ch), lanes(h)] = y.astype(o_ref.dtype)
            fill()


MIX_TM = 512


def _mix_kernel(ya_ref, yb_ref, gate_ref, x_ref, wa_ref, wb_ref, wo_ref, lnw_ref, lnb_ref,
                o_ref, ob_ref):
    half = MIX_TM // 2
    first, second = slice(0, half), slice(half, MIX_TM)

    def merged(rows, za):
        zb = lax.dot_general(yb_ref[rows.start // MOBA_BLOCK], wb_ref[...], TN_DIMS,
                             preferred_element_type=F32)
        return (gate_ref[rows, :D_MODEL] * za + gate_ref[rows, D_MODEL:] * zb).astype(BF16)

    def pre_norm(rows, za):
        mixed = jnp.dot(merged(rows, za), wo_ref[...], preferred_element_type=F32)
        return DN_ALPHA * x_ref[rows, :] + mixed

    def finish(rows, r):
        y = _layer_norm(r, lnw_ref[...], lnb_ref[...])
        o_ref[rows, :] = y
        ob_ref[rows, :] = y.astype(BF16)

    branch_a = lambda rows: jnp.dot(ya_ref[rows, :], wa_ref[...], preferred_element_type=F32)
    r_first = pre_norm(first, branch_a(first))
    za_second = branch_a(second)
    finish(first, r_first)
    finish(second, pre_norm(second, za_second))


def _mix(ya, yb, gates, x, wa, wb, wo, lnw, lnb):
    d = D_MODEL
    row = lambda n: pl.BlockSpec((MIX_TM, n), lambda i: (i, 0))
    return pl.pallas_call(
        _mix_kernel,
        grid=(M_TOKENS // MIX_TM,),
        in_specs=[row(d),
                  pl.BlockSpec((MIX_TM // MOBA_BLOCK, d, MOBA_BLOCK), lambda i: (i, 0, 0)),
                  row(2 * d), row(d),
                  _resident((d, d)), _resident((d, d)), _resident((d, d)),
                  _resident((1, d)), _resident((1, d))],
        out_specs=[row(d), row(d)],
        out_shape=[jax.ShapeDtypeStruct((M_TOKENS, d), F32),
                   jax.ShapeDtypeStruct((M_TOKENS, d), BF16)],
        compiler_params=_params(("parallel",)),
        name="mix",
    )(ya, yb, gates, x, wa, wb, wo, lnw, lnb)


FFN_TM = 512
FFN_CHUNK = 1408


def _ffn_kernel(xb_ref, x_ref, win_ref, wd_ref, lnw_ref, lnb_ref, o_ref):
    half = FFN_TM // 2
    n_chunks = D_FF // FFN_CHUNK

    def chunk(rows, c):
        cs = slice(c * FFN_CHUNK, (c + 1) * FFN_CHUNK)
        xb = xb_ref[rows, :]
        up = slice(D_FF + c * FFN_CHUNK, D_FF + (c + 1) * FFN_CHUNK)
        hg = jnp.dot(xb, win_ref[:, cs], preferred_element_type=F32)
        hu = jnp.dot(xb, win_ref[:, up], preferred_element_type=F32)
        act = (hg * _sigmoid(hg) * hu).astype(BF16)
        return jnp.dot(act, wd_ref[cs, :], preferred_element_type=F32)

    def finish(rows, y):
        o_ref[rows, :] = _layer_norm(DN_ALPHA * x_ref[rows, :] + y, lnw_ref[...], lnb_ref[...])

    first, second = slice(0, half), slice(half, FFN_TM)
    y_first = sum(chunk(first, c) for c in range(n_chunks))
    y_second = chunk(second, 0)
    finish(first, y_first)
    for c in range(1, n_chunks):
        y_second = y_second + chunk(second, c)
    finish(second, y_second)


def _ffn(xb, x, w_in, wd, lnw, lnb):
    d = D_MODEL
    row = lambda n: pl.BlockSpec((FFN_TM, n), lambda i: (i, 0))
    return pl.pallas_call(
        _ffn_kernel,
        grid=(M_TOKENS // FFN_TM,),
        in_specs=[row(d), row(d), _resident((d, 2 * D_FF)),
                  _resident((D_FF, d)), _resident((1, d)), _resident((1, d))],
        out_specs=row(d),
        out_shape=jax.ShapeDtypeStruct((M_TOKENS, d), F32),
        compiler_params=_params(("parallel",)),
        name="ffn",
    )(xb, x, w_in, wd, lnw, lnb)


def _rope_frequencies():
    half = HEAD_DIM // 2
    inv_freq = ROPE_THETA ** (-jnp.arange(half, dtype=F32) / half)
    return jnp.concatenate([inv_freq, inv_freq]).reshape(1, HEAD_DIM)


def _layer(x2, w_in, lb_logits, hgrn_norm_w, w_branch_a, w_branch_b, b_gate, w_out,
           ln1_w, ln1_b, w_ffn_in, w_ffn_down, ln2_w, ln2_b):
    d = D_MODEL
    row2 = lambda v: v.reshape(1, -1).astype(F32)

    q_scaled, bias, k_rot, v_t, gates, y_a = _proj(
        x2, w_in.astype(BF16), _rope_frequencies(), row2(b_gate), lb_logits.astype(F32),
        row2(hgrn_norm_w))
    y_b = _moba_attention(q_scaled, k_rot, v_t, bias)

    x1, x1b = _mix(y_a, y_b, gates, x2, w_branch_a.astype(BF16), w_branch_b.astype(BF16),
                   w_out.astype(BF16), row2(ln1_w), row2(ln1_b))
    return _ffn(x1b, x1, w_ffn_in.astype(BF16), w_ffn_down.astype(BF16), row2(ln2_w), row2(ln2_b))


def kernel(x, w_in, lb_logits, hgrn_norm_w, w_branch_a, w_branch_b, b_gate, w_out, ln1_w, ln1_b,
           w_ffn_in, w_ffn_down, ln2_w, ln2_b):
    assert DEPTH == 1
    h = x.reshape(M_TOKENS, D_MODEL)
    h = _layer(h, w_in[0], lb_logits, hgrn_norm_w[0], w_branch_a[0], w_branch_b[0],
               b_gate[0], w_out[0], ln1_w[0], ln1_b[0], w_ffn_in[0], w_ffn_down[0],
               ln2_w[0], ln2_b[0])
    return h.reshape(BATCH, SEQ, D_MODEL)
```
